```python
import math
import jax
import jax.numpy as jnp
from jax import lax
import numpy as np


D_MODEL = 1024
BATCH = 4
SEQ = 8192
DEPTH = 4

PLE_DIM = 256
D_FF = 2816
ROPE_THETA = 10000.0
EPS = 1e-6
NEG = -1e30
Q_BLOCK = 128

S5_WIDTH = 512
S5_GROUP = 16
S5_GROUPS = S5_WIDTH // S5_GROUP
S5_STATE = 64

NSA_HEADS = 8
NSA_KV_HEADS = 2
NSA_GQA = NSA_HEADS // NSA_KV_HEADS
NSA_HEAD_DIM = 64
CMP_LEN = 32
CMP_STRIDE = 16
SEL_BLOCK = 64
SEL_TOP = 16
WINDOW = 512
FORCE_BONUS = 1000.0
NSA_Q = NSA_HEADS * NSA_HEAD_DIM
NSA_KV = NSA_KV_HEADS * NSA_HEAD_DIM
EVEN_SIZES = (S5_WIDTH, NSA_Q, NSA_KV, NSA_KV, NSA_KV, NSA_KV, NSA_KV, NSA_KV, 3 * NSA_HEADS)
EVEN_IN = sum(EVEN_SIZES)
EVEN_MIX = S5_WIDTH + NSA_Q

MLA_HEADS = 16
MLA_LATENT = 256
MLA_ROPE = 32
MLA_V_DIM = 64
IDX_HEADS = 8
IDX_DIM = 64
DSA_TOP = 256
ODD_SIZES = (MLA_HEADS * MLA_LATENT, MLA_HEADS * MLA_ROPE, MLA_LATENT, MLA_ROPE, IDX_HEADS * IDX_DIM, IDX_DIM, IDX_HEADS)
ODD_IN = sum(ODD_SIZES)
ODD_MIX = MLA_HEADS * MLA_V_DIM

kernel_name = "hybrid_s5_nsa_dsa_macaron_trunk"


def rms_norm(t, g):
    tf = t.astype(jnp.float32)
    y = tf * lax.rsqrt(jnp.mean(tf * tf, axis=-1, keepdims=True) + EPS)
    return (y * g.astype(jnp.float32)).astype(t.dtype)


def rope(t, pos):
    half = t.shape[-1] // 2
    inv = ROPE_THETA ** (-jnp.arange(half, dtype=jnp.float32) / half)
    ang = pos.astype(jnp.float32)[:, None] * inv[None, :]
    cos = jnp.cos(ang)[:, None, :]
    sin = jnp.sin(ang)[:, None, :]
    tf = t.astype(jnp.float32)
    t1, t2 = tf[..., :half], tf[..., half:]
    return jnp.concatenate([t1 * cos - t2 * sin, t2 * cos + t1 * sin], axis=-1).astype(t.dtype)


def masked_softmax(s, mask):
    return jax.nn.softmax(jnp.where(mask, s.astype(jnp.float32), NEG), axis=-1)


def swiglu(t, w_in, w_out):
    g, u = jnp.split(t @ w_in, 2, axis=-1)
    return (jax.nn.silu(g) * u) @ w_out


def _split(z, sizes):
    cuts = [int(c) for c in np.cumsum(sizes)[:-1]]
    return jnp.split(z, cuts, axis=-1)


def _linear_recurrence(c1, c2):
    a1, b1 = c1
    a2, b2 = c2
    return a1 * a2, a2 * b1 + b2


def s5_mixer(u, a_re, a_im, b_re, b_im, c_re, c_im, log_dt, d_skip, w_glu):
    f32 = jnp.float32
    bsz, seq_len, _ = u.shape
    uf = u.astype(f32)
    ug = uf.reshape(bsz, seq_len, S5_GROUPS, S5_GROUP)
    lam = lax.complex(a_re.astype(f32), a_im.astype(f32))
    dt = jnp.exp(log_dt.astype(f32))[:, None]
    lam_bar = jnp.exp(lam * dt)
    b_bar = ((lam_bar - 1.0) / lam)[..., None] * lax.complex(b_re.astype(f32), b_im.astype(f32))
    bu = jnp.einsum('blgp,gnp->blgn', ug.astype(jnp.complex64), b_bar)
    a = jnp.broadcast_to(lam_bar, (1, seq_len) + lam_bar.shape)
    _, states = lax.associative_scan(_linear_recurrence, (a, bu), axis=1)
    y = (jnp.einsum('blgn,gpn->blgp', states.real, c_re.astype(f32))
         - jnp.einsum('blgn,gpn->blgp', states.imag, c_im.astype(f32)))
    y = y.reshape(bsz, seq_len, S5_WIDTH) + d_skip.astype(f32) * uf
    z = jax.nn.gelu(y)
    z = z * jax.nn.sigmoid(z @ w_glu.astype(f32))
    return z.astype(u.dtype)


def nsa_mixer(q, kc, vc, ks, vs, kw, vw, gates, pe_k, pe_v, wk1, wk2, wv1, wv2):
    bsz, seq_len = q.shape[0], q.shape[1]
    n_cmp = (seq_len - CMP_LEN) // CMP_STRIDE + 1
    n_sb = seq_len // SEL_BLOCK
    top_n = min(SEL_TOP, n_sb)
    n_sel = top_n * SEL_BLOCK
    scale = NSA_HEAD_DIM ** -0.5
    cidx = jnp.arange(n_cmp)[:, None] * CMP_STRIDE + jnp.arange(CMP_LEN)[None, :]

    def compress(t, pe, w1, w2):
        blk = t[:, cidx] + pe[:, None, :]
        blk = blk.transpose(0, 1, 3, 2, 4).reshape(bsz, n_cmp, NSA_KV_HEADS, CMP_LEN * NSA_HEAD_DIM)
        return jax.nn.gelu(blk @ w1) @ w2

    k_cmp = compress(kc, pe_k, wk1, wk2)
    v_cmp = compress(vc, pe_v, wv1, wv2)
    cmp_start = jnp.arange(n_cmp) * CMP_STRIDE
    cmp_last = cmp_start + CMP_LEN - 1
    sb_ids = jnp.arange(n_sb)
    sb_start = sb_ids * SEL_BLOCK
    overlap = ((cmp_start[:, None] < sb_start[None, :] + SEL_BLOCK)
               & (cmp_start[:, None] + CMP_LEN > sb_start[None, :])).astype(jnp.float32)
    k_sb = ks.reshape(bsz, n_sb, SEL_BLOCK, NSA_KV_HEADS, NSA_HEAD_DIM).transpose(0, 3, 1, 2, 4)
    v_sb = vs.reshape(bsz, n_sb, SEL_BLOCK, NSA_KV_HEADS, NSA_HEAD_DIM).transpose(0, 3, 1, 2, 4)
    kw_pad = jnp.pad(kw, ((0, 0), (WINDOW, 0), (0, 0), (0, 0)))
    vw_pad = jnp.pad(vw, ((0, 0), (WINDOW, 0), (0, 0), (0, 0)))
    qg = q.reshape(bsz, seq_len, NSA_KV_HEADS, NSA_GQA, NSA_HEAD_DIM)
    gg = gates.reshape(bsz, seq_len, NSA_KV_HEADS, NSA_GQA, 3)
    b_ix = jnp.arange(bsz)[:, None, None, None]
    h_ix = jnp.arange(NSA_KV_HEADS)[None, :, None, None]

    def block(i):
        s0 = i * Q_BLOCK
        t_pos = s0 + jnp.arange(Q_BLOCK)
        qb = lax.dynamic_slice_in_dim(qg, s0, Q_BLOCK, axis=1)
        gb = lax.dynamic_slice_in_dim(gg, s0, Q_BLOCK, axis=1)
        m_c = cmp_last[None, :] <= t_pos[:, None]
        p_c = masked_softmax(jnp.einsum('bqhgd,bchd->bhgqc', qb, k_cmp) * scale, m_c) * m_c
        o_c = jnp.einsum('bhgqc,bchd->bqhgd', p_c.astype(v_cmp.dtype), v_cmp)
        imp = jnp.einsum('bhgqc,cs->bhqs', p_c, overlap)
        cur = (t_pos // SEL_BLOCK)[:, None]
        forced = (sb_ids[None, :] == 0) | (sb_ids[None, :] == cur) | (sb_ids[None, :] == cur - 1)
        imp = imp + FORCE_BONUS * forced
        imp = jnp.where(sb_start[None, :] <= t_pos[:, None], imp, NEG)
        _, sel = lax.top_k(imp, top_n)
        k_sel = k_sb[b_ix, h_ix, sel].reshape(bsz, NSA_KV_HEADS, Q_BLOCK, n_sel, NSA_HEAD_DIM)
        v_sel = v_sb[b_ix, h_ix, sel].reshape(bsz, NSA_KV_HEADS, Q_BLOCK, n_sel, NSA_HEAD_DIM)
        k_pos = (sel[..., None] * SEL_BLOCK + jnp.arange(SEL_BLOCK)).reshape(bsz, NSA_KV_HEADS, Q_BLOCK, n_sel)
        m_s = (k_pos <= t_pos[None, None, :, None])[:, :, None]
        p_s = masked_softmax(jnp.einsum('bqhgd,bhqkd->bhgqk', qb, k_sel) * scale, m_s)
        o_s = jnp.einsum('bhgqk,bhqkd->bqhgd', p_s.astype(v_sel.dtype), v_sel)
        kwb = lax.dynamic_slice_in_dim(kw_pad, s0, WINDOW + Q_BLOCK, axis=1)
        vwb = lax.dynamic_slice_in_dim(vw_pad, s0, WINDOW + Q_BLOCK, axis=1)
        s_pos = s0 - WINDOW + jnp.arange(WINDOW + Q_BLOCK)
        diff = t_pos[:, None] - s_pos[None, :]
        m_w = (diff >= 0) & (diff < WINDOW) & (s_pos[None, :] >= 0)
        p_w = masked_softmax(jnp.einsum('bqhgd,bkhd->bhgqk', qb, kwb) * scale, m_w)
        o_w = jnp.einsum('bhgqk,bkhd->bqhgd', p_w.astype(vwb.dtype), vwb)
        return gb[..., 0:1] * o_c + gb[..., 1:2] * o_s + gb[..., 2:3] * o_w

    out = lax.map(block, jnp.arange(seq_len // Q_BLOCK))
    return out.transpose(1, 0, 2, 3, 4, 5).reshape(bsz, seq_len, NSA_Q)


def dsa_mixer(q_lat, q_rope, c_kv, k_rope, q_idx, k_idx, w_idx, w_uv):
    bsz, seq_len = q_lat.shape[0], q_lat.shape[1]
    top = min(DSA_TOP, seq_len // 4)
    scale = (MLA_LATENT + MLA_ROPE) ** -0.5
    idx_scale = IDX_DIM ** -0.5
    w_scale = IDX_HEADS ** -0.5
    key_pos = jnp.arange(seq_len)
    b_ix = jnp.arange(bsz)[:, None, None]

    def block(i):
        s0 = i * Q_BLOCK
        t_pos = s0 + jnp.arange(Q_BLOCK)
        qi = lax.dynamic_slice_in_dim(q_idx, s0, Q_BLOCK, axis=1)
        wi = lax.dynamic_slice_in_dim(w_idx, s0, Q_BLOCK, axis=1)
        dots = jax.nn.relu(jnp.einsum('bqhd,bsd->bqhs', qi, k_idx).astype(jnp.float32) * idx_scale)
        score = jnp.einsum('bqh,bqhs->bqs', wi.astype(jnp.float32) * w_scale, dots)
        score = jnp.where(key_pos[None, :] <= t_pos[:, None], score, NEG)
        _, sel = lax.top_k(score, top)
        ck = c_kv[b_ix, sel]
        kr = k_rope[b_ix, sel]
        m = (sel <= t_pos[None, :, None])[:, :, None, :]
        ql = lax.dynamic_slice_in_dim(q_lat, s0, Q_BLOCK, axis=1)
        qr = lax.dynamic_slice_in_dim(q_rope, s0, Q_BLOCK, axis=1)
        s = (jnp.einsum('bqhr,bqkr->bqhk', ql, ck) + jnp.einsum('bqhe,bqke->bqhk', qr, kr)) * scale
        p = masked_softmax(s, m)
        o = jnp.einsum('bqhk,bqkr->bqhr', p.astype(ck.dtype), ck)
        return jnp.einsum('bqhr,hrv->bqhv', o, w_uv)

    out = lax.map(block, jnp.arange(seq_len // Q_BLOCK))
    return out.transpose(1, 0, 2, 3, 4).reshape(bsz, seq_len, ODD_MIX)


def even_mixer(hn, pos, w_in, w_out, a_re, a_im, b_re, b_im, c_re, c_im, log_dt, d_skip, w_glu,
               pe_k, pe_v, wk1, wk2, wv1, wv2):
    bsz, seq_len, _ = hn.shape
    u, q, kc, vc, ks, vs, kw, vw, gt = _split(hn @ w_in, EVEN_SIZES)
    q = rope(q.reshape(bsz, seq_len, NSA_HEADS, NSA_HEAD_DIM), pos)
    kv_shape = (bsz, seq_len, NSA_KV_HEADS, NSA_HEAD_DIM)
    kc = rope(kc.reshape(kv_shape), pos)
    ks = rope(ks.reshape(kv_shape), pos)
    kw = rope(kw.reshape(kv_shape), pos)
    gates = jax.nn.sigmoid(gt.reshape(bsz, seq_len, NSA_HEADS, 3))
    a_out = s5_mixer(u, a_re, a_im, b_re, b_im, c_re, c_im, log_dt, d_skip, w_glu)
    b_out = nsa_mixer(q, kc, vc.reshape(kv_shape), ks, vs.reshape(kv_shape), kw, vw.reshape(kv_shape),
                      gates, pe_k, pe_v, wk1, wk2, wv1, wv2)
    return jnp.concatenate([a_out, b_out], axis=-1) @ w_out


def odd_mixer(hn, pos, w_in, kv_norm, w_uv, w_out):
    bsz, seq_len, _ = hn.shape
    q_lat, q_rope, c_kv, k_rope, q_idx, k_idx, w_idx = _split(hn @ w_in, ODD_SIZES)
    q_lat = q_lat.reshape(bsz, seq_len, MLA_HEADS, MLA_LATENT)
    q_rope = rope(q_rope.reshape(bsz, seq_len, MLA_HEADS, MLA_ROPE), pos)
    c_kv = rms_norm(c_kv, kv_norm)
    k_rope = rope(k_rope[:, :, None, :], pos)[:, :, 0, :]
    q_idx = rope(q_idx.reshape(bsz, seq_len, IDX_HEADS, IDX_DIM), pos)
    k_idx = rope(k_idx[:, :, None, :], pos)[:, :, 0, :]
    o = dsa_mixer(q_lat, q_rope, c_kv, k_rope, q_idx, k_idx, w_idx, w_uv)
    return o @ w_out


def setup_inputs(seed: int = 0) -> dict:
    key = jax.random.key(seed)
    ks = jax.random.split(key, 32)
    n_even = (DEPTH + 1) // 2
    n_odd = DEPTH // 2
    nrm = jax.random.normal
    f32 = jnp.float32

    def w(k, shape, fan_in):
        return nrm(k, shape, f32) * (fan_in ** -0.5)

    n_idx = jnp.arange(S5_STATE, dtype=f32)
    return {
        "x": nrm(ks[0], (BATCH, SEQ, D_MODEL), f32),
        "p": nrm(ks[1], (DEPTH, BATCH, SEQ, PLE_DIM), f32),
        "norm_g": 1.0 + 0.02 * nrm(ks[2], (DEPTH, 8, D_MODEL), f32),
        "ffn1_w_in": w(ks[3], (DEPTH, D_MODEL, 2 * D_FF), D_MODEL),
        "ffn1_w_out": w(ks[4], (DEPTH, D_FF, D_MODEL), D_FF),
        "ffn2_w_in": w(ks[5], (DEPTH, D_MODEL, 2 * D_FF), D_MODEL),
        "ffn2_w_out": w(ks[6], (DEPTH, D_FF, D_MODEL), D_FF),
        "ple_w_gate": w(ks[7], (DEPTH, D_MODEL, D_MODEL), D_MODEL),
        "ple_w_proj": w(ks[8], (DEPTH, PLE_DIM, D_MODEL), PLE_DIM),
        "ev_w_in": w(ks[9], (n_even, D_MODEL, EVEN_IN), D_MODEL),
        "ev_w_out": w(ks[10], (n_even, EVEN_MIX, D_MODEL), EVEN_MIX),
        "s5_a_re": -0.5 + 0.01 * nrm(ks[11], (n_even, S5_GROUPS, S5_STATE), f32),
        "s5_a_im": math.pi * n_idx + 0.01 * nrm(ks[12], (n_even, S5_GROUPS, S5_STATE), f32),
        "s5_b_re": w(ks[13], (n_even, S5_GROUPS, S5_STATE, S5_GROUP), 2 * S5_GROUP),
        "s5_b_im": w(ks[14], (n_even, S5_GROUPS, S5_STATE, S5_GROUP), 2 * S5_GROUP),
        "s5_c_re": w(ks[15], (n_even, S5_GROUPS, S5_GROUP, S5_STATE), 2 * S5_STATE),
        "s5_c_im": w(ks[16], (n_even, S5_GROUPS, S5_GROUP, S5_STATE), 2 * S5_STATE),
        "s5_log_dt": jax.random.uniform(ks[17], (n_even, S5_GROUPS), f32, math.log(1e-3), math.log(1e-1)),
        "s5_d": nrm(ks[18], (n_even, S5_WIDTH), f32),
        "s5_w_glu": w(ks[19], (n_even, S5_WIDTH, S5_WIDTH), S5_WIDTH),
        "nsa_pe_k": 0.1 * nrm(ks[20], (n_even, CMP_LEN, NSA_HEAD_DIM), f32),
        "nsa_pe_v": 0.1 * nrm(ks[21], (n_even, CMP_LEN, NSA_HEAD_DIM), f32),
        "nsa_wk1": w(ks[22], (n_even, CMP_LEN * NSA_HEAD_DIM, NSA_HEAD_DIM), CMP_LEN * NSA_HEAD_DIM),
        "nsa_wk2": w(ks[23], (n_even, NSA_HEAD_DIM, NSA_HEAD_DIM), NSA_HEAD_DIM),
        "nsa_wv1": w(ks[24], (n_even, CMP_LEN * NSA_HEAD_DIM, NSA_HEAD_DIM), CMP_LEN * NSA_HEAD_DIM),
        "nsa_wv2": w(ks[25], (n_even, NSA_HEAD_DIM, NSA_HEAD_DIM), NSA_HEAD_DIM),
        "od_w_in": w(ks[26], (n_odd, D_MODEL, ODD_IN), D_MODEL),
        "od_kv_norm": 1.0 + 0.02 * nrm(ks[27], (n_odd, MLA_LATENT), f32),
        "od_w_uv": w(ks[28], (n_odd, MLA_HEADS, MLA_LATENT, MLA_V_DIM), MLA_LATENT),
        "od_w_out": w(ks[29], (n_odd, ODD_MIX, D_MODEL), ODD_MIX),
    }


def reference(x, p, norm_g, ffn1_w_in, ffn1_w_out, ffn2_w_in, ffn2_w_out, ple_w_gate, ple_w_proj,
              ev_w_in, ev_w_out, s5_a_re, s5_a_im, s5_b_re, s5_b_im, s5_c_re, s5_c_im, s5_log_dt, s5_d, s5_w_glu,
              nsa_pe_k, nsa_pe_v, nsa_wk1, nsa_wk2, nsa_wv1, nsa_wv2,
              od_w_in, od_kv_norm, od_w_uv, od_w_out):
    seq_len = x.shape[1]
    pos = jnp.arange(seq_len)
    h = x
    for i in range(DEPTH):
        g = norm_g[i]
        j = i // 2
        h = h + 0.5 * rms_norm(swiglu(rms_norm(h, g[0]), ffn1_w_in[i], ffn1_w_out[i]), g[1])
        hn = rms_norm(h, g[2])
        if i % 2 == 0:
            mix = even_mixer(hn, pos, ev_w_in[j], ev_w_out[j], s5_a_re[j], s5_a_im[j], s5_b_re[j], s5_b_im[j],
                             s5_c_re[j], s5_c_im[j], s5_log_dt[j], s5_d[j], s5_w_glu[j],
                             nsa_pe_k[j], nsa_pe_v[j], nsa_wk1[j], nsa_wk2[j], nsa_wv1[j], nsa_wv2[j])
        else:
            mix = odd_mixer(hn, pos, od_w_in[j], od_kv_norm[j], od_w_uv[j], od_w_out[j])
        h = h + rms_norm(mix, g[3])
        h = h + 0.5 * rms_norm(swiglu(rms_norm(h, g[4]), ffn2_w_in[i], ffn2_w_out[i]), g[5])
        gate = jax.nn.sigmoid(rms_norm(h, g[6]) @ ple_w_gate[i])
        h = h + rms_norm((p[i] @ ple_w_proj[i]) * gate, g[7])
    return h
```

```python
import functools
import math

import jax
import jax.numpy as jnp
from jax import lax
from jax.experimental import pallas as pl
from jax.experimental.pallas import tpu as pltpu

F32 = jnp.float32
BF16 = jnp.bfloat16
I32 = jnp.int32

ROPE_THETA = 10000.0
EPS = 1e-6
NEG = -1e30
D_FF = 2816
S5_WIDTH = 512
S5_GROUP = 16
S5_GROUPS = S5_WIDTH // S5_GROUP
S5_STATE = 64
NSA_HEADS = 8
NSA_KV_HEADS = 2
NSA_GQA = NSA_HEADS // NSA_KV_HEADS
NSA_HEAD_DIM = 64
CMP_LEN = 32
CMP_STRIDE = 16
SEL_BLOCK = 64
SEL_TOP = 16
WINDOW = 512
FORCE_BONUS = 1000.0
NSA_Q = NSA_HEADS * NSA_HEAD_DIM
NSA_KV = NSA_KV_HEADS * NSA_HEAD_DIM
MLA_HEADS = 16
MLA_LATENT = 256
MLA_ROPE = 32
MLA_QK = MLA_LATENT + MLA_ROPE
MLA_V_DIM = 64
IDX_HEADS = 8
IDX_DIM = 64
DSA_TOP = 256

V7X_VMEM_BYTES = 64 * 2**20
VMEM_LIMIT = V7X_VMEM_BYTES - 8 * 2**20
LANE = 128
TM_FFN = 512
TQ_PROJ = 256
TQ_NSA = 128
TK_NSA = 512
TQ_DSA = 256
S5_CHUNK = 64


def _cparams(sem):
    return pltpu.CompilerParams(dimension_semantics=sem, vmem_limit_bytes=VMEM_LIMIT)


def _resident(shape):
    nd = len(shape)
    return pl.BlockSpec(shape, lambda *_: (0,) * nd, pipeline_mode=pl.Buffered(1))


def _dot(a, b):
    return jnp.dot(a, b, preferred_element_type=F32)


def _dot_nt(a, b):
    return lax.dot_general(a, b, (((1,), (1,)), ((), ())), preferred_element_type=F32)


def _rms_rows(x, g):
    return x * lax.rsqrt(jnp.mean(x * x, axis=-1, keepdims=True) + EPS) * g


def _rope_fmaj(y, cos, sin):
    half = y.shape[1] // 2
    t1 = y[:, :half, :]
    t2 = y[:, half:, :]
    return jnp.concatenate([t1 * cos - t2 * sin, t2 * cos + t1 * sin], axis=1)


def _ffn_kernel(h_ref, g_ref, win_ref, wout_ref, o_ref, *, g0, n_chunk):
    x = h_ref[...]
    xn = _rms_rows(x, g_ref[g0:g0 + 1, :]).astype(BF16)
    ck = D_FF // n_chunk
    acc = None
    for c in range(n_chunk):
        a = _dot(xn, win_ref[:, c * ck:(c + 1) * ck])
        u = _dot(xn, win_ref[:, D_FF + c * ck:D_FF + (c + 1) * ck])
        act = (jax.nn.silu(a) * u).astype(BF16)
        y = _dot(act, wout_ref[c * ck:(c + 1) * ck, :])
        acc = y if acc is None else acc + y
    o_ref[...] = x + 0.5 * _rms_rows(acc, g_ref[g0 + 1:g0 + 2, :])


def _ffn(h, g, w_in, w_out, g0):
    t, d = h.shape
    tm = min(TM_FFN, t)
    return pl.pallas_call(
        functools.partial(_ffn_kernel, g0=g0, n_chunk=2),
        grid=(t // tm,),
        in_specs=[pl.BlockSpec((tm, d), lambda i: (i, 0)), _resident(g.shape),
                  _resident(w_in.shape), _resident(w_out.shape)],
        out_specs=pl.BlockSpec((tm, d), lambda i: (i, 0)),
        out_shape=jax.ShapeDtypeStruct((t, d), F32),
        compiler_params=_cparams(("parallel",)),
        name="ffn",
    )(h, g, w_in, w_out)


def _ple_kernel(h_ref, p_ref, g_ref, wg_ref, wp_ref, o_ref):
    x = h_ref[...]
    gate = jax.nn.sigmoid(_dot(_rms_rows(x, g_ref[6:7, :]).astype(BF16), wg_ref[...]))
    e = _dot(p_ref[...].astype(BF16), wp_ref[...]) * gate
    o_ref[...] = x + _rms_rows(e, g_ref[7:8, :])


def _ple(h, p, g, w_gate, w_proj):
    t, d = h.shape
    tm = min(TM_FFN, t)
    return pl.pallas_call(
        _ple_kernel,
        grid=(t // tm,),
        in_specs=[pl.BlockSpec((tm, d), lambda i: (i, 0)),
                  pl.BlockSpec((tm, p.shape[1]), lambda i: (i, 0)),
                  _resident(g.shape), _resident(w_gate.shape), _resident(w_proj.shape)],
        out_specs=pl.BlockSpec((tm, d), lambda i: (i, 0)),
        out_shape=jax.ShapeDtypeStruct((t, d), F32),
        compiler_params=_cparams(("parallel",)),
        name="ple",
    )(h, p, g, w_gate, w_proj)


def _outproj_odd_kernel(h_ref, mix_ref, g_ref, w_ref, o_ref):
    y = _dot(mix_ref[...].astype(BF16), w_ref[...])
    o_ref[...] = h_ref[...] + _rms_rows(y, g_ref[3:4, :])


def _outproj_odd(h, mix, g, w_out):
    t, d = h.shape
    tm = min(TM_FFN, t)
    return pl.pallas_call(
        _outproj_odd_kernel,
        grid=(t // tm,),
        in_specs=[pl.BlockSpec((tm, d), lambda i: (i, 0)),
                  pl.BlockSpec((tm, mix.shape[1]), lambda i: (i, 0)),
                  _resident(g.shape), _resident(w_out.shape)],
        out_specs=pl.BlockSpec((tm, d), lambda i: (i, 0)),
        out_shape=jax.ShapeDtypeStruct((t, d), F32),
        compiler_params=_cparams(("parallel",)),
        name="outproj_odd",
    )(h, mix, g, w_out)


def _outproj_even_kernel(h_ref, ys_ref, u_ref, b_ref, g_ref, d_ref, wglu_ref, wa_ref, wb_ref, o_ref):
    y = ys_ref[...] + d_ref[...] * u_ref[...]
    z = jax.nn.gelu(y)
    a = z * jax.nn.sigmoid(_dot(z.astype(BF16), wglu_ref[...]))
    mix = _dot(a.astype(BF16), wa_ref[...]) + _dot(b_ref[...].astype(BF16), wb_ref[...])
    o_ref[...] = h_ref[...] + _rms_rows(mix, g_ref[3:4, :])


def _outproj_even(h, ys, u, b_out, g, d_skip, w_glu, w_a, w_b):
    t, d = h.shape
    tm = min(TM_FFN, t)
    tok = lambda w: pl.BlockSpec((tm, w), lambda i: (i, 0))
    return pl.pallas_call(
        _outproj_even_kernel,
        grid=(t // tm,),
        in_specs=[tok(d), tok(S5_WIDTH), tok(S5_WIDTH), tok(NSA_Q), _resident(g.shape),
                  _resident(d_skip.shape), _resident(w_glu.shape), _resident(w_a.shape),
                  _resident(w_b.shape)],
        out_specs=tok(d),
        out_shape=jax.ShapeDtypeStruct((t, d), F32),
        compiler_params=_cparams(("parallel",)),
        name="outproj_even",
    )(h, ys, u, b_out, g, d_skip, w_glu, w_a, w_b)


def _proj_even_kernel(h_ref, g_ref, wu_ref, wt_ref, cos_ref, sin_ref,
                      u_ref, q_ref, k_ref, v_ref, gt_ref):
    xn = _rms_rows(h_ref[...], g_ref[2:3, :]).astype(BF16)
    u_ref[...] = _dot(xn, wu_ref[...])
    y = _dot_nt(wt_ref[...], xn)
    tq = y.shape[1]
    cos = cos_ref[...]
    sin = sin_ref[...]
    hd = NSA_HEAD_DIM
    q = _rope_fmaj(y[0:NSA_Q].reshape(NSA_HEADS, hd, tq), cos, sin)
    q_ref[...] = (q * (hd ** -0.5)).reshape(NSA_Q, tq)
    k0 = NSA_Q
    nk = 3 * NSA_KV
    k = _rope_fmaj(y[k0:k0 + nk].reshape(3 * NSA_KV_HEADS, hd, tq), cos, sin)
    k_ref[...] = k.reshape(nk, tq).astype(k_ref.dtype)
    v0 = k0 + nk
    v_ref[...] = y[v0:v0 + nk].astype(v_ref.dtype)
    g0 = v0 + nk
    gt_ref[...] = jax.nn.sigmoid(y[g0:g0 + 3 * NSA_HEADS])


def _proj_even(h, g, wu, wt, cos, sin, bsz, seq):
    d = h.shape[1]
    tq = min(TQ_PROJ, seq)
    nq = seq // tq
    nk = 3 * NSA_KV
    fm = lambda rows: pl.BlockSpec((None, rows, tq), lambda b, i: (b, 0, i))
    return pl.pallas_call(
        _proj_even_kernel,
        grid=(bsz, nq),
        in_specs=[pl.BlockSpec((tq, d), lambda b, i: (b * nq + i, 0)), _resident(g.shape),
                  _resident(wu.shape), _resident(wt.shape),
                  pl.BlockSpec((cos.shape[0], tq), lambda b, i: (0, i)),
                  pl.BlockSpec((sin.shape[0], tq), lambda b, i: (0, i))],
        out_specs=[pl.BlockSpec((tq, S5_WIDTH), lambda b, i: (b * nq + i, 0)),
                   fm(NSA_Q), fm(nk), fm(nk), fm(3 * NSA_HEADS)],
        out_shape=[jax.ShapeDtypeStruct((bsz * seq, S5_WIDTH), F32),
                   jax.ShapeDtypeStruct((bsz, NSA_Q, seq), F32),
                   jax.ShapeDtypeStruct((bsz, nk, seq), BF16),
                   jax.ShapeDtypeStruct((bsz, nk, seq), BF16),
                   jax.ShapeDtypeStruct((bsz, 3 * NSA_HEADS, seq), F32)],
        compiler_params=_cparams(("parallel", "parallel")),
        name="proj_even",
    )(h, g, wu, wt, cos, sin)


def _proj_odd_kv_kernel(h_ref, g_ref, wt_ref, kvn_ref, cos32_ref, sin32_ref, cos16_ref, sin16_ref,
                        ckv_ref, kr_ref, qi_ref, ki_ref, wi_ref):
    xn = _rms_rows(h_ref[...], g_ref[2:3, :]).astype(BF16)
    y = _dot_nt(wt_ref[...], xn)
    tq = y.shape[1]
    c = y[0:MLA_LATENT]
    c = c * lax.rsqrt(jnp.mean(c * c, axis=0, keepdims=True) + EPS) * kvn_ref[...]
    ckv_ref[...] = c.astype(ckv_ref.dtype)
    r0 = MLA_LATENT
    kr = _rope_fmaj(y[r0:r0 + MLA_ROPE].reshape(1, MLA_ROPE, tq), cos16_ref[...], sin16_ref[...])
    kr_ref[...] = kr.reshape(MLA_ROPE, tq).astype(kr_ref.dtype)
    q0 = r0 + MLA_ROPE
    nqi = IDX_HEADS * IDX_DIM
    qi = _rope_fmaj(y[q0:q0 + nqi].reshape(IDX_HEADS, IDX_DIM, tq), cos32_ref[...], sin32_ref[...])
    qi_ref[...] = qi.reshape(nqi, tq).astype(qi_ref.dtype)
    k0 = q0 + nqi
    ki = _rope_fmaj(y[k0:k0 + IDX_DIM].reshape(1, IDX_DIM, tq), cos32_ref[...], sin32_ref[...])
    ki_ref[...] = ki.reshape(IDX_DIM, tq).astype(ki_ref.dtype)
    w0 = k0 + IDX_DIM
    wi_ref[...] = y[w0:w0 + IDX_HEADS]


def _proj_odd_kv(h, g, wt, kvn, cos32, sin32, cos16, sin16, bsz, seq):
    d = h.shape[1]
    tq = min(TQ_PROJ, seq)
    nq = seq // tq
    fm = lambda rows: pl.BlockSpec((None, rows, tq), lambda b, i: (b, 0, i))
    tab = lambda a: pl.BlockSpec((a.shape[0], tq), lambda b, i: (0, i))
    nqi = IDX_HEADS * IDX_DIM
    return pl.pallas_call(
        _proj_odd_kv_kernel,
        grid=(bsz, nq),
        in_specs=[pl.BlockSpec((tq, d), lambda b, i: (b * nq + i, 0)), _resident(g.shape),
                  _resident(wt.shape), _resident(kvn.shape),
                  tab(cos32), tab(sin32), tab(cos16), tab(sin16)],
        out_specs=[fm(MLA_LATENT), fm(MLA_ROPE), fm(nqi), fm(IDX_DIM), fm(IDX_HEADS)],
        out_shape=[jax.ShapeDtypeStruct((bsz, MLA_LATENT, seq), BF16),
                   jax.ShapeDtypeStruct((bsz, MLA_ROPE, seq), BF16),
                   jax.ShapeDtypeStruct((bsz, nqi, seq), BF16),
                   jax.ShapeDtypeStruct((bsz, IDX_DIM, seq), BF16),
                   jax.ShapeDtypeStruct((bsz, IDX_HEADS, seq), F32)],
        compiler_params=_cparams(("parallel", "parallel")),
        name="proj_odd_kv",
    )(h, g, wt, kvn, cos32, sin32, cos16, sin16)


def _proj_odd_q_kernel(h_ref, g_ref, wt_ref, cos16_ref, sin16_ref, q_ref, xn_ref):
    @pl.when(pl.program_id(2) == 0)
    def _():
        xn_ref[...] = _rms_rows(h_ref[...], g_ref[2:3, :]).astype(BF16)

    y = _dot_nt(wt_ref[...], xn_ref[...])
    tq = y.shape[1]
    r = _rope_fmaj(y[MLA_LATENT:].reshape(1, MLA_ROPE, tq), cos16_ref[...], sin16_ref[...])
    q = jnp.concatenate([y[:MLA_LATENT], r.reshape(MLA_ROPE, tq)], axis=0)
    q_ref[...] = (q * (MLA_QK ** -0.5)).astype(q_ref.dtype)


def _proj_odd_q(h, g, wt, cos16, sin16, bsz, seq):
    d = h.shape[1]
    tq = min(TQ_PROJ, seq)
    nq = seq // tq
    return pl.pallas_call(
        _proj_odd_q_kernel,
        grid=(bsz, nq, MLA_HEADS),
        in_specs=[pl.BlockSpec((tq, d), lambda b, i, f: (b * nq + i, 0)), _resident(g.shape),
                  pl.BlockSpec((MLA_QK, d), lambda b, i, f: (f, 0)),
                  pl.BlockSpec((cos16.shape[0], tq), lambda b, i, f: (0, i)),
                  pl.BlockSpec((sin16.shape[0], tq), lambda b, i, f: (0, i))],
        out_specs=pl.BlockSpec((None, None, MLA_QK, tq), lambda b, i, f: (b, f, 0, i)),
        out_shape=jax.ShapeDtypeStruct((bsz, MLA_HEADS, MLA_QK, seq), BF16),
        scratch_shapes=[pltpu.VMEM((tq, d), BF16)],
        compiler_params=_cparams(("parallel", "parallel", "arbitrary")),
        name="proj_odd_q",
    )(h, g, wt, cos16, sin16)


def _s5_state_kernel(u_ref, bc_ref, s_ref):
    s_ref[...] = _dot(u_ref[...], bc_ref[...])


def _s5_scan_kernel(sr_ref, si_ref, lr_ref, li_ref, xr_ref, xi_ref):
    n_chunk = sr_ref.shape[0]
    lr = lr_ref[...]
    li = li_ref[...]

    def body(c, carry):
        xr, xi = carry
        xr_ref[c] = xr
        xi_ref[c] = xi
        return (lr * xr - li * xi + sr_ref[c], lr * xi + li * xr + si_ref[c])

    zero = jnp.zeros(sr_ref.shape[1:], F32)
    lax.fori_loop(0, n_chunk, body, (zero, zero))


def _s5_out_kernel(u_ref, x_ref, m_ref, cc_ref, y_ref):
    x = x_ref[...]
    hi = x.astype(BF16)
    lo = (x - hi.astype(F32)).astype(BF16)
    cc = cc_ref[...]
    y_ref[...] = _dot(u_ref[...], m_ref[...]) + _dot(hi, cc) + _dot(lo, cc)


def _s5_matrices(a_re, a_im, b_re, b_im, c_re, c_im, log_dt, tc):
    hp = lax.Precision.HIGHEST
    dt = jnp.exp(log_dt)[:, None]
    lam = lax.complex(a_re, a_im)
    lam_dt = lam * dt
    lam_bar = jnp.exp(lam_dt)
    b_bar = ((lam_bar - 1.0) / lam)[..., None] * lax.complex(b_re, b_im)
    k = jnp.arange(tc + 1, dtype=F32)[:, None, None]
    pw = jnp.exp(lam_dt[None] * k)
    pr, pi = jnp.real(pw), jnp.imag(pw)
    bbr, bbi = jnp.real(b_bar), jnp.imag(b_bar)
    cpr = c_re[None] * pr[:, :, None, :] - c_im[None] * pi[:, :, None, :]
    cpi = c_re[None] * pi[:, :, None, :] + c_im[None] * pr[:, :, None, :]
    kk = (jnp.einsum('kgpn,gnq->kgpq', cpr[:tc], bbr, precision=hp)
          - jnp.einsum('kgpn,gnq->kgpq', cpi[:tc], bbi, precision=hp))
    lag = jnp.arange(tc)[None, :] - jnp.arange(tc)[:, None]
    kt = jnp.where((lag >= 0)[:, :, None, None, None], kk[jnp.clip(lag, 0, tc - 1)], 0.0)
    n_g, n_p = a_re.shape[0], b_re.shape[2]
    m = kt.transpose(2, 0, 4, 1, 3).reshape(n_g, tc * n_p, tc * n_p)
    rev = pw[tc - 1 - jnp.arange(tc)]
    bc = rev[:, :, :, None] * b_bar[None]
    bc = bc.transpose(1, 0, 3, 2).reshape(n_g, tc * n_p, -1)
    bc = jnp.concatenate([jnp.real(bc), jnp.imag(bc)], axis=-1)
    mr = cpr[1:tc + 1].transpose(1, 3, 0, 2).reshape(n_g, -1, tc * n_p)
    mi = cpi[1:tc + 1].transpose(1, 3, 0, 2).reshape(n_g, -1, tc * n_p)
    cc = jnp.concatenate([mr, -mi], axis=1)
    ltc = pw[tc].reshape(1, -1)
    return m.astype(BF16), bc.astype(BF16), cc.astype(BF16), jnp.real(ltc), jnp.imag(ltc)


def _s5_scan(u, mats, bsz, seq):
    m, bc, cc, lr, li = mats
    tc = S5_CHUNK
    n_c = seq // tc
    n_g, n_p, n_s = S5_GROUPS, S5_GROUP, S5_STATE
    rows = bsz * n_c
    kd = tc * n_p
    ug = (u.reshape(bsz, n_c, tc, n_g, n_p).transpose(3, 0, 1, 2, 4)
          .reshape(n_g, rows, kd).astype(BF16))
    grp = lambda a, b: pl.BlockSpec((None, a, b), lambda gi: (gi, 0, 0))
    s = pl.pallas_call(
        _s5_state_kernel,
        grid=(n_g,),
        in_specs=[grp(rows, kd), grp(kd, 2 * n_s)],
        out_specs=grp(rows, 2 * n_s),
        out_shape=jax.ShapeDtypeStruct((n_g, rows, 2 * n_s), F32),
        compiler_params=_cparams(("parallel",)),
        name="s5_state",
    )(ug, bc)
    s5 = s.reshape(n_g, bsz, n_c, 2, n_s).transpose(3, 2, 1, 0, 4).reshape(2, n_c, bsz, n_g * n_s)
    full = lambda shp: pl.BlockSpec(shp, lambda: (0,) * len(shp))
    xr, xi = pl.pallas_call(
        _s5_scan_kernel,
        in_specs=[full(s5.shape[1:]), full(s5.shape[1:]), full(lr.shape), full(li.shape)],
        out_specs=[full(s5.shape[1:]), full(s5.shape[1:])],
        out_shape=[jax.ShapeDtypeStruct(s5.shape[1:], F32)] * 2,
        compiler_params=pltpu.CompilerParams(vmem_limit_bytes=VMEM_LIMIT),
        name="s5_scan",
    )(s5[0], s5[1], lr, li)
    x = jnp.stack([xr, xi]).reshape(2, n_c, bsz, n_g, n_s).transpose(3, 2, 1, 0, 4)
    x = x.reshape(n_g, rows, 2 * n_s)
    y = pl.pallas_call(
        _s5_out_kernel,
        grid=(n_g,),
        in_specs=[grp(rows, kd), grp(rows, 2 * n_s), grp(kd, kd), grp(2 * n_s, kd)],
        out_specs=grp(rows, kd),
        out_shape=jax.ShapeDtypeStruct((n_g, rows, kd), F32),
        compiler_params=_cparams(("parallel",)),
        name="s5_out",
    )(ug, x, m, cc)
    return (y.reshape(n_g, bsz, n_c, tc, n_p).transpose(1, 2, 3, 0, 4)
            .reshape(bsz * seq, n_g * n_p))


def _compress_kernel(x_ref, pe_ref, w1_ref, w2_ref, o_ref):
    x = x_ref[...].astype(F32)
    nb = x.shape[0]
    a = _dot((x + pe_ref[0:1, :]).astype(BF16), w1_ref[0])
    b = _dot((x + pe_ref[1:2, :]).astype(BF16), w1_ref[1])
    pre = a + pltpu.roll(b, nb - 1, 0)
    o_ref[...] = _dot(jax.nn.gelu(pre).astype(BF16), w2_ref[...]).astype(o_ref.dtype)


def _compress(x, pe, w1, w2):
    bsz, hk, nb, kd = x.shape
    hd = w2.shape[1]
    return pl.pallas_call(
        _compress_kernel,
        grid=(bsz, hk),
        in_specs=[pl.BlockSpec((None, None, nb, kd), lambda b, h: (b, h, 0, 0)),
                  _resident(pe.shape), _resident(w1.shape), _resident(w2.shape)],
        out_specs=pl.BlockSpec((None, None, nb, hd), lambda b, h: (b, h, 0, 0)),
        out_shape=jax.ShapeDtypeStruct((bsz, hk, nb, hd), BF16),
        compiler_params=_cparams(("parallel", "parallel")),
        name="nsa_compress",
    )(x, pe, w1, w2)


def _softmax_step(s, carry, v):
    m, l, acc = carry
    m_new = jnp.maximum(m, jnp.max(s, axis=0, keepdims=True))
    alpha = jnp.exp(m - m_new)
    p = jnp.exp(s - m_new)
    l = alpha * l + jnp.sum(p, axis=0, keepdims=True)
    acc = alpha * acc + _dot(v, p.astype(BF16))
    return m_new, l, acc


def _nsa_kernel(q_ref, g_ref, kc_ref, vct_ref, ks_ref, vst_ref, kw_ref, vwt_ref, ov_ref,
                o_ref, sel_ref, *, tq, tk, top_n):
    gq, hd = NSA_GQA, NSA_HEAD_DIM
    n_sb = sel_ref.shape[0]
    nb = kc_ref.shape[0]
    wd = gq * tq
    i = pl.program_id(2)
    s0 = i * tq
    q = q_ref[...]
    qt = jnp.concatenate([q[g * hd:(g + 1) * hd, :] for g in range(gq)], axis=1).astype(BF16)
    t1 = s0 + lax.broadcasted_iota(I32, (1, tq), 1)
    t4 = s0 + (lax.broadcasted_iota(I32, (1, wd), 1) & (tq - 1))

    sc = _dot(kc_ref[...], qt)
    c_last = lax.broadcasted_iota(I32, (nb, 1), 0) * CMP_STRIDE + (CMP_LEN - 1)
    m_c = c_last <= t4
    sm = jnp.where(m_c, sc, NEG)
    e = jnp.exp(sm - jnp.max(sm, axis=0, keepdims=True))
    p_c = jnp.where(m_c, e / jnp.sum(e, axis=0, keepdims=True), 0.0)
    o_c = _dot(vct_ref[...], p_c.astype(BF16))
    psum = p_c[:, 0:tq]
    for g in range(1, gq):
        psum = psum + p_c[:, g * tq:(g + 1) * tq]
    hi = psum.astype(BF16)
    lo = (psum - hi.astype(F32)).astype(BF16)
    imp = _dot(ov_ref[...], hi) + _dot(ov_ref[...], lo)
    blk = lax.broadcasted_iota(I32, (n_sb, 1), 0)
    cur = t1 >> int(math.log2(SEL_BLOCK))
    forced = (blk == 0) | (blk == cur) | (blk == cur - 1)
    imp = imp + jnp.where(forced, FORCE_BONUS, 0.0)
    imp = jnp.where(blk * SEL_BLOCK <= t1, imp, NEG)

    rowf = lax.broadcasted_iota(I32, (n_sb, tq), 0).astype(F32)
    sel = jnp.zeros((n_sb, tq), F32)
    x = imp
    for _ in range(top_n):
        mx = jnp.max(x, axis=0, keepdims=True)
        first = jnp.min(jnp.where(x == mx, rowf, float(n_sb)), axis=0, keepdims=True)
        hit = rowf == first
        sel = jnp.where(hit, 1.0, sel)
        x = jnp.where(hit, -jnp.inf, x)
    sel_ref[...] = sel

    init = (jnp.full((1, wd), NEG, F32), jnp.zeros((1, wd), F32), jnp.zeros((hd, wd), F32))

    bpt = tk // SEL_BLOCK
    key_row = lax.broadcasted_iota(I32, (tk, 1), 0)

    def sel_tile(j, carry, causal):
        s = _dot(ks_ref[j], qt)
        rows = [jnp.broadcast_to(sel_ref[pl.ds(j * bpt + b, 1), :], (SEL_BLOCK, tq))
                for b in range(bpt)]
        keep = jnp.concatenate(rows, axis=0) > 0.5
        if causal:
            keep = keep & (j * tk + key_row <= t1)
        bias = jnp.where(keep, 0.0, NEG)
        s = s + jnp.concatenate([bias] * gq, axis=1)
        return _softmax_step(s, carry, vst_ref[j])

    jd = s0 // tk
    carry = lax.fori_loop(0, jd, lambda j, c: sel_tile(j, c, False), init)
    _, l_s, a_s = sel_tile(jd, carry, True)
    o_s = a_s / l_s

    wrow = lax.broadcasted_iota(I32, (LANE, 1), 0)

    def win_tile(j, carry):
        s = _dot(kw_ref[j], qt)
        diff = t4 - (j * LANE + wrow)
        s = jnp.where((diff >= 0) & (diff < WINDOW), s, NEG)
        return _softmax_step(s, carry, vwt_ref[j])

    tpq = tq // LANE
    j_lo = jnp.maximum(i * tpq - WINDOW // LANE, 0)
    _, l_w, a_w = lax.fori_loop(j_lo, (i + 1) * tpq, win_tile, init)
    o_w = a_w / l_w

    gt = g_ref[...]
    for g in range(gq):
        cs = slice(g * tq, (g + 1) * tq)
        o_ref[g * hd:(g + 1) * hd, :] = (gt[3 * g:3 * g + 1, :] * o_c[:, cs]
                                         + gt[3 * g + 1:3 * g + 2, :] * o_s[:, cs]
                                         + gt[3 * g + 2:3 * g + 3, :] * o_w[:, cs])


def _nsa(q, gates, kcmp, vcmp_t, ks, vs_t, kw, vw_t, ov_t, seq):
    bsz = q.shape[0]
    tq = min(TQ_NSA, seq)
    tk = min(TK_NSA, seq)
    n_sb = seq // SEL_BLOCK
    gq, hd = NSA_GQA, NSA_HEAD_DIM
    per_head = lambda a: pl.BlockSpec((None, None) + a.shape[2:],
                                      lambda b, h, i: (b, h) + (0,) * (a.ndim - 2),
                                      pipeline_mode=pl.Buffered(1))
    return pl.pallas_call(
        functools.partial(_nsa_kernel, tq=tq, tk=tk, top_n=min(SEL_TOP, n_sb)),
        grid=(bsz, NSA_KV_HEADS, seq // tq),
        in_specs=[pl.BlockSpec((None, gq * hd, tq), lambda b, h, i: (b, h, i)),
                  pl.BlockSpec((None, None, 3 * gq, tq), lambda b, h, i: (b, h, 0, i)),
                  per_head(kcmp), per_head(vcmp_t), per_head(ks), per_head(vs_t),
                  per_head(kw), per_head(vw_t), _resident(ov_t.shape)],
        out_specs=pl.BlockSpec((None, gq * hd, tq), lambda b, h, i: (b, h, i)),
        out_shape=jax.ShapeDtypeStruct((bsz, NSA_Q, seq), F32),
        scratch_shapes=[pltpu.VMEM((n_sb, tq), F32)],
        compiler_params=_cparams(("parallel", "parallel", "arbitrary")),
        name="nsa",
    )(q, gates, kcmp, vcmp_t, ks, vs_t, kw, vw_t, ov_t)


def _dsa_kernel(q_ref, qi_ref, wi_ref, kcat_ref, ckt_ref, kidx_ref, wuv_ref, o_ref,
                ibuf, acc_ref, m_ref, l_ref, *, tq, top):
    tk = tq
    n_h = MLA_HEADS
    i = pl.program_id(1)
    row = lax.broadcasted_iota(I32, (tk, 1), 0)
    col = lax.broadcasted_iota(I32, (1, tq), 1)
    causal = row <= col

    qi = qi_ref[...]
    wi = wi_ref[...] * (IDX_HEADS ** -0.5)

    def idx_scores(j):
        kj = kidx_ref[j]
        sc = jnp.zeros((tk, tq), F32)
        for h in range(IDX_HEADS):
            d = _dot(kj, qi[h * IDX_DIM:(h + 1) * IDX_DIM, :])
            sc = sc + wi[h:h + 1, :] * jnp.maximum(d * (IDX_DIM ** -0.5), 0.0)
        return sc

    def to_key(sc):
        b = pltpu.bitcast(sc, I32)
        return b ^ ((b >> 31) & 0x7FFFFFFF)

    def idx_tile(j, _):
        ibuf[pl.ds(pl.multiple_of(j * tk, tk), tk), :] = to_key(idx_scores(j))
        return 0

    lax.fori_loop(0, i, idx_tile, 0)
    ibuf[pl.ds(pl.multiple_of(i * tk, tk), tk), :] = to_key(jnp.where(causal, idx_scores(i), NEG))

    def count_ge(cand):
        def body(j, c):
            ge = ibuf[pl.ds(pl.multiple_of(j * tk, tk), tk), :] >= cand
            return c + jnp.sum(jnp.where(ge, 1, 0).reshape(tk // 8, 8, tq), axis=0)
        c8 = lax.fori_loop(0, i + 1, body, jnp.zeros((8, tq), I32))
        return jnp.sum(c8, axis=0, keepdims=True)

    def bit_step(b, thr):
        cand = thr + (jnp.int32(1) << (31 - b))
        return jnp.where(count_ge(cand) >= top, cand, thr)

    thr = lax.fori_loop(0, 32, bit_step, jnp.full((1, tq), -2**31, I32))

    m_ref[...] = jnp.full(m_ref.shape, NEG, F32)
    l_ref[...] = jnp.zeros(l_ref.shape, F32)
    acc_ref[...] = jnp.zeros(acc_ref.shape, F32)

    def att_tile(j, diag):
        keep = ibuf[pl.ds(pl.multiple_of(j * tk, tk), tk), :] >= thr
        if diag:
            keep = keep & causal
        bias = jnp.where(keep, 0.0, NEG)
        kj = kcat_ref[j]
        cj = ckt_ref[j]
        for h in range(n_h):
            s = _dot(kj, q_ref[h]) + bias
            m_old = m_ref[h:h + 1, :]
            m_new = jnp.maximum(m_old, jnp.max(s, axis=0, keepdims=True))
            alpha = jnp.exp(m_old - m_new)
            p = jnp.exp(s - m_new)
            l_ref[h:h + 1, :] = alpha * l_ref[h:h + 1, :] + jnp.sum(p, axis=0, keepdims=True)
            acc_ref[h] = alpha * acc_ref[h] + _dot(cj, p.astype(BF16))
            m_ref[h:h + 1, :] = m_new

    def att_body(j, _):
        att_tile(j, False)
        return 0

    lax.fori_loop(0, i, att_body, 0)
    att_tile(i, True)

    for h in range(n_h):
        o = acc_ref[h] / l_ref[h:h + 1, :]
        o_ref[h * MLA_V_DIM:(h + 1) * MLA_V_DIM, :] = _dot(wuv_ref[h], o.astype(BF16))


def _dsa(q, qi, wi, kcat, ckt, kidx, wuv_t, seq):
    bsz = q.shape[0]
    tq = min(TQ_DSA, seq)
    top = min(DSA_TOP, seq // 4)
    per_b = lambda a: pl.BlockSpec((None,) + a.shape[1:], lambda b, i: (b,) + (0,) * (a.ndim - 1),
                                   pipeline_mode=pl.Buffered(1))
    return pl.pallas_call(
        functools.partial(_dsa_kernel, tq=tq, top=top),
        grid=(bsz, seq // tq),
        in_specs=[pl.BlockSpec((None, MLA_HEADS, MLA_QK, tq), lambda b, i: (b, 0, 0, i)),
                  pl.BlockSpec((None, IDX_HEADS * IDX_DIM, tq), lambda b, i: (b, 0, i)),
                  pl.BlockSpec((None, IDX_HEADS, tq), lambda b, i: (b, 0, i)),
                  per_b(kcat), per_b(ckt), per_b(kidx), _resident(wuv_t.shape)],
        out_specs=pl.BlockSpec((None, MLA_HEADS * MLA_V_DIM, tq), lambda b, i: (b, 0, i)),
        out_shape=jax.ShapeDtypeStruct((bsz, MLA_HEADS * MLA_V_DIM, seq), F32),
        scratch_shapes=[pltpu.VMEM((seq, tq), I32),
                        pltpu.VMEM((MLA_HEADS, MLA_LATENT, tq), F32),
                        pltpu.VMEM((MLA_HEADS, tq), F32),
                        pltpu.VMEM((MLA_HEADS, tq), F32)],
        compiler_params=_cparams(("parallel", "arbitrary")),
        name="dsa",
    )(q, qi, wi, kcat, ckt, kidx, wuv_t)


def _rope_tables(seq, dim):
    half = dim // 2
    inv = ROPE_THETA ** (-jnp.arange(half, dtype=F32) / half)
    ang = inv[:, None] * jnp.arange(seq, dtype=F32)[None, :]
    return jnp.cos(ang), jnp.sin(ang)


def _even_mixer(h, g, bsz, seq, w_in, w_out, s5, pe_k, pe_v, wk1, wk2, wv1, wv2, tabs):
    cos32, sin32 = tabs[64]
    hd, hk = NSA_HEAD_DIM, NSA_KV_HEADS
    o = 0
    cols = {}
    for name, size in (("u", S5_WIDTH), ("q", NSA_Q), ("kc", NSA_KV), ("vc", NSA_KV), ("ks", NSA_KV),
                       ("vs", NSA_KV), ("kw", NSA_KV), ("vw", NSA_KV), ("gt", 3 * NSA_HEADS)):
        cols[name] = w_in[:, o:o + size]
        o += size
    wu = cols["u"].astype(BF16)
    wt = jnp.concatenate([cols[n] for n in ("q", "kc", "ks", "kw", "vc", "vs", "vw", "gt")], axis=1).T
    wt = jnp.pad(wt, ((0, (-wt.shape[0]) % 16), (0, 0))).astype(BF16)
    u, q_t, k_t, v_t, gt_t = _proj_even(h, g, wu, wt, cos32, sin32, bsz, seq)

    a_re, a_im, b_re, b_im, c_re, c_im, log_dt, d_skip, w_glu = s5
    mats = _s5_matrices(a_re, a_im, b_re, b_im, c_re, c_im, log_dt, S5_CHUNK)
    ys = _s5_scan(u, mats, bsz, seq)

    k5 = k_t.reshape(bsz, 3, hk, hd, seq)
    v5 = v_t.reshape(bsz, 3, hk, hd, seq)
    nb = seq // CMP_STRIDE
    half_blk = lambda a: (a.reshape(bsz, hk, hd, nb, CMP_STRIDE).transpose(0, 1, 3, 4, 2)
                          .reshape(bsz, hk, nb, CMP_STRIDE * hd))
    pe2 = lambda pe: pe.reshape(2, CMP_STRIDE * hd)
    w1s = lambda w: w.reshape(2, CMP_STRIDE * hd, hd).astype(BF16)
    kcmp = _compress(half_blk(k5[:, 0]), pe2(pe_k), w1s(wk1), wk2.astype(BF16))
    vcmp = _compress(half_blk(v5[:, 0]), pe2(pe_v), w1s(wv1), wv2.astype(BF16))
    vcmp_t = vcmp.transpose(0, 1, 3, 2)
    tk = min(TK_NSA, seq)
    ks = k5[:, 1].reshape(bsz, hk, hd, seq // tk, tk).transpose(0, 1, 3, 4, 2)
    vs_t = v5[:, 1].reshape(bsz, hk, hd, seq // tk, tk).transpose(0, 1, 3, 2, 4)
    kw = k5[:, 2].reshape(bsz, hk, hd, seq // LANE, LANE).transpose(0, 1, 3, 4, 2)
    vw_t = v5[:, 2].reshape(bsz, hk, hd, seq // LANE, LANE).transpose(0, 1, 3, 2, 4)
    n_sb = seq // SEL_BLOCK
    c_start = jnp.arange(nb) * CMP_STRIDE
    b_start = jnp.arange(n_sb) * SEL_BLOCK
    ov_t = ((c_start[None, :] < b_start[:, None] + SEL_BLOCK)
            & (c_start[None, :] + CMP_LEN > b_start[:, None])
            & (jnp.arange(nb)[None, :] < nb - 1)).astype(BF16)
    gates = gt_t.reshape(bsz, hk, 3 * NSA_GQA, seq)
    b_t = _nsa(q_t, gates, kcmp, vcmp_t, ks, vs_t, kw, vw_t, ov_t, seq)
    b_out = b_t.transpose(0, 2, 1).reshape(bsz * seq, NSA_Q)
    return _outproj_even(h, ys, u, b_out, g, d_skip.reshape(1, -1), w_glu.astype(BF16),
                         w_out[:S5_WIDTH].astype(BF16), w_out[S5_WIDTH:].astype(BF16))


def _odd_mixer(h, g, bsz, seq, w_in, kv_norm, w_uv, w_out, tabs):
    cos32, sin32 = tabs[64]
    cos16, sin16 = tabs[32]
    d = h.shape[1]
    sizes = (MLA_HEADS * MLA_LATENT, MLA_HEADS * MLA_ROPE, MLA_LATENT, MLA_ROPE,
             IDX_HEADS * IDX_DIM, IDX_DIM, IDX_HEADS)
    parts = []
    o = 0
    for s in sizes:
        parts.append(w_in[:, o:o + s])
        o += s
    w_ql, w_qr, w_c, w_kr, w_qi, w_ki, w_wi = parts
    wq = jnp.concatenate([w_ql.reshape(d, MLA_HEADS, MLA_LATENT), w_qr.reshape(d, MLA_HEADS, MLA_ROPE)],
                         axis=2).reshape(d, MLA_HEADS * MLA_QK).T.astype(BF16)
    wkv = jnp.concatenate([w_c, w_kr, w_qi, w_ki, w_wi], axis=1).T
    wkv = jnp.pad(wkv, ((0, (-wkv.shape[0]) % 16), (0, 0))).astype(BF16)
    ckv_t, kr_t, qi_t, ki_t, wi_t = _proj_odd_kv(h, g, wkv, kv_norm.reshape(-1, 1), cos32, sin32,
                                                 cos16, sin16, bsz, seq)
    q = _proj_odd_q(h, g, wq, cos16, sin16, bsz, seq)
    tk = min(TQ_DSA, seq)
    nk = seq // tk
    kcat = jnp.concatenate([ckv_t, kr_t], axis=1).transpose(0, 2, 1).reshape(bsz, nk, tk, MLA_QK)
    ckt = ckv_t.reshape(bsz, MLA_LATENT, nk, tk).transpose(0, 2, 1, 3)
    kidx = ki_t.transpose(0, 2, 1).reshape(bsz, nk, tk, IDX_DIM)
    wuv_t = w_uv.transpose(0, 2, 1).astype(BF16)
    o_t = _dsa(q, qi_t, wi_t, kcat, ckt, kidx, wuv_t, seq)
    mix = o_t.transpose(0, 2, 1).reshape(bsz * seq, MLA_HEADS * MLA_V_DIM)
    return _outproj_odd(h, mix, g, w_out.astype(BF16))


def kernel(x, p, norm_g, ffn1_w_in, ffn1_w_out, ffn2_w_in, ffn2_w_out, ple_w_gate, ple_w_proj,
           ev_w_in, ev_w_out, s5_a_re, s5_a_im, s5_b_re, s5_b_im, s5_c_re, s5_c_im, s5_log_dt, s5_d,
           s5_w_glu, nsa_pe_k, nsa_pe_v, nsa_wk1, nsa_wk2, nsa_wv1, nsa_wv2,
           od_w_in, od_kv_norm, od_w_uv, od_w_out):
    bsz, seq, d = x.shape
    depth = norm_g.shape[0]
    tabs = {64: _rope_tables(seq, 64), 32: _rope_tables(seq, 32)}
    h = x.reshape(bsz * seq, d)
    for i in range(depth):
        g = norm_g[i]
        j = i // 2
        h = _ffn(h, g, ffn1_w_in[i].astype(BF16), ffn1_w_out[i].astype(BF16), 0)
        if i % 2 == 0:
            s5 = (s5_a_re[j], s5_a_im[j], s5_b_re[j], s5_b_im[j], s5_c_re[j], s5_c_im[j],
                  s5_log_dt[j], s5_d[j], s5_w_glu[j])
            h = _even_mixer(h, g, bsz, seq, ev_w_in[j], ev_w_out[j], s5, nsa_pe_k[j], nsa_pe_v[j],
                            nsa_wk1[j], nsa_wk2[j], nsa_wv1[j], nsa_wv2[j], tabs)
        else:
            h = _odd_mixer(h, g, bsz, seq, od_w_in[j], od_kv_norm[j], od_w_uv[j], od_w_out[j], tabs)
        h = _ffn(h, g, ffn2_w_in[i].astype(BF16), ffn2_w_out[i].astype(BF16), 4)
        h = _ple(h, p[i].reshape(bsz * seq, -1), g, ple_w_gate[i].astype(BF16),
                 ple_w_proj[i].astype(BF16))
    return h.reshape(bsz, seq, d)
```

```python
import functools
import math

import jax
import jax.numpy as jnp
from jax import lax
from jax.experimental import pallas as pl
from jax.experimental.pallas import tpu as pltpu

F32 = jnp.float32
BF16 = jnp.bfloat16
I32 = jnp.int32

ROPE_THETA = 10000.0
EPS = 1e-6
NEG = -1e30
D_FF = 2816
S5_WIDTH = 512
S5_GROUP = 16
S5_GROUPS = S5_WIDTH // S5_GROUP
S5_STATE = 64
NSA_HEADS = 8
NSA_KV_HEADS = 2
NSA_GQA = NSA_HEADS // NSA_KV_HEADS
NSA_HEAD_DIM = 64
CMP_LEN = 32
CMP_STRIDE = 16
SEL_BLOCK = 64
SEL_TOP = 16
WINDOW = 512
FORCE_BONUS = 1000.0
NSA_Q = NSA_HEADS * NSA_HEAD_DIM
NSA_KV = NSA_KV_HEADS * NSA_HEAD_DIM
MLA_HEADS = 16
MLA_LATENT = 256
MLA_ROPE = 32
MLA_QK = MLA_LATENT + MLA_ROPE
MLA_V_DIM = 64
IDX_HEADS = 8
IDX_DIM = 64
DSA_TOP = 256

V7X_VMEM_BYTES = 64 * 2**20
VMEM_LIMIT = V7X_VMEM_BYTES - 8 * 2**20
LANE = 128
TM_FFN = 512
TQ_PROJ = 256
TQ_NSA = 128
TK_NSA = 512
TQ_DSA = 256
S5_CHUNK = 64


def _cparams(sem):
    return pltpu.CompilerParams(dimension_semantics=sem, vmem_limit_bytes=VMEM_LIMIT)


def _resident(shape):
    nd = len(shape)
    return pl.BlockSpec(shape, lambda *_: (0,) * nd, pipeline_mode=pl.Buffered(1))


def _dot(a, b):
    return jnp.dot(a, b, preferred_element_type=F32)


def _dot_nt(a, b):
    return lax.dot_general(a, b, (((1,), (1,)), ((), ())), preferred_element_type=F32)


def _rms_rows(x, g):
    return x * lax.rsqrt(jnp.mean(x * x, axis=-1, keepdims=True) + EPS) * g


def _rope_fmaj(y, cos, sin):
    half = y.shape[1] // 2
    t1 = y[:, :half, :]
    t2 = y[:, half:, :]
    return jnp.concatenate([t1 * cos - t2 * sin, t2 * cos + t1 * sin], axis=1)


def _ffn_kernel(h_ref, g_ref, win_ref, wout_ref, o_ref, *, g0, n_chunk):
    x = h_ref[...]
    xn = _rms_rows(x, g_ref[g0:g0 + 1, :]).astype(BF16)
    ck = D_FF // n_chunk
    acc = None
    for c in range(n_chunk):
        a = _dot(xn, win_ref[:, c * ck:(c + 1) * ck])
        u = _dot(xn, win_ref[:, D_FF + c * ck:D_FF + (c + 1) * ck])
        act = (jax.nn.silu(a) * u).astype(BF16)
        y = _dot(act, wout_ref[c * ck:(c + 1) * ck, :])
        acc = y if acc is None else acc + y
    o_ref[...] = x + 0.5 * _rms_rows(acc, g_ref[g0 + 1:g0 + 2, :])


def _ffn(h, g, w_in, w_out, g0):
    t, d = h.shape
    tm = min(TM_FFN, t)
    return pl.pallas_call(
        functools.partial(_ffn_kernel, g0=g0, n_chunk=2),
        grid=(t // tm,),
        in_specs=[pl.BlockSpec((tm, d), lambda i: (i, 0)), _resident(g.shape),
                  _resident(w_in.shape), _resident(w_out.shape)],
        out_specs=pl.BlockSpec((tm, d), lambda i: (i, 0)),
        out_shape=jax.ShapeDtypeStruct((t, d), F32),
        compiler_params=_cparams(("parallel",)),
        name="ffn",
    )(h, g, w_in, w_out)


def _ple_kernel(h_ref, p_ref, g_ref, wg_ref, wp_ref, o_ref):
    x = h_ref[...]
    gate = jax.nn.sigmoid(_dot(_rms_rows(x, g_ref[6:7, :]).astype(BF16), wg_ref[...]))
    e = _dot(p_ref[...].astype(BF16), wp_ref[...]) * gate
    o_ref[...] = x + _rms_rows(e, g_ref[7:8, :])


def _ple(h, p, g, w_gate, w_proj):
    t, d = h.shape
    tm = min(TM_FFN, t)
    return pl.pallas_call(
        _ple_kernel,
        grid=(t // tm,),
        in_specs=[pl.BlockSpec((tm, d), lambda i: (i, 0)),
                  pl.BlockSpec((tm, p.shape[1]), lambda i: (i, 0)),
                  _resident(g.shape), _resident(w_gate.shape), _resident(w_proj.shape)],
        out_specs=pl.BlockSpec((tm, d), lambda i: (i, 0)),
        out_shape=jax.ShapeDtypeStruct((t, d), F32),
        compiler_params=_cparams(("parallel",)),
        name="ple",
    )(h, p, g, w_gate, w_proj)


def _outproj_odd_kernel(h_ref, mix_ref, g_ref, w_ref, o_ref):
    y = _dot(mix_ref[...].astype(BF16), w_ref[...])
    o_ref[...] = h_ref[...] + _rms_rows(y, g_ref[3:4, :])


def _outproj_odd(h, mix, g, w_out):
    t, d = h.shape
    tm = min(TM_FFN, t)
    return pl.pallas_call(
        _outproj_odd_kernel,
        grid=(t // tm,),
        in_specs=[pl.BlockSpec((tm, d), lambda i: (i, 0)),
                  pl.BlockSpec((tm, mix.shape[1]), lambda i: (i, 0)),
                  _resident(g.shape), _resident(w_out.shape)],
        out_specs=pl.BlockSpec((tm, d), lambda i: (i, 0)),
        out_shape=jax.ShapeDtypeStruct((t, d), F32),
        compiler_params=_cparams(("parallel",)),
        name="outproj_odd",
    )(h, mix, g, w_out)


def _outproj_even_kernel(h_ref, ys_ref, u_ref, b_ref, g_ref, d_ref, wglu_ref, wa_ref, wb_ref, o_ref):
    y = ys_ref[...] + d_ref[...] * u_ref[...]
    z = jax.nn.gelu(y)
    a = z * jax.nn.sigmoid(_dot(z.astype(BF16), wglu_ref[...]))
    mix = _dot(a.astype(BF16), wa_ref[...]) + _dot(b_ref[...].astype(BF16), wb_ref[...])
    o_ref[...] = h_ref[...] + _rms_rows(mix, g_ref[3:4, :])


def _outproj_even(h, ys, u, b_out, g, d_skip, w_glu, w_a, w_b):
    t, d = h.shape
    tm = min(TM_FFN, t)
    tok = lambda w: pl.BlockSpec((tm, w), lambda i: (i, 0))
    return pl.pallas_call(
        _outproj_even_kernel,
        grid=(t // tm,),
        in_specs=[tok(d), tok(S5_WIDTH), tok(S5_WIDTH), tok(NSA_Q), _resident(g.shape),
                  _resident(d_skip.shape), _resident(w_glu.shape), _resident(w_a.shape),
                  _resident(w_b.shape)],
        out_specs=tok(d),
        out_shape=jax.ShapeDtypeStruct((t, d), F32),
        compiler_params=_cparams(("parallel",)),
        name="outproj_even",
    )(h, ys, u, b_out, g, d_skip, w_glu, w_a, w_b)


def _proj_even_kernel(h_ref, g_ref, wu_ref, wt_ref, cos_ref, sin_ref,
                      u_ref, q_ref, k_ref, v_ref, gt_ref):
    xn = _rms_rows(h_ref[...], g_ref[2:3, :]).astype(BF16)
    u_ref[...] = _dot(xn, wu_ref[...])
    y = _dot_nt(wt_ref[...], xn)
    tq = y.shape[1]
    cos = cos_ref[...]
    sin = sin_ref[...]
    hd = NSA_HEAD_DIM
    q = _rope_fmaj(y[0:NSA_Q].reshape(NSA_HEADS, hd, tq), cos, sin)
    q_ref[...] = (q * (hd ** -0.5)).reshape(NSA_Q, tq)
    k0 = NSA_Q
    nk = 3 * NSA_KV
    k = _rope_fmaj(y[k0:k0 + nk].reshape(3 * NSA_KV_HEADS, hd, tq), cos, sin)
    k_ref[...] = k.reshape(nk, tq).astype(k_ref.dtype)
    v0 = k0 + nk
    v_ref[...] = y[v0:v0 + nk].astype(v_ref.dtype)
    g0 = v0 + nk
    gt_ref[...] = jax.nn.sigmoid(y[g0:g0 + 3 * NSA_HEADS])


def _proj_even(h, g, wu, wt, cos, sin, bsz, seq):
    d = h.shape[1]
    tq = min(TQ_PROJ, seq)
    nq = seq // tq
    nk = 3 * NSA_KV
    fm = lambda rows: pl.BlockSpec((None, rows, tq), lambda b, i: (b, 0, i))
    return pl.pallas_call(
        _proj_even_kernel,
        grid=(bsz, nq),
        in_specs=[pl.BlockSpec((tq, d), lambda b, i: (b * nq + i, 0)), _resident(g.shape),
                  _resident(wu.shape), _resident(wt.shape),
                  pl.BlockSpec((cos.shape[0], tq), lambda b, i: (0, i)),
                  pl.BlockSpec((sin.shape[0], tq), lambda b, i: (0, i))],
        out_specs=[pl.BlockSpec((tq, S5_WIDTH), lambda b, i: (b * nq + i, 0)),
                   fm(NSA_Q), fm(nk), fm(nk), fm(3 * NSA_HEADS)],
        out_shape=[jax.ShapeDtypeStruct((bsz * seq, S5_WIDTH), F32),
                   jax.ShapeDtypeStruct((bsz, NSA_Q, seq), F32),
                   jax.ShapeDtypeStruct((bsz, nk, seq), BF16),
                   jax.ShapeDtypeStruct((bsz, nk, seq), BF16),
                   jax.ShapeDtypeStruct((bsz, 3 * NSA_HEADS, seq), F32)],
        compiler_params=_cparams(("parallel", "parallel")),
        name="proj_even",
    )(h, g, wu, wt, cos, sin)


def _proj_odd_kv_kernel(h_ref, g_ref, wt_ref, kvn_ref, cos32_ref, sin32_ref, cos16_ref, sin16_ref,
                        ckv_ref, kr_ref, qi_ref, ki_ref, wi_ref):
    xn = _rms_rows(h_ref[...], g_ref[2:3, :]).astype(BF16)
    y = _dot_nt(wt_ref[...], xn)
    tq = y.shape[1]
    c = y[0:MLA_LATENT]
    c = c * lax.rsqrt(jnp.mean(c * c, axis=0, keepdims=True) + EPS) * kvn_ref[...]
    ckv_ref[...] = c.astype(ckv_ref.dtype)
    r0 = MLA_LATENT
    kr = _rope_fmaj(y[r0:r0 + MLA_ROPE].reshape(1, MLA_ROPE, tq), cos16_ref[...], sin16_ref[...])
    kr_ref[...] = kr.reshape(MLA_ROPE, tq).astype(kr_ref.dtype)
    q0 = r0 + MLA_ROPE
    nqi = IDX_HEADS * IDX_DIM
    qi = _rope_fmaj(y[q0:q0 + nqi].reshape(IDX_HEADS, IDX_DIM, tq), cos32_ref[...], sin32_ref[...])
    qi_ref[...] = qi.reshape(nqi, tq).astype(qi_ref.dtype)
    k0 = q0 + nqi
    ki = _rope_fmaj(y[k0:k0 + IDX_DIM].reshape(1, IDX_DIM, tq), cos32_ref[...], sin32_ref[...])
    ki_ref[...] = ki.reshape(IDX_DIM, tq).astype(ki_ref.dtype)
    w0 = k0 + IDX_DIM
    wi_ref[...] = y[w0:w0 + IDX_HEADS]


def _proj_odd_kv(h, g, wt, kvn, cos32, sin32, cos16, sin16, bsz, seq):
    d = h.shape[1]
    tq = min(TQ_PROJ, seq)
    nq = seq // tq
    fm = lambda rows: pl.BlockSpec((None, rows, tq), lambda b, i: (b, 0, i))
    tab = lambda a: pl.BlockSpec((a.shape[0], tq), lambda b, i: (0, i))
    nqi = IDX_HEADS * IDX_DIM
    return pl.pallas_call(
        _proj_odd_kv_kernel,
        grid=(bsz, nq),
        in_specs=[pl.BlockSpec((tq, d), lambda b, i: (b * nq + i, 0)), _resident(g.shape),
                  _resident(wt.shape), _resident(kvn.shape),
                  tab(cos32), tab(sin32), tab(cos16), tab(sin16)],
        out_specs=[fm(MLA_LATENT), fm(MLA_ROPE), fm(nqi), fm(IDX_DIM), fm(IDX_HEADS)],
        out_shape=[jax.ShapeDtypeStruct((bsz, MLA_LATENT, seq), BF16),
                   jax.ShapeDtypeStruct((bsz, MLA_ROPE, seq), BF16),
                   jax.ShapeDtypeStruct((bsz, nqi, seq), BF16),
                   jax.ShapeDtypeStruct((bsz, IDX_DIM, seq), BF16),
                   jax.ShapeDtypeStruct((bsz, IDX_HEADS, seq), F32)],
        compiler_params=_cparams(("parallel", "parallel")),
        name="proj_odd_kv",
    )(h, g, wt, kvn, cos32, sin32, cos16, sin16)


def _proj_odd_q_kernel(h_ref, g_ref, wt_ref, cos16_ref, sin16_ref, q_ref):
    xn = _rms_rows(h_ref[...], g_ref[2:3, :]).astype(BF16)
    tq = xn.shape[0]
    cos = cos16_ref[...]
    sin = sin16_ref[...]
    for hh in range(MLA_HEADS):
        y = _dot_nt(wt_ref[hh * MLA_QK:(hh + 1) * MLA_QK, :], xn)
        r = _rope_fmaj(y[MLA_LATENT:].reshape(1, MLA_ROPE, tq), cos, sin)
        q = jnp.concatenate([y[:MLA_LATENT], r.reshape(MLA_ROPE, tq)], axis=0)
        q_ref[:, hh * tq:(hh + 1) * tq] = (q * (MLA_QK ** -0.5)).astype(q_ref.dtype)


def _proj_odd_q(h, g, wt, cos16, sin16, bsz, seq):
    d = h.shape[1]
    tq = min(TQ_DSA, seq)
    nq = seq // tq
    return pl.pallas_call(
        _proj_odd_q_kernel,
        grid=(bsz, nq),
        in_specs=[pl.BlockSpec((tq, d), lambda b, i: (b * nq + i, 0)), _resident(g.shape),
                  _resident(wt.shape),
                  pl.BlockSpec((cos16.shape[0], tq), lambda b, i: (0, i)),
                  pl.BlockSpec((sin16.shape[0], tq), lambda b, i: (0, i))],
        out_specs=pl.BlockSpec((None, None, MLA_QK, MLA_HEADS * tq), lambda b, i: (b, i, 0, 0)),
        out_shape=jax.ShapeDtypeStruct((bsz, nq, MLA_QK, MLA_HEADS * tq), BF16),
        compiler_params=_cparams(("parallel", "parallel")),
        name="proj_odd_q",
    )(h, g, wt, cos16, sin16)


def _s5_state_kernel(u_ref, bc_ref, s_ref):
    s_ref[...] = _dot(u_ref[...], bc_ref[...])


def _s5_scan_kernel(sr_ref, si_ref, lr_ref, li_ref, xr_ref, xi_ref):
    n_chunk = sr_ref.shape[0]
    lr = lr_ref[...]
    li = li_ref[...]

    def body(c, carry):
        xr, xi = carry
        xr_ref[c] = xr
        xi_ref[c] = xi
        return (lr * xr - li * xi + sr_ref[c], lr * xi + li * xr + si_ref[c])

    zero = jnp.zeros(sr_ref.shape[1:], F32)
    lax.fori_loop(0, n_chunk, body, (zero, zero))


def _s5_out_kernel(u_ref, x_ref, m_ref, cc_ref, y_ref):
    x = x_ref[...]
    hi = x.astype(BF16)
    lo = (x - hi.astype(F32)).astype(BF16)
    cc = cc_ref[...]
    y_ref[...] = _dot(u_ref[...], m_ref[...]) + _dot(hi, cc) + _dot(lo, cc)


def _s5_matrices(a_re, a_im, b_re, b_im, c_re, c_im, log_dt, tc):
    hp = lax.Precision.HIGHEST
    dt = jnp.exp(log_dt)[:, None]
    lam = lax.complex(a_re, a_im)
    lam_dt = lam * dt
    lam_bar = jnp.exp(lam_dt)
    b_bar = ((lam_bar - 1.0) / lam)[..., None] * lax.complex(b_re, b_im)
    k = jnp.arange(tc + 1, dtype=F32)[:, None, None]
    pw = jnp.exp(lam_dt[None] * k)
    pr, pi = jnp.real(pw), jnp.imag(pw)
    bbr, bbi = jnp.real(b_bar), jnp.imag(b_bar)
    cpr = c_re[None] * pr[:, :, None, :] - c_im[None] * pi[:, :, None, :]
    cpi = c_re[None] * pi[:, :, None, :] + c_im[None] * pr[:, :, None, :]
    kk = (jnp.einsum('kgpn,gnq->kgpq', cpr[:tc], bbr, precision=hp)
          - jnp.einsum('kgpn,gnq->kgpq', cpi[:tc], bbi, precision=hp))
    lag = jnp.arange(tc)[None, :] - jnp.arange(tc)[:, None]
    kt = jnp.where((lag >= 0)[:, :, None, None, None], kk[jnp.clip(lag, 0, tc - 1)], 0.0)
    n_g, n_p = a_re.shape[0], b_re.shape[2]
    m = kt.transpose(2, 0, 4, 1, 3).reshape(n_g, tc * n_p, tc * n_p)
    rev = pw[tc - 1 - jnp.arange(tc)]
    bc = rev[:, :, :, None] * b_bar[None]
    bc = bc.transpose(1, 0, 3, 2).reshape(n_g, tc * n_p, -1)
    bc = jnp.concatenate([jnp.real(bc), jnp.imag(bc)], axis=-1)
    mr = cpr[1:tc + 1].transpose(1, 3, 0, 2).reshape(n_g, -1, tc * n_p)
    mi = cpi[1:tc + 1].transpose(1, 3, 0, 2).reshape(n_g, -1, tc * n_p)
    cc = jnp.concatenate([mr, -mi], axis=1)
    ltc = pw[tc].reshape(1, -1)
    return m.astype(BF16), bc.astype(BF16), cc.astype(BF16), jnp.real(ltc), jnp.imag(ltc)


def _s5_scan(u, mats, bsz, seq):
    m, bc, cc, lr, li = mats
    tc = S5_CHUNK
    n_c = seq // tc
    n_g, n_p, n_s = S5_GROUPS, S5_GROUP, S5_STATE
    rows = bsz * n_c
    kd = tc * n_p
    ug = (u.reshape(bsz, n_c, tc, n_g, n_p).transpose(3, 0, 1, 2, 4)
          .reshape(n_g, rows, kd).astype(BF16))
    grp = lambda a, b: pl.BlockSpec((None, a, b), lambda gi: (gi, 0, 0))
    s = pl.pallas_call(
        _s5_state_kernel,
        grid=(n_g,),
        in_specs=[grp(rows, kd), grp(kd, 2 * n_s)],
        out_specs=grp(rows, 2 * n_s),
        out_shape=jax.ShapeDtypeStruct((n_g, rows, 2 * n_s), F32),
        compiler_params=_cparams(("parallel",)),
        name="s5_state",
    )(ug, bc)
    s5 = s.reshape(n_g, bsz, n_c, 2, n_s).transpose(3, 2, 1, 0, 4).reshape(2, n_c, bsz, n_g * n_s)
    full = lambda shp: pl.BlockSpec(shp, lambda: (0,) * len(shp))
    xr, xi = pl.pallas_call(
        _s5_scan_kernel,
        in_specs=[full(s5.shape[1:]), full(s5.shape[1:]), full(lr.shape), full(li.shape)],
        out_specs=[full(s5.shape[1:]), full(s5.shape[1:])],
        out_shape=[jax.ShapeDtypeStruct(s5.shape[1:], F32)] * 2,
        compiler_params=pltpu.CompilerParams(vmem_limit_bytes=VMEM_LIMIT),
        name="s5_scan",
    )(s5[0], s5[1], lr, li)
    x = jnp.stack([xr, xi]).reshape(2, n_c, bsz, n_g, n_s).transpose(3, 2, 1, 0, 4)
    x = x.reshape(n_g, rows, 2 * n_s)
    y = pl.pallas_call(
        _s5_out_kernel,
        grid=(n_g,),
        in_specs=[grp(rows, kd), grp(rows, 2 * n_s), grp(kd, kd), grp(2 * n_s, kd)],
        out_specs=grp(rows, kd),
        out_shape=jax.ShapeDtypeStruct((n_g, rows, kd), F32),
        compiler_params=_cparams(("parallel",)),
        name="s5_out",
    )(ug, x, m, cc)
    return (y.reshape(n_g, bsz, n_c, tc, n_p).transpose(1, 2, 3, 0, 4)
            .reshape(bsz * seq, n_g * n_p))


def _compress_kernel(x_ref, pe_ref, w1_ref, w2_ref, o_ref):
    x = x_ref[...].astype(F32)
    nb = x.shape[0]
    a = _dot((x + pe_ref[0:1, :]).astype(BF16), w1_ref[0])
    b = _dot((x + pe_ref[1:2, :]).astype(BF16), w1_ref[1])
    pre = a + pltpu.roll(b, nb - 1, 0)
    o_ref[...] = _dot(jax.nn.gelu(pre).astype(BF16), w2_ref[...]).astype(o_ref.dtype)


def _compress(x, pe, w1, w2):
    bsz, hk, nb, kd = x.shape
    hd = w2.shape[1]
    return pl.pallas_call(
        _compress_kernel,
        grid=(bsz, hk),
        in_specs=[pl.BlockSpec((None, None, nb, kd), lambda b, h: (b, h, 0, 0)),
                  _resident(pe.shape), _resident(w1.shape), _resident(w2.shape)],
        out_specs=pl.BlockSpec((None, None, nb, hd), lambda b, h: (b, h, 0, 0)),
        out_shape=jax.ShapeDtypeStruct((bsz, hk, nb, hd), BF16),
        compiler_params=_cparams(("parallel", "parallel")),
        name="nsa_compress",
    )(x, pe, w1, w2)


def _softmax_step(s, carry, v):
    m, l, acc = carry
    m_new = jnp.maximum(m, jnp.max(s, axis=0, keepdims=True))
    alpha = jnp.exp(m - m_new)
    p = jnp.exp(s - m_new)
    l = alpha * l + jnp.sum(p, axis=0, keepdims=True)
    acc = alpha * acc + _dot(v, p.astype(BF16))
    return m_new, l, acc


def _nsa_kernel(q_ref, g_ref, kc_ref, vct_ref, ks_ref, vst_ref, kw_ref, vwt_ref, ov_ref,
                o_ref, sel_ref, *, tq, tk, top_n):
    gq, hd = NSA_GQA, NSA_HEAD_DIM
    n_sb = sel_ref.shape[0]
    nb = kc_ref.shape[0]
    wd = gq * tq
    i = pl.program_id(2)
    s0 = i * tq
    q = q_ref[...]
    qt = jnp.concatenate([q[g * hd:(g + 1) * hd, :] for g in range(gq)], axis=1).astype(BF16)
    t1 = s0 + lax.broadcasted_iota(I32, (1, tq), 1)
    t4 = s0 + (lax.broadcasted_iota(I32, (1, wd), 1) & (tq - 1))

    sc = _dot(kc_ref[...], qt)
    c_last = lax.broadcasted_iota(I32, (nb, 1), 0) * CMP_STRIDE + (CMP_LEN - 1)
    m_c = c_last <= t4
    sm = jnp.where(m_c, sc, NEG)
    e = jnp.exp(sm - jnp.max(sm, axis=0, keepdims=True))
    p_c = jnp.where(m_c, e / jnp.sum(e, axis=0, keepdims=True), 0.0)
    o_c = _dot(vct_ref[...], p_c.astype(BF16))
    psum = p_c[:, 0:tq]
    for g in range(1, gq):
        psum = psum + p_c[:, g * tq:(g + 1) * tq]
    hi = psum.astype(BF16)
    lo = (psum - hi.astype(F32)).astype(BF16)
    imp = _dot(ov_ref[...], hi) + _dot(ov_ref[...], lo)
    blk = lax.broadcasted_iota(I32, (n_sb, 1), 0)
    cur = t1 >> int(math.log2(SEL_BLOCK))
    forced = (blk == 0) | (blk == cur) | (blk == cur - 1)
    imp = imp + jnp.where(forced, FORCE_BONUS, 0.0)
    imp = jnp.where(blk * SEL_BLOCK <= t1, imp, NEG)

    rowf = lax.broadcasted_iota(I32, (n_sb, tq), 0).astype(F32)
    sel = jnp.zeros((n_sb, tq), F32)
    x = imp
    for _ in range(top_n):
        mx = jnp.max(x, axis=0, keepdims=True)
        first = jnp.min(jnp.where(x == mx, rowf, float(n_sb)), axis=0, keepdims=True)
        hit = rowf == first
        sel = jnp.where(hit, 1.0, sel)
        x = jnp.where(hit, -jnp.inf, x)
    sel_ref[...] = sel

    init = (jnp.full((1, wd), NEG, F32), jnp.zeros((1, wd), F32), jnp.zeros((hd, wd), F32))

    bpt = tk // SEL_BLOCK
    key_row = lax.broadcasted_iota(I32, (tk, 1), 0)

    def sel_tile(j, carry):
        s = _dot(ks_ref[j], qt)
        rows = [jnp.broadcast_to(sel_ref[pl.ds(j * bpt + b, 1), :], (SEL_BLOCK, tq))
                for b in range(bpt)]
        keep = (jnp.concatenate(rows, axis=0) > 0.5) & (j * tk + key_row <= t1)
        bias = jnp.where(keep, 0.0, NEG)
        s = s + jnp.concatenate([bias] * gq, axis=1)
        return _softmax_step(s, carry, vst_ref[j])

    _, l_s, a_s = lax.fori_loop(0, s0 // tk + 1, sel_tile, init)
    o_s = a_s / l_s

    n_wt = (WINDOW + tq) // LANE
    j_lo = jnp.clip(i * (tq // LANE) - WINDOW // LANE, 0, kw_ref.shape[0] - n_wt)
    kwin = jnp.concatenate([kw_ref[j_lo + w] for w in range(n_wt)], axis=0)
    vwin = jnp.concatenate([vwt_ref[j_lo + w] for w in range(n_wt)], axis=1)
    sw = _dot(kwin, qt)
    diff = t4 - (j_lo * LANE + lax.broadcasted_iota(I32, (n_wt * LANE, 1), 0))
    sw = jnp.where((diff >= 0) & (diff < WINDOW), sw, NEG)
    pw = jnp.exp(sw - jnp.max(sw, axis=0, keepdims=True))
    o_w = _dot(vwin, pw.astype(BF16)) / jnp.sum(pw, axis=0, keepdims=True)

    gt = g_ref[...]
    for g in range(gq):
        cs = slice(g * tq, (g + 1) * tq)
        o_ref[g * hd:(g + 1) * hd, :] = (gt[3 * g:3 * g + 1, :] * o_c[:, cs]
                                         + gt[3 * g + 1:3 * g + 2, :] * o_s[:, cs]
                                         + gt[3 * g + 2:3 * g + 3, :] * o_w[:, cs])


def _nsa(q, gates, kcmp, vcmp_t, ks, vs_t, kw, vw_t, ov_t, seq):
    bsz = q.shape[0]
    tq = min(TQ_NSA, seq)
    tk = min(TK_NSA, seq)
    n_sb = seq // SEL_BLOCK
    gq, hd = NSA_GQA, NSA_HEAD_DIM
    per_head = lambda a: pl.BlockSpec((None, None) + a.shape[2:],
                                      lambda b, h, i: (b, h) + (0,) * (a.ndim - 2),
                                      pipeline_mode=pl.Buffered(1))
    return pl.pallas_call(
        functools.partial(_nsa_kernel, tq=tq, tk=tk, top_n=min(SEL_TOP, n_sb)),
        grid=(bsz, NSA_KV_HEADS, seq // tq),
        in_specs=[pl.BlockSpec((None, gq * hd, tq), lambda b, h, i: (b, h, i)),
                  pl.BlockSpec((None, None, 3 * gq, tq), lambda b, h, i: (b, h, 0, i)),
                  per_head(kcmp), per_head(vcmp_t), per_head(ks), per_head(vs_t),
                  per_head(kw), per_head(vw_t), _resident(ov_t.shape)],
        out_specs=pl.BlockSpec((None, gq * hd, tq), lambda b, h, i: (b, h, i)),
        out_shape=jax.ShapeDtypeStruct((bsz, NSA_Q, seq), F32),
        scratch_shapes=[pltpu.VMEM((n_sb, tq), F32)],
        compiler_params=_cparams(("parallel", "parallel", "arbitrary")),
        name="nsa",
    )(q, gates, kcmp, vcmp_t, ks, vs_t, kw, vw_t, ov_t)


def _dsa_kernel(q_ref, qi_ref, wi_ref, kcat_ref, ckt_ref, kidx_ref, wuv_ref, o_ref,
                ihi, ilo, acc_ref, m_ref, l_ref, *, tq, top):
    tk = tq
    n_h = MLA_HEADS
    i16 = jnp.int16
    i = pl.program_id(1)
    row = lax.broadcasted_iota(I32, (tk, 1), 0)
    col = lax.broadcasted_iota(I32, (1, tq), 1)
    causal = row <= col

    def tile(ref, j):
        return ref.at[pl.ds(pl.multiple_of(j * tk, tk), tk), :]

    qi = qi_ref[...]
    wi = wi_ref[...] * (IDX_HEADS ** -0.5) * (IDX_DIM ** -0.5)

    def idx_scores(j):
        kj = kidx_ref[j]
        sc = jnp.zeros((tk, tq), F32)
        for h in range(IDX_HEADS):
            d = _dot(kj, qi[h * IDX_DIM:(h + 1) * IDX_DIM, :])
            sc = sc + wi[h:h + 1, :] * jnp.maximum(d, 0.0)
        return sc

    def store_keys(j, sc):
        b = pltpu.bitcast(sc, I32)
        key = b ^ ((b >> 31) & 0x7FFFFFFF)
        tile(ihi, j)[...] = (key >> 16).astype(i16)
        tile(ilo, j)[...] = ((key & 0xFFFF) - 32768).astype(i16)

    def idx_tile(j, _):
        store_keys(j, idx_scores(j))
        return 0

    lax.fori_loop(0, i, idx_tile, 0)
    store_keys(i, jnp.where(causal, idx_scores(i), NEG))

    def count(ref, pred):
        def body(j, c):
            hit = jnp.where(pred(tile(ref, j)[...]), jnp.ones((), i16), jnp.zeros((), i16))
            parts = [hit[k * 16:(k + 1) * 16] for k in range(tk // 16)]
            while len(parts) > 1:
                parts = [a + b for a, b in zip(parts[0::2], parts[1::2])]
            return c + parts[0]
        c16 = lax.fori_loop(0, i + 1, body, jnp.zeros((16, tq), i16))
        return jnp.sum(c16.astype(I32), axis=0, keepdims=True)

    def search(ref, need):
        def bit_step(b, thr):
            cand = thr + (jnp.int32(1) << (15 - b))
            cand16 = cand.astype(i16)
            return jnp.where(count(ref, lambda t: t >= cand16) >= need, cand, thr)
        return lax.fori_loop(0, 16, bit_step, jnp.full((1, tq), -32768, I32))

    thr_hi = search(ihi, top)
    thr_hi16 = thr_hi.astype(i16)
    need = top - count(ihi, lambda t: t > thr_hi16)

    def mask_lo(j, _):
        lo = tile(ilo, j)
        lo[...] = jnp.where(tile(ihi, j)[...] == thr_hi16, lo[...], jnp.full((), -32768, i16))
        return 0

    lax.fori_loop(0, i + 1, mask_lo, 0)
    thr_lo = search(ilo, need)

    m_ref[...] = jnp.full(m_ref.shape, NEG, F32)
    l_ref[...] = jnp.zeros(l_ref.shape, F32)
    acc_ref[...] = jnp.zeros(acc_ref.shape, F32)

    def att_tile(j, diag):
        hi = tile(ihi, j)[...].astype(I32)
        lo = tile(ilo, j)[...].astype(I32)
        keep = (hi > thr_hi) | ((hi == thr_hi) & (lo >= thr_lo))
        bias = jnp.where(keep, 0.0, NEG)
        if diag:
            bias = jnp.where(causal, bias, NEG)
        s_all = _dot(kcat_ref[j], q_ref[...])
        ps, alphas = [], []
        for h in range(n_h):
            cs = slice(h * tq, (h + 1) * tq)
            s = s_all[:, cs] + bias
            m_old = m_ref[:, cs]
            m_new = jnp.maximum(m_old, jnp.max(s, axis=0, keepdims=True))
            alpha = jnp.exp(m_old - m_new)
            p = jnp.exp(s - m_new)
            l_ref[:, cs] = alpha * l_ref[:, cs] + jnp.sum(p, axis=0, keepdims=True)
            m_ref[:, cs] = m_new
            ps.append(p.astype(BF16))
            alphas.append(alpha)
        p_all = jnp.concatenate(ps, axis=1)
        acc_ref[...] = acc_ref[...] * jnp.concatenate(alphas, axis=1) + _dot(ckt_ref[j], p_all)

    def att_body(j, _):
        att_tile(j, False)
        return 0

    lax.fori_loop(0, i, att_body, 0)
    att_tile(i, True)

    for h in range(n_h):
        cs = slice(h * tq, (h + 1) * tq)
        o = acc_ref[:, cs] / l_ref[:, cs]
        o_ref[h * MLA_V_DIM:(h + 1) * MLA_V_DIM, :] = _dot(wuv_ref[h], o.astype(BF16))


def _dsa(q, qi, wi, kcat, ckt, kidx, wuv_t, seq):
    bsz = q.shape[0]
    tq = min(TQ_DSA, seq)
    top = min(DSA_TOP, seq // 4)
    wd = MLA_HEADS * tq
    per_b = lambda a: pl.BlockSpec((None,) + a.shape[1:], lambda b, i: (b,) + (0,) * (a.ndim - 1),
                                   pipeline_mode=pl.Buffered(1))
    return pl.pallas_call(
        functools.partial(_dsa_kernel, tq=tq, top=top),
        grid=(bsz, seq // tq),
        in_specs=[pl.BlockSpec((None, None, MLA_QK, wd), lambda b, i: (b, i, 0, 0)),
                  pl.BlockSpec((None, IDX_HEADS * IDX_DIM, tq), lambda b, i: (b, 0, i)),
                  pl.BlockSpec((None, IDX_HEADS, tq), lambda b, i: (b, 0, i)),
                  per_b(kcat), per_b(ckt), per_b(kidx), _resident(wuv_t.shape)],
        out_specs=pl.BlockSpec((None, MLA_HEADS * MLA_V_DIM, tq), lambda b, i: (b, 0, i)),
        out_shape=jax.ShapeDtypeStruct((bsz, MLA_HEADS * MLA_V_DIM, seq), F32),
        scratch_shapes=[pltpu.VMEM((seq, tq), jnp.int16),
                        pltpu.VMEM((seq, tq), jnp.int16),
                        pltpu.VMEM((MLA_LATENT, wd), F32),
                        pltpu.VMEM((1, wd), F32),
                        pltpu.VMEM((1, wd), F32)],
        compiler_params=_cparams(("parallel", "arbitrary")),
        name="dsa",
    )(q, qi, wi, kcat, ckt, kidx, wuv_t)


def _rope_tables(seq, dim):
    half = dim // 2
    inv = ROPE_THETA ** (-jnp.arange(half, dtype=F32) / half)
    ang = inv[:, None] * jnp.arange(seq, dtype=F32)[None, :]
    return jnp.cos(ang), jnp.sin(ang)


def _even_mixer(h, g, bsz, seq, w_in, w_out, s5, pe_k, pe_v, wk1, wk2, wv1, wv2, tabs):
    cos32, sin32 = tabs[64]
    hd, hk = NSA_HEAD_DIM, NSA_KV_HEADS
    o = 0
    cols = {}
    for name, size in (("u", S5_WIDTH), ("q", NSA_Q), ("kc", NSA_KV), ("vc", NSA_KV), ("ks", NSA_KV),
                       ("vs", NSA_KV), ("kw", NSA_KV), ("vw", NSA_KV), ("gt", 3 * NSA_HEADS)):
        cols[name] = w_in[:, o:o + size]
        o += size
    wu = cols["u"].astype(BF16)
    wt = jnp.concatenate([cols[n] for n in ("q", "kc", "ks", "kw", "vc", "vs", "vw", "gt")], axis=1).T
    wt = jnp.pad(wt, ((0, (-wt.shape[0]) % 16), (0, 0))).astype(BF16)
    u, q_t, k_t, v_t, gt_t = _proj_even(h, g, wu, wt, cos32, sin32, bsz, seq)

    a_re, a_im, b_re, b_im, c_re, c_im, log_dt, d_skip, w_glu = s5
    mats = _s5_matrices(a_re, a_im, b_re, b_im, c_re, c_im, log_dt, S5_CHUNK)
    ys = _s5_scan(u, mats, bsz, seq)

    k5 = k_t.reshape(bsz, 3, hk, hd, seq)
    v5 = v_t.reshape(bsz, 3, hk, hd, seq)
    nb = seq // CMP_STRIDE
    half_blk = lambda a: (a.reshape(bsz, hk, hd, nb, CMP_STRIDE).transpose(0, 1, 3, 4, 2)
                          .reshape(bsz, hk, nb, CMP_STRIDE * hd))
    pe2 = lambda pe: pe.reshape(2, CMP_STRIDE * hd)
    w1s = lambda w: w.reshape(2, CMP_STRIDE * hd, hd).astype(BF16)
    kcmp = _compress(half_blk(k5[:, 0]), pe2(pe_k), w1s(wk1), wk2.astype(BF16))
    vcmp = _compress(half_blk(v5[:, 0]), pe2(pe_v), w1s(wv1), wv2.astype(BF16))
    vcmp_t = vcmp.transpose(0, 1, 3, 2)
    tk = min(TK_NSA, seq)
    ks = k5[:, 1].reshape(bsz, hk, hd, seq // tk, tk).transpose(0, 1, 3, 4, 2)
    vs_t = v5[:, 1].reshape(bsz, hk, hd, seq // tk, tk).transpose(0, 1, 3, 2, 4)
    kw = k5[:, 2].reshape(bsz, hk, hd, seq // LANE, LANE).transpose(0, 1, 3, 4, 2)
    vw_t = v5[:, 2].reshape(bsz, hk, hd, seq // LANE, LANE).transpose(0, 1, 3, 2, 4)
    n_sb = seq // SEL_BLOCK
    c_start = jnp.arange(nb) * CMP_STRIDE
    b_start = jnp.arange(n_sb) * SEL_BLOCK
    ov_t = ((c_start[None, :] < b_start[:, None] + SEL_BLOCK)
            & (c_start[None, :] + CMP_LEN > b_start[:, None])
            & (jnp.arange(nb)[None, :] < nb - 1)).astype(BF16)
    gates = gt_t.reshape(bsz, hk, 3 * NSA_GQA, seq)
    b_t = _nsa(q_t, gates, kcmp, vcmp_t, ks, vs_t, kw, vw_t, ov_t, seq)
    b_out = b_t.transpose(0, 2, 1).reshape(bsz * seq, NSA_Q)
    return _outproj_even(h, ys, u, b_out, g, d_skip.reshape(1, -1), w_glu.astype(BF16),
                         w_out[:S5_WIDTH].astype(BF16), w_out[S5_WIDTH:].astype(BF16))


def _odd_mixer(h, g, bsz, seq, w_in, kv_norm, w_uv, w_out, tabs):
    cos32, sin32 = tabs[64]
    cos16, sin16 = tabs[32]
    d = h.shape[1]
    sizes = (MLA_HEADS * MLA_LATENT, MLA_HEADS * MLA_ROPE, MLA_LATENT, MLA_ROPE,
             IDX_HEADS * IDX_DIM, IDX_DIM, IDX_HEADS)
    parts = []
    o = 0
    for s in sizes:
        parts.append(w_in[:, o:o + s])
        o += s
    w_ql, w_qr, w_c, w_kr, w_qi, w_ki, w_wi = parts
    wq = jnp.concatenate([w_ql.reshape(d, MLA_HEADS, MLA_LATENT), w_qr.reshape(d, MLA_HEADS, MLA_ROPE)],
                         axis=2).reshape(d, MLA_HEADS * MLA_QK).T.astype(BF16)
    wkv = jnp.concatenate([w_c, w_kr, w_qi, w_ki, w_wi], axis=1).T
    wkv = jnp.pad(wkv, ((0, (-wkv.shape[0]) % 16), (0, 0))).astype(BF16)
    ckv_t, kr_t, qi_t, ki_t, wi_t = _proj_odd_kv(h, g, wkv, kv_norm.reshape(-1, 1), cos32, sin32,
                                                 cos16, sin16, bsz, seq)
    q = _proj_odd_q(h, g, wq, cos16, sin16, bsz, seq)
    tk = min(TQ_DSA, seq)
    nk = seq // tk
    kcat = jnp.concatenate([ckv_t, kr_t], axis=1).transpose(0, 2, 1).reshape(bsz, nk, tk, MLA_QK)
    ckt = ckv_t.reshape(bsz, MLA_LATENT, nk, tk).transpose(0, 2, 1, 3)
    kidx = ki_t.transpose(0, 2, 1).reshape(bsz, nk, tk, IDX_DIM)
    wuv_t = w_uv.transpose(0, 2, 1).astype(BF16)
    o_t = _dsa(q, qi_t, wi_t, kcat, ckt, kidx, wuv_t, seq)
    mix = o_t.transpose(0, 2, 1).reshape(bsz * seq, MLA_HEADS * MLA_V_DIM)
    return _outproj_odd(h, mix, g, w_out.astype(BF16))


def kernel(x, p, norm_g, ffn1_w_in, ffn1_w_out, ffn2_w_in, ffn2_w_out, ple_w_gate, ple_w_proj,
           ev_w_in, ev_w_out, s5_a_re, s5_a_im, s5_b_re, s5_b_im, s5_c_re, s5_c_im, s5_log_dt, s5_d,
           s5_w_glu, nsa_pe_k, nsa_pe_v, nsa_wk1, nsa_wk2, nsa_wv1, nsa_wv2,
           od_w_in, od_kv_norm, od_w_uv, od_w_out):
    bsz, seq, d = x.shape
    depth = norm_g.shape[0]
    tabs = {64: _rope_tables(seq, 64), 32: _rope_tables(seq, 32)}
    h = x.reshape(bsz * seq, d)
    for i in range(depth):
        g = norm_g[i]
        j = i // 2
        h = _ffn(h, g, ffn1_w_in[i].astype(BF16), ffn1_w_out[i].astype(BF16), 0)
        if i % 2 == 0:
            s5 = (s5_a_re[j], s5_a_im[j], s5_b_re[j], s5_b_im[j], s5_c_re[j], s5_c_im[j],
                  s5_log_dt[j], s5_d[j], s5_w_glu[j])
            h = _even_mixer(h, g, bsz, seq, ev_w_in[j], ev_w_out[j], s5, nsa_pe_k[j], nsa_pe_v[j],
                            nsa_wk1[j], nsa_wk2[j], nsa_wv1[j], nsa_wv2[j], tabs)
        else:
            h = _odd_mixer(h, g, bsz, seq, od_w_in[j], od_kv_norm[j], od_w_uv[j], od_w_out[j], tabs)
        h = _ffn(h, g, ffn2_w_in[i].astype(BF16), ffn2_w_out[i].astype(BF16), 4)
        h = _ple(h, p[i].reshape(bsz * seq, -1), g, ple_w_gate[i].astype(BF16),
                 ple_w_proj[i].astype(BF16))
    return h.reshape(bsz, seq, d)
```

```python
import functools
import math

import jax
import jax.numpy as jnp
from jax import lax
from jax.experimental import pallas as pl
from jax.experimental.pallas import tpu as pltpu

F32 = jnp.float32
BF16 = jnp.bfloat16
I32 = jnp.int32

ROPE_THETA = 10000.0
EPS = 1e-6
NEG = -1e30
LOG2E = math.log2(math.e)
D_FF = 2816
S5_WIDTH = 512
S5_GROUP = 16
S5_GROUPS = S5_WIDTH // S5_GROUP
S5_STATE = 64
NSA_HEADS = 8
NSA_KV_HEADS = 2
NSA_GQA = NSA_HEADS // NSA_KV_HEADS
NSA_HEAD_DIM = 64
CMP_LEN = 32
CMP_STRIDE = 16
SEL_BLOCK = 64
SEL_TOP = 16
WINDOW = 512
FORCE_BONUS = 1000.0
NSA_Q = NSA_HEADS * NSA_HEAD_DIM
NSA_KV = NSA_KV_HEADS * NSA_HEAD_DIM
MLA_HEADS = 16
MLA_LATENT = 256
MLA_ROPE = 32
MLA_QK = MLA_LATENT + MLA_ROPE
MLA_V_DIM = 64
IDX_HEADS = 8
IDX_DIM = 64
DSA_TOP = 256

V7X_VMEM_BYTES = 64 * 2**20
VMEM_LIMIT = V7X_VMEM_BYTES - 8 * 2**20
LANE = 128
TM_FFN = 512
TQ_PROJ = 256
TQ_NSA = 128
TK_NSA = 1024
TQ_DSA = 256
S5_CHUNK = 64


def _cparams(sem):
    return pltpu.CompilerParams(dimension_semantics=sem, vmem_limit_bytes=VMEM_LIMIT)


def _resident(shape):
    nd = len(shape)
    return pl.BlockSpec(shape, lambda *_: (0,) * nd, pipeline_mode=pl.Buffered(1))


def _dot(a, b):
    return jnp.dot(a, b, preferred_element_type=F32)


def _dot_nt(a, b):
    return lax.dot_general(a, b, (((1,), (1,)), ((), ())), preferred_element_type=F32)


def _rms_rows(x, g):
    return x * lax.rsqrt(jnp.mean(x * x, axis=-1, keepdims=True) + EPS) * g


def _rope_fmaj(y, cos, sin):
    half = y.shape[1] // 2
    t1 = y[:, :half, :]
    t2 = y[:, half:, :]
    return jnp.concatenate([t1 * cos - t2 * sin, t2 * cos + t1 * sin], axis=1)


def _ffn_tile(x, g_ref, g0, win_ref, wout_ref, n_chunk):
    xn = _rms_rows(x, g_ref[g0:g0 + 1, :]).astype(BF16)
    ck = D_FF // n_chunk
    acc = None
    for c in range(n_chunk):
        a = _dot(xn, win_ref[:, c * ck:(c + 1) * ck])
        u = _dot(xn, win_ref[:, D_FF + c * ck:D_FF + (c + 1) * ck])
        act = (jax.nn.silu(a) * u).astype(BF16)
        y = _dot(act, wout_ref[c * ck:(c + 1) * ck, :])
        acc = y if acc is None else acc + y
    return x + 0.5 * _rms_rows(acc, g_ref[g0 + 1:g0 + 2, :])


def _ffn_kernel(h_ref, g_ref, win_ref, wout_ref, o_ref, *, g0, n_chunk):
    o_ref[...] = _ffn_tile(h_ref[...], g_ref, g0, win_ref, wout_ref, n_chunk)


def _ffn_ple_kernel(h_ref, p_ref, g_ref, win_ref, wout_ref, wg_ref, wp_ref, o_ref, *, n_chunk):
    x = _ffn_tile(h_ref[...], g_ref, 4, win_ref, wout_ref, n_chunk)
    gate = jax.nn.sigmoid(_dot(_rms_rows(x, g_ref[6:7, :]).astype(BF16), wg_ref[...]))
    e = _dot(p_ref[...].astype(BF16), wp_ref[...]) * gate
    o_ref[...] = x + _rms_rows(e, g_ref[7:8, :])


def _ffn_ple(h, p, g, w_in, w_out, w_gate, w_proj):
    t, d = h.shape
    tm = min(TM_FFN, t)
    return pl.pallas_call(
        functools.partial(_ffn_ple_kernel, n_chunk=2),
        grid=(t // tm,),
        in_specs=[pl.BlockSpec((tm, d), lambda i: (i, 0)),
                  pl.BlockSpec((tm, p.shape[1]), lambda i: (i, 0)), _resident(g.shape),
                  _resident(w_in.shape), _resident(w_out.shape), _resident(w_gate.shape),
                  _resident(w_proj.shape)],
        out_specs=pl.BlockSpec((tm, d), lambda i: (i, 0)),
        out_shape=jax.ShapeDtypeStruct((t, d), F32),
        compiler_params=_cparams(("parallel",)),
        name="ffn_ple",
    )(h, p, g, w_in, w_out, w_gate, w_proj)


def _ffn(h, g, w_in, w_out, g0):
    t, d = h.shape
    tm = min(TM_FFN, t)
    return pl.pallas_call(
        functools.partial(_ffn_kernel, g0=g0, n_chunk=2),
        grid=(t // tm,),
        in_specs=[pl.BlockSpec((tm, d), lambda i: (i, 0)), _resident(g.shape),
                  _resident(w_in.shape), _resident(w_out.shape)],
        out_specs=pl.BlockSpec((tm, d), lambda i: (i, 0)),
        out_shape=jax.ShapeDtypeStruct((t, d), F32),
        compiler_params=_cparams(("parallel",)),
        name="ffn",
    )(h, g, w_in, w_out)


def _outproj_even_kernel(h_ref, ys_ref, u_ref, b_ref, g_ref, d_ref, wglu_ref, wa_ref, wb_ref, o_ref):
    y = ys_ref[...] + d_ref[...] * u_ref[...]
    z = jax.nn.gelu(y)
    a = z * jax.nn.sigmoid(_dot(z.astype(BF16), wglu_ref[...]))
    mix = _dot(a.astype(BF16), wa_ref[...]) + _dot(b_ref[...].astype(BF16), wb_ref[...])
    o_ref[...] = h_ref[...] + _rms_rows(mix, g_ref[3:4, :])


def _outproj_even(h, ys, u, b_out, g, d_skip, w_glu, w_a, w_b):
    t, d = h.shape
    tm = min(TM_FFN, t)
    tok = lambda w: pl.BlockSpec((tm, w), lambda i: (i, 0))
    return pl.pallas_call(
        _outproj_even_kernel,
        grid=(t // tm,),
        in_specs=[tok(d), tok(S5_WIDTH), tok(S5_WIDTH), tok(NSA_Q), _resident(g.shape),
                  _resident(d_skip.shape), _resident(w_glu.shape), _resident(w_a.shape),
                  _resident(w_b.shape)],
        out_specs=tok(d),
        out_shape=jax.ShapeDtypeStruct((t, d), F32),
        compiler_params=_cparams(("parallel",)),
        name="outproj_even",
    )(h, ys, u, b_out, g, d_skip, w_glu, w_a, w_b)


def _proj_even_kernel(h_ref, g_ref, wu_ref, wt_ref, cos_ref, sin_ref,
                      u_ref, q_ref, k_ref, v_ref, gt_ref):
    xn = _rms_rows(h_ref[...], g_ref[2:3, :]).astype(BF16)
    u_ref[...] = _dot(xn, wu_ref[...])
    y = _dot_nt(wt_ref[...], xn)
    tq = y.shape[1]
    cos = cos_ref[...]
    sin = sin_ref[...]
    hd = NSA_HEAD_DIM
    q = _rope_fmaj(y[0:NSA_Q].reshape(NSA_HEADS, hd, tq), cos, sin)
    q_ref[...] = (q * (hd ** -0.5 * LOG2E)).reshape(NSA_Q, tq)
    k0 = NSA_Q
    nk = 3 * NSA_KV
    k = _rope_fmaj(y[k0:k0 + nk].reshape(3 * NSA_KV_HEADS, hd, tq), cos, sin)
    k_ref[...] = k.reshape(nk, tq).astype(k_ref.dtype)
    v0 = k0 + nk
    v_ref[...] = y[v0:v0 + nk].astype(v_ref.dtype)
    g0 = v0 + nk
    gt_ref[...] = jax.nn.sigmoid(y[g0:g0 + 3 * NSA_HEADS])


def _proj_even(h, g, wu, wt, cos, sin, bsz, seq):
    d = h.shape[1]
    tq = min(TQ_PROJ, seq)
    nq = seq // tq
    nk = 3 * NSA_KV
    fm = lambda rows: pl.BlockSpec((None, rows, tq), lambda b, i: (b, 0, i))
    return pl.pallas_call(
        _proj_even_kernel,
        grid=(bsz, nq),
        in_specs=[pl.BlockSpec((tq, d), lambda b, i: (b * nq + i, 0)), _resident(g.shape),
                  _resident(wu.shape), _resident(wt.shape),
                  pl.BlockSpec((cos.shape[0], tq), lambda b, i: (0, i)),
                  pl.BlockSpec((sin.shape[0], tq), lambda b, i: (0, i))],
        out_specs=[pl.BlockSpec((tq, S5_WIDTH), lambda b, i: (b * nq + i, 0)),
                   fm(NSA_Q), fm(nk), fm(nk), fm(3 * NSA_HEADS)],
        out_shape=[jax.ShapeDtypeStruct((bsz * seq, S5_WIDTH), F32),
                   jax.ShapeDtypeStruct((bsz, NSA_Q, seq), F32),
                   jax.ShapeDtypeStruct((bsz, nk, seq), BF16),
                   jax.ShapeDtypeStruct((bsz, nk, seq), BF16),
                   jax.ShapeDtypeStruct((bsz, 3 * NSA_HEADS, seq), F32)],
        compiler_params=_cparams(("parallel", "parallel")),
        name="proj_even",
    )(h, g, wu, wt, cos, sin)


def _proj_odd_kv_kernel(h_ref, g_ref, wt_ref, kvn_ref, cos32_ref, sin32_ref, cos16_ref, sin16_ref,
                        ckv_ref, kr_ref, qi_ref, ki_ref, wi_ref):
    xn = _rms_rows(h_ref[...], g_ref[2:3, :]).astype(BF16)
    y = _dot_nt(wt_ref[...], xn)
    tq = y.shape[1]
    c = y[0:MLA_LATENT]
    c = c * lax.rsqrt(jnp.mean(c * c, axis=0, keepdims=True) + EPS) * kvn_ref[...]
    ckv_ref[...] = c.astype(ckv_ref.dtype)
    r0 = MLA_LATENT
    kr = _rope_fmaj(y[r0:r0 + MLA_ROPE].reshape(1, MLA_ROPE, tq), cos16_ref[...], sin16_ref[...])
    kr_ref[...] = kr.reshape(MLA_ROPE, tq).astype(kr_ref.dtype)
    q0 = r0 + MLA_ROPE
    nqi = IDX_HEADS * IDX_DIM
    qi = _rope_fmaj(y[q0:q0 + nqi].reshape(IDX_HEADS, IDX_DIM, tq), cos32_ref[...], sin32_ref[...])
    qi_ref[...] = qi.reshape(nqi, tq).astype(qi_ref.dtype)
    k0 = q0 + nqi
    ki = _rope_fmaj(y[k0:k0 + IDX_DIM].reshape(1, IDX_DIM, tq), cos32_ref[...], sin32_ref[...])
    ki_ref[...] = ki.reshape(IDX_DIM, tq).astype(ki_ref.dtype)
    w0 = k0 + IDX_DIM
    wi_ref[...] = y[w0:w0 + IDX_HEADS]


def _proj_odd_kv(h, g, wt, kvn, cos32, sin32, cos16, sin16, bsz, seq):
    d = h.shape[1]
    tq = min(TQ_PROJ, seq)
    nq = seq // tq
    fm = lambda rows: pl.BlockSpec((None, rows, tq), lambda b, i: (b, 0, i))
    tab = lambda a: pl.BlockSpec((a.shape[0], tq), lambda b, i: (0, i))
    nqi = IDX_HEADS * IDX_DIM
    return pl.pallas_call(
        _proj_odd_kv_kernel,
        grid=(bsz, nq),
        in_specs=[pl.BlockSpec((tq, d), lambda b, i: (b * nq + i, 0)), _resident(g.shape),
                  _resident(wt.shape), _resident(kvn.shape),
                  tab(cos32), tab(sin32), tab(cos16), tab(sin16)],
        out_specs=[fm(MLA_LATENT), fm(MLA_ROPE), fm(nqi), fm(IDX_DIM), fm(IDX_HEADS)],
        out_shape=[jax.ShapeDtypeStruct((bsz, MLA_LATENT, seq), BF16),
                   jax.ShapeDtypeStruct((bsz, MLA_ROPE, seq), BF16),
                   jax.ShapeDtypeStruct((bsz, nqi, seq), BF16),
                   jax.ShapeDtypeStruct((bsz, IDX_DIM, seq), BF16),
                   jax.ShapeDtypeStruct((bsz, IDX_HEADS, seq), F32)],
        compiler_params=_cparams(("parallel", "parallel")),
        name="proj_odd_kv",
    )(h, g, wt, kvn, cos32, sin32, cos16, sin16)


def _proj_odd_q_kernel(h_ref, g_ref, wt_ref, cos16_ref, sin16_ref, q_ref):
    xn = _rms_rows(h_ref[...], g_ref[2:3, :]).astype(BF16)
    tq = xn.shape[0]
    cos = cos16_ref[...]
    sin = sin16_ref[...]
    for hh in range(MLA_HEADS):
        y = _dot_nt(wt_ref[hh * MLA_QK:(hh + 1) * MLA_QK, :], xn)
        r = _rope_fmaj(y[MLA_LATENT:].reshape(1, MLA_ROPE, tq), cos, sin)
        q = jnp.concatenate([y[:MLA_LATENT], r.reshape(MLA_ROPE, tq)], axis=0)
        q_ref[:, hh * tq:(hh + 1) * tq] = (q * (MLA_QK ** -0.5 * LOG2E)).astype(q_ref.dtype)


def _proj_odd_q(h, g, wt, cos16, sin16, bsz, seq):
    d = h.shape[1]
    tq = min(TQ_DSA, seq)
    nq = seq // tq
    return pl.pallas_call(
        _proj_odd_q_kernel,
        grid=(bsz, nq),
        in_specs=[pl.BlockSpec((tq, d), lambda b, i: (b * nq + i, 0)), _resident(g.shape),
                  _resident(wt.shape),
                  pl.BlockSpec((cos16.shape[0], tq), lambda b, i: (0, i)),
                  pl.BlockSpec((sin16.shape[0], tq), lambda b, i: (0, i))],
        out_specs=pl.BlockSpec((None, None, MLA_QK, MLA_HEADS * tq), lambda b, i: (b, i, 0, 0)),
        out_shape=jax.ShapeDtypeStruct((bsz, nq, MLA_QK, MLA_HEADS * tq), BF16),
        compiler_params=_cparams(("parallel", "parallel")),
        name="proj_odd_q",
    )(h, g, wt, cos16, sin16)


def _s5_state_kernel(u_ref, bc_ref, s_ref):
    s_ref[...] = _dot(u_ref[...], bc_ref[...])


def _s5_scan_kernel(sr_ref, si_ref, lr_ref, li_ref, xr_ref, xi_ref):
    n_chunk = sr_ref.shape[0]
    lr = lr_ref[...]
    li = li_ref[...]

    def body(c, carry):
        xr, xi = carry
        xr_ref[c] = xr
        xi_ref[c] = xi
        return (lr * xr - li * xi + sr_ref[c], lr * xi + li * xr + si_ref[c])

    zero = jnp.zeros(sr_ref.shape[1:], F32)
    lax.fori_loop(0, n_chunk, body, (zero, zero))


def _s5_out_kernel(u_ref, x_ref, m_ref, cc_ref, y_ref):
    x = x_ref[...]
    hi = x.astype(BF16)
    lo = (x - hi.astype(F32)).astype(BF16)
    cc = cc_ref[...]
    y_ref[...] = _dot(u_ref[...], m_ref[...]) + _dot(hi, cc) + _dot(lo, cc)


def _s5_matrices(a_re, a_im, b_re, b_im, c_re, c_im, log_dt, tc):
    hp = lax.Precision.HIGHEST
    dt = jnp.exp(log_dt)[:, None]
    lam = lax.complex(a_re, a_im)
    lam_dt = lam * dt
    lam_bar = jnp.exp(lam_dt)
    b_bar = ((lam_bar - 1.0) / lam)[..., None] * lax.complex(b_re, b_im)
    k = jnp.arange(tc + 1, dtype=F32)[:, None, None]
    pw = jnp.exp(lam_dt[None] * k)
    pr, pi = jnp.real(pw), jnp.imag(pw)
    bbr, bbi = jnp.real(b_bar), jnp.imag(b_bar)
    cpr = c_re[None] * pr[:, :, None, :] - c_im[None] * pi[:, :, None, :]
    cpi = c_re[None] * pi[:, :, None, :] + c_im[None] * pr[:, :, None, :]
    kk = (jnp.einsum('kgpn,gnq->kgpq', cpr[:tc], bbr, precision=hp)
          - jnp.einsum('kgpn,gnq->kgpq', cpi[:tc], bbi, precision=hp))
    n_g, n_p = a_re.shape[0], b_re.shape[2]
    kq = jnp.pad(kk.transpose(1, 3, 0, 2).astype(BF16), ((0, 0), (0, 0), (tc, 0), (0, 0)))
    lag = tc + jnp.arange(tc)[None, :] - jnp.arange(tc)[:, None]
    m = kq[:, :, lag, :].reshape(n_g, n_p * tc, tc * n_p)
    rev = pw[tc - 1 - jnp.arange(tc)]
    bc = rev[:, :, :, None] * b_bar[None]
    bc = bc.transpose(1, 3, 0, 2).reshape(n_g, n_p * tc, -1)
    bc = jnp.concatenate([jnp.real(bc), jnp.imag(bc)], axis=-1)
    mr = cpr[1:tc + 1].transpose(1, 3, 0, 2).reshape(n_g, -1, tc * n_p)
    mi = cpi[1:tc + 1].transpose(1, 3, 0, 2).reshape(n_g, -1, tc * n_p)
    cc = jnp.concatenate([mr, -mi], axis=1)
    ltc = pw[tc].reshape(1, -1)
    return m, bc.astype(BF16), cc.astype(BF16), jnp.real(ltc), jnp.imag(ltc)


def _s5_scan(u, mats, bsz, seq):
    m, bc, cc, lr, li = mats
    tc = S5_CHUNK
    n_c = seq // tc
    n_g, n_p, n_s = S5_GROUPS, S5_GROUP, S5_STATE
    rows = bsz * n_c
    kd = tc * n_p
    ug = (u.reshape(bsz, n_c, tc, n_g, n_p).transpose(3, 0, 1, 4, 2)
          .reshape(n_g, rows, kd).astype(BF16))
    grp = lambda a, b: pl.BlockSpec((None, a, b), lambda gi: (gi, 0, 0))
    s = pl.pallas_call(
        _s5_state_kernel,
        grid=(n_g,),
        in_specs=[grp(rows, kd), grp(kd, 2 * n_s)],
        out_specs=grp(rows, 2 * n_s),
        out_shape=jax.ShapeDtypeStruct((n_g, rows, 2 * n_s), F32),
        compiler_params=_cparams(("parallel",)),
        name="s5_state",
    )(ug, bc)
    s5 = s.reshape(n_g, bsz, n_c, 2, n_s).transpose(3, 2, 1, 0, 4).reshape(2, n_c, bsz, n_g * n_s)
    full = lambda shp: pl.BlockSpec(shp, lambda: (0,) * len(shp))
    xr, xi = pl.pallas_call(
        _s5_scan_kernel,
        in_specs=[full(s5.shape[1:]), full(s5.shape[1:]), full(lr.shape), full(li.shape)],
        out_specs=[full(s5.shape[1:]), full(s5.shape[1:])],
        out_shape=[jax.ShapeDtypeStruct(s5.shape[1:], F32)] * 2,
        compiler_params=pltpu.CompilerParams(vmem_limit_bytes=VMEM_LIMIT),
        name="s5_scan",
    )(s5[0], s5[1], lr, li)
    x = jnp.stack([xr, xi]).reshape(2, n_c, bsz, n_g, n_s).transpose(3, 2, 1, 0, 4)
    x = x.reshape(n_g, rows, 2 * n_s)
    y = pl.pallas_call(
        _s5_out_kernel,
        grid=(n_g,),
        in_specs=[grp(rows, kd), grp(rows, 2 * n_s), grp(kd, kd), grp(2 * n_s, kd)],
        out_specs=grp(rows, kd),
        out_shape=jax.ShapeDtypeStruct((n_g, rows, kd), F32),
        compiler_params=_cparams(("parallel",)),
        name="s5_out",
    )(ug, x, m, cc)
    return (y.reshape(n_g, bsz, n_c, tc, n_p).transpose(1, 2, 3, 0, 4)
            .reshape(bsz * seq, n_g * n_p))


def _compress_kernel(x_ref, pe_ref, w1_ref, w2_ref, o_ref):
    x = x_ref[...].astype(F32)
    nb = x.shape[0]
    a = _dot((x + pe_ref[0:1, :]).astype(BF16), w1_ref[0])
    b = _dot((x + pe_ref[1:2, :]).astype(BF16), w1_ref[1])
    pre = a + pltpu.roll(b, nb - 1, 0)
    o_ref[...] = _dot(jax.nn.gelu(pre).astype(BF16), w2_ref[...]).astype(o_ref.dtype)


def _compress(x, pe, w1, w2):
    bsz, hk, nb, kd = x.shape
    hd = w2.shape[1]
    return pl.pallas_call(
        _compress_kernel,
        grid=(bsz, hk),
        in_specs=[pl.BlockSpec((None, None, nb, kd), lambda b, h: (b, h, 0, 0)),
                  _resident(pe.shape), _resident(w1.shape), _resident(w2.shape)],
        out_specs=pl.BlockSpec((None, None, nb, hd), lambda b, h: (b, h, 0, 0)),
        out_shape=jax.ShapeDtypeStruct((bsz, hk, nb, hd), BF16),
        compiler_params=_cparams(("parallel", "parallel")),
        name="nsa_compress",
    )(x, pe, w1, w2)


def _softmax_step(s, carry, v):
    m, l, acc = carry
    m_new = jnp.maximum(m, jnp.max(s, axis=0, keepdims=True))
    alpha = jnp.exp2(m - m_new)
    p = jnp.exp2(s - m_new)
    l = alpha * l + jnp.sum(p, axis=0, keepdims=True)
    acc = alpha * acc + _dot(v, p.astype(BF16))
    return m_new, l, acc


def _nsa_kernel(q_ref, g_ref, kc_ref, vct_ref, ks_ref, vst_ref, kw_ref, vwt_ref, ov_ref,
                o_ref, sel_ref, *, tq, tk, top_n):
    gq, hd = NSA_GQA, NSA_HEAD_DIM
    n_sb = sel_ref.shape[0]
    nb = kc_ref.shape[0]
    wd = gq * tq
    i = pl.program_id(2)
    s0 = i * tq
    q = q_ref[...]
    qt = jnp.concatenate([q[g * hd:(g + 1) * hd, :] for g in range(gq)], axis=1).astype(BF16)
    t1 = s0 + lax.broadcasted_iota(I32, (1, tq), 1)
    t4 = s0 + (lax.broadcasted_iota(I32, (1, wd), 1) & (tq - 1))

    sc = _dot(kc_ref[...], qt)
    c_last = lax.broadcasted_iota(I32, (nb, 1), 0) * CMP_STRIDE + (CMP_LEN - 1)
    m_c = c_last <= t4
    sm = jnp.where(m_c, sc, NEG)
    e = jnp.exp2(sm - jnp.max(sm, axis=0, keepdims=True))
    p_c = jnp.where(m_c, e / jnp.sum(e, axis=0, keepdims=True), 0.0)
    o_c = _dot(vct_ref[...], p_c.astype(BF16))
    psum = p_c[:, 0:tq]
    for g in range(1, gq):
        psum = psum + p_c[:, g * tq:(g + 1) * tq]
    hi = psum.astype(BF16)
    lo = (psum - hi.astype(F32)).astype(BF16)
    imp = _dot(ov_ref[...], hi) + _dot(ov_ref[...], lo)
    blk = lax.broadcasted_iota(I32, (n_sb, 1), 0)
    cur = t1 >> int(math.log2(SEL_BLOCK))
    forced = (blk == 0) | (blk == cur) | (blk == cur - 1)
    imp = imp + jnp.where(forced, FORCE_BONUS, 0.0)
    imp = jnp.where(blk * SEL_BLOCK <= t1, imp, NEG)

    rowf = lax.broadcasted_iota(I32, (n_sb, tq), 0).astype(F32)
    sel = jnp.zeros((n_sb, tq), F32)
    x = imp
    for _ in range(top_n):
        mx = jnp.max(x, axis=0, keepdims=True)
        first = jnp.min(jnp.where(x == mx, rowf, float(n_sb)), axis=0, keepdims=True)
        hit = rowf == first
        sel = jnp.where(hit, 1.0, sel)
        x = jnp.where(hit, -jnp.inf, x)
    sel_ref[...] = sel

    init = (jnp.full((1, wd), NEG, F32), jnp.zeros((1, wd), F32), jnp.zeros((hd, wd), F32))

    bpt = tk // SEL_BLOCK
    key_row = lax.broadcasted_iota(I32, (tk, 1), 0)

    def sel_tile(j, carry):
        s = _dot(ks_ref[j], qt)
        rows = [jnp.broadcast_to(sel_ref[pl.ds(j * bpt + b, 1), :], (SEL_BLOCK, tq))
                for b in range(bpt)]
        keep = (jnp.concatenate(rows, axis=0) > 0.5) & (j * tk + key_row <= t1)
        bias = jnp.where(keep, 0.0, NEG)
        s = s + jnp.concatenate([bias] * gq, axis=1)
        return _softmax_step(s, carry, vst_ref[j])

    _, l_s, a_s = lax.fori_loop(0, s0 // tk + 1, sel_tile, init)
    o_s = a_s / l_s

    n_wt = (WINDOW + tq) // LANE
    j_lo = jnp.clip(i * (tq // LANE) - WINDOW // LANE, 0, kw_ref.shape[0] - n_wt)
    kwin = jnp.concatenate([kw_ref[j_lo + w] for w in range(n_wt)], axis=0)
    vwin = jnp.concatenate([vwt_ref[j_lo + w] for w in range(n_wt)], axis=1)
    sw = _dot(kwin, qt)
    diff = t4 - (j_lo * LANE + lax.broadcasted_iota(I32, (n_wt * LANE, 1), 0))
    sw = jnp.where((diff >= 0) & (diff < WINDOW), sw, NEG)
    pw = jnp.exp2(sw - jnp.max(sw, axis=0, keepdims=True))
    o_w = _dot(vwin, pw.astype(BF16)) / jnp.sum(pw, axis=0, keepdims=True)

    gt = g_ref[...]
    outs = []
    for g in range(gq):
        cs = slice(g * tq, (g + 1) * tq)
        outs.append(gt[3 * g:3 * g + 1, :] * o_c[:, cs] + gt[3 * g + 1:3 * g + 2, :] * o_s[:, cs]
                    + gt[3 * g + 2:3 * g + 3, :] * o_w[:, cs])
    o_ref[...] = jnp.concatenate(outs, axis=0).T


def _nsa(q, gates, kcmp, vcmp_t, ks, vs_t, kw, vw_t, ov_t, seq):
    bsz = q.shape[0]
    tq = min(TQ_NSA, seq)
    tk = min(TK_NSA, seq)
    n_sb = seq // SEL_BLOCK
    gq, hd = NSA_GQA, NSA_HEAD_DIM
    per_head = lambda a: pl.BlockSpec((None, None) + a.shape[2:],
                                      lambda b, h, i: (b, h) + (0,) * (a.ndim - 2),
                                      pipeline_mode=pl.Buffered(1))
    return pl.pallas_call(
        functools.partial(_nsa_kernel, tq=tq, tk=tk, top_n=min(SEL_TOP, n_sb)),
        grid=(bsz, NSA_KV_HEADS, seq // tq),
        in_specs=[pl.BlockSpec((None, gq * hd, tq), lambda b, h, i: (b, h, i)),
                  pl.BlockSpec((None, None, 3 * gq, tq), lambda b, h, i: (b, h, 0, i)),
                  per_head(kcmp), per_head(vcmp_t), per_head(ks), per_head(vs_t),
                  per_head(kw), per_head(vw_t), _resident(ov_t.shape)],
        out_specs=pl.BlockSpec((tq, gq * hd), lambda b, h, i: (b * (seq // tq) + i, h)),
        out_shape=jax.ShapeDtypeStruct((bsz * seq, NSA_Q), F32),
        scratch_shapes=[pltpu.VMEM((n_sb, tq), F32)],
        compiler_params=_cparams(("parallel", "parallel", "arbitrary")),
        name="nsa",
    )(q, gates, kcmp, vcmp_t, ks, vs_t, kw, vw_t, ov_t)


def _dsa_kernel(h_ref, g_ref, q_ref, qi_ref, wi_ref, kcat_ref, ckt_ref, kidx_ref, wuv_ref, wout_ref,
                o_ref, ihi, ilo, acc_ref, m_ref, l_ref, tie_cap, *, tq, top):
    tk = tq
    n_h = MLA_HEADS
    i16 = jnp.int16
    i = pl.program_id(1)
    row = lax.broadcasted_iota(I32, (tk, 1), 0)
    col = lax.broadcasted_iota(I32, (1, tq), 1)
    causal = row <= col

    def tile(ref, j):
        return ref.at[pl.ds(pl.multiple_of(j * tk, tk), tk), :]

    qi = qi_ref[...]
    wi = wi_ref[...] * (IDX_HEADS ** -0.5) * (IDX_DIM ** -0.5)

    def idx_scores(j):
        kj = kidx_ref[j]
        sc = jnp.zeros((tk, tq), F32)
        for h in range(IDX_HEADS):
            d = _dot(kj, qi[h * IDX_DIM:(h + 1) * IDX_DIM, :])
            sc = sc + wi[h:h + 1, :] * jnp.maximum(d, 0.0)
        return sc

    def store_keys(j, sc):
        b = pltpu.bitcast(sc, I32)
        key = b ^ ((b >> 31) & 0x7FFFFFFF)
        tile(ihi, j)[...] = (key >> 16).astype(i16)
        tile(ilo, j)[...] = ((key & 0xFFFF) - 32768).astype(i16)

    def idx_tile(j, _):
        store_keys(j, idx_scores(j))
        return 0

    lax.fori_loop(0, i, idx_tile, 0)
    store_keys(i, jnp.where(causal, idx_scores(i), NEG))

    def count(ref, pred):
        def body(j, c):
            hit = jnp.where(pred(tile(ref, j)[...]), jnp.ones((), i16), jnp.zeros((), i16))
            parts = [hit[k * 16:(k + 1) * 16] for k in range(tk // 16)]
            while len(parts) > 1:
                parts = [a + b for a, b in zip(parts[0::2], parts[1::2])]
            return c + parts[0]
        c16 = lax.fori_loop(0, i + 1, body, jnp.zeros((16, tq), i16))
        return jnp.sum(c16.astype(I32), axis=0, keepdims=True)

    def search(ref, need):
        def bit_step(b, thr):
            cand = thr + (jnp.int32(1) << (15 - b))
            cand16 = cand.astype(i16)
            return jnp.where(count(ref, lambda t: t >= cand16) >= need, cand, thr)
        return lax.fori_loop(0, 16, bit_step, jnp.full((1, tq), -32768, I32))

    thr_hi = search(ihi, top)
    thr_hi16 = thr_hi.astype(i16)
    need = top - count(ihi, lambda t: t > thr_hi16)

    def mask_lo(j, _):
        lo = tile(ilo, j)
        lo[...] = jnp.where(tile(ihi, j)[...] == thr_hi16, lo[...], jnp.full((), -32768, i16))
        return 0

    lax.fori_loop(0, i + 1, mask_lo, 0)
    thr_lo = search(ilo, need)

    n_eq_hi = count(ihi, lambda t: t == thr_hi16)
    thr_lo16 = thr_lo.astype(i16)
    n_sel = (top - need) + jnp.where(thr_lo == -32768, n_eq_hi, count(ilo, lambda t: t >= thr_lo16))
    tie_cap[...] = jnp.full((1, tq), ihi.shape[0], I32)

    @pl.when(jnp.max(n_sel) > top)
    def _():
        def eq_tile(j):
            return ((tile(ihi, j)[...].astype(I32) == thr_hi)
                    & (tile(ilo, j)[...].astype(I32) == thr_lo))

        def count32(pred):
            def body(j, c):
                return c + jnp.sum(jnp.where(pred(j), 1, 0).reshape(tk // 8, 8, tq), axis=0)
            c8 = lax.fori_loop(0, i + 1, body, jnp.zeros((8, tq), I32))
            return jnp.sum(c8, axis=0, keepdims=True)

        need_eq = top - (n_sel - count32(eq_tile))
        n_bits = (ihi.shape[0] - 1).bit_length()

        def bit_step(b, cap):
            cand = cap + (jnp.int32(1) << (n_bits - 1 - b))
            below = count32(lambda j: eq_tile(j) & (j * tk + row < cand))
            return jnp.where(below >= need_eq, cap, cand)

        tie_cap[...] = lax.fori_loop(0, n_bits, bit_step, jnp.zeros((1, tq), I32))

    m_ref[...] = jnp.full(m_ref.shape, NEG, F32)
    l_ref[...] = jnp.zeros(l_ref.shape, F32)
    acc_ref[...] = jnp.zeros(acc_ref.shape, F32)
    cap = tie_cap[...]

    def att_tile(j, diag):
        hi = tile(ihi, j)[...].astype(I32)
        lo = tile(ilo, j)[...].astype(I32)
        keep = (hi > thr_hi) | ((hi == thr_hi) & ((lo > thr_lo)
                                                  | ((lo == thr_lo) & (j * tk + row <= cap))))
        bias = jnp.where(keep, 0.0, NEG)
        if diag:
            bias = jnp.where(causal, bias, NEG)
        s_all = _dot(kcat_ref[j], q_ref[...])
        ps, alphas = [], []
        for h in range(n_h):
            cs = slice(h * tq, (h + 1) * tq)
            s = s_all[:, cs] + bias
            m_old = m_ref[:, cs]
            m_new = jnp.maximum(m_old, jnp.max(s, axis=0, keepdims=True))
            alpha = jnp.exp2(m_old - m_new)
            p = jnp.exp2(s - m_new)
            l_ref[:, cs] = alpha * l_ref[:, cs] + jnp.sum(p, axis=0, keepdims=True)
            m_ref[:, cs] = m_new
            ps.append(p.astype(BF16))
            alphas.append(alpha)
        p_all = jnp.concatenate(ps, axis=1)
        acc_ref[...] = acc_ref[...] * jnp.concatenate(alphas, axis=1) + _dot(ckt_ref[j], p_all)

    def att_body(j, _):
        att_tile(j, False)
        return 0

    lax.fori_loop(0, i, att_body, 0)
    att_tile(i, True)

    outs = []
    for h in range(n_h):
        cs = slice(h * tq, (h + 1) * tq)
        o = acc_ref[:, cs] / l_ref[:, cs]
        outs.append(_dot(wuv_ref[h], o.astype(BF16)))
    mix = jnp.concatenate(outs, axis=0).T
    y = _dot(mix.astype(BF16), wout_ref[...])
    o_ref[...] = h_ref[...] + _rms_rows(y, g_ref[3:4, :])


def _dsa(h, g, q, qi, wi, kcat, ckt, kidx, wuv_t, w_out, seq):
    bsz = q.shape[0]
    d = h.shape[1]
    tq = min(TQ_DSA, seq)
    nq = seq // tq
    top = min(DSA_TOP, seq // 4)
    wd = MLA_HEADS * tq
    per_b = lambda a: pl.BlockSpec((None,) + a.shape[1:], lambda b, i: (b,) + (0,) * (a.ndim - 1),
                                   pipeline_mode=pl.Buffered(1))
    return pl.pallas_call(
        functools.partial(_dsa_kernel, tq=tq, top=top),
        grid=(bsz, nq),
        in_specs=[pl.BlockSpec((tq, d), lambda b, i: (b * nq + i, 0)), _resident(g.shape),
                  pl.BlockSpec((None, None, MLA_QK, wd), lambda b, i: (b, i, 0, 0)),
                  pl.BlockSpec((None, IDX_HEADS * IDX_DIM, tq), lambda b, i: (b, 0, i)),
                  pl.BlockSpec((None, IDX_HEADS, tq), lambda b, i: (b, 0, i)),
                  per_b(kcat), per_b(ckt), per_b(kidx), _resident(wuv_t.shape),
                  _resident(w_out.shape)],
        out_specs=pl.BlockSpec((tq, d), lambda b, i: (b * nq + i, 0)),
        out_shape=jax.ShapeDtypeStruct(h.shape, F32),
        scratch_shapes=[pltpu.VMEM((seq, tq), jnp.int16),
                        pltpu.VMEM((seq, tq), jnp.int16),
                        pltpu.VMEM((MLA_LATENT, wd), F32),
                        pltpu.VMEM((1, wd), F32),
                        pltpu.VMEM((1, wd), F32),
                        pltpu.VMEM((1, tq), I32)],
        compiler_params=_cparams(("parallel", "arbitrary")),
        name="dsa",
    )(h, g, q, qi, wi, kcat, ckt, kidx, wuv_t, w_out)


def _rope_tables(seq, dim):
    half = dim // 2
    inv = ROPE_THETA ** (-jnp.arange(half, dtype=F32) / half)
    ang = inv[:, None] * jnp.arange(seq, dtype=F32)[None, :]
    return jnp.cos(ang), jnp.sin(ang)


def _even_mixer(h, g, bsz, seq, w_in, w_out, s5, pe_k, pe_v, wk1, wk2, wv1, wv2, tabs):
    cos32, sin32 = tabs[64]
    hd, hk = NSA_HEAD_DIM, NSA_KV_HEADS
    o = 0
    cols = {}
    for name, size in (("u", S5_WIDTH), ("q", NSA_Q), ("kc", NSA_KV), ("vc", NSA_KV), ("ks", NSA_KV),
                       ("vs", NSA_KV), ("kw", NSA_KV), ("vw", NSA_KV), ("gt", 3 * NSA_HEADS)):
        cols[name] = w_in[:, o:o + size]
        o += size
    wu = cols["u"].astype(BF16)
    wt = jnp.concatenate([cols[n] for n in ("q", "kc", "ks", "kw", "vc", "vs", "vw", "gt")], axis=1).T
    wt = jnp.pad(wt, ((0, (-wt.shape[0]) % 16), (0, 0))).astype(BF16)
    u, q_t, k_t, v_t, gt_t = _proj_even(h, g, wu, wt, cos32, sin32, bsz, seq)

    a_re, a_im, b_re, b_im, c_re, c_im, log_dt, d_skip, w_glu = s5
    mats = _s5_matrices(a_re, a_im, b_re, b_im, c_re, c_im, log_dt, S5_CHUNK)
    ys = _s5_scan(u, mats, bsz, seq)

    k5 = k_t.reshape(bsz, 3, hk, hd, seq)
    v5 = v_t.reshape(bsz, 3, hk, hd, seq)
    nb = seq // CMP_STRIDE
    half_blk = lambda a: (a.reshape(bsz, hk, hd, nb, CMP_STRIDE).transpose(0, 1, 3, 4, 2)
                          .reshape(bsz, hk, nb, CMP_STRIDE * hd))
    pe2 = lambda pe: pe.reshape(2, CMP_STRIDE * hd)
    w1s = lambda w: w.reshape(2, CMP_STRIDE * hd, hd).astype(BF16)
    kcmp = _compress(half_blk(k5[:, 0]), pe2(pe_k), w1s(wk1), wk2.astype(BF16))
    vcmp = _compress(half_blk(v5[:, 0]), pe2(pe_v), w1s(wv1), wv2.astype(BF16))
    vcmp_t = vcmp.transpose(0, 1, 3, 2)
    tk = min(TK_NSA, seq)
    ks = k5[:, 1].reshape(bsz, hk, hd, seq // tk, tk).transpose(0, 1, 3, 4, 2)
    vs_t = v5[:, 1].reshape(bsz, hk, hd, seq // tk, tk).transpose(0, 1, 3, 2, 4)
    kw = k5[:, 2].reshape(bsz, hk, hd, seq // LANE, LANE).transpose(0, 1, 3, 4, 2)
    vw_t = v5[:, 2].reshape(bsz, hk, hd, seq // LANE, LANE).transpose(0, 1, 3, 2, 4)
    n_sb = seq // SEL_BLOCK
    c_start = jnp.arange(nb) * CMP_STRIDE
    b_start = jnp.arange(n_sb) * SEL_BLOCK
    ov_t = ((c_start[None, :] < b_start[:, None] + SEL_BLOCK)
            & (c_start[None, :] + CMP_LEN > b_start[:, None])
            & (jnp.arange(nb)[None, :] < nb - 1)).astype(BF16)
    gates = gt_t.reshape(bsz, hk, 3 * NSA_GQA, seq)
    b_out = _nsa(q_t, gates, kcmp, vcmp_t, ks, vs_t, kw, vw_t, ov_t, seq)
    return _outproj_even(h, ys, u, b_out, g, d_skip.reshape(1, -1), w_glu.astype(BF16),
                         w_out[:S5_WIDTH].astype(BF16), w_out[S5_WIDTH:].astype(BF16))


def _odd_mixer(h, g, bsz, seq, w_in, kv_norm, w_uv, w_out, tabs):
    cos32, sin32 = tabs[64]
    cos16, sin16 = tabs[32]
    d = h.shape[1]
    sizes = (MLA_HEADS * MLA_LATENT, MLA_HEADS * MLA_ROPE, MLA_LATENT, MLA_ROPE,
             IDX_HEADS * IDX_DIM, IDX_DIM, IDX_HEADS)
    parts = []
    o = 0
    for s in sizes:
        parts.append(w_in[:, o:o + s])
        o += s
    w_ql, w_qr, w_c, w_kr, w_qi, w_ki, w_wi = parts
    wq = jnp.concatenate([w_ql.reshape(d, MLA_HEADS, MLA_LATENT), w_qr.reshape(d, MLA_HEADS, MLA_ROPE)],
                         axis=2).reshape(d, MLA_HEADS * MLA_QK).T.astype(BF16)
    wkv = jnp.concatenate([w_c, w_kr, w_qi, w_ki, w_wi], axis=1).T
    wkv = jnp.pad(wkv, ((0, (-wkv.shape[0]) % 16), (0, 0))).astype(BF16)
    ckv_t, kr_t, qi_t, ki_t, wi_t = _proj_odd_kv(h, g, wkv, kv_norm.reshape(-1, 1), cos32, sin32,
                                                 cos16, sin16, bsz, seq)
    q = _proj_odd_q(h, g, wq, cos16, sin16, bsz, seq)
    tk = min(TQ_DSA, seq)
    nk = seq // tk
    kcat = jnp.concatenate([ckv_t, kr_t], axis=1).transpose(0, 2, 1).reshape(bsz, nk, tk, MLA_QK)
    ckt = ckv_t.reshape(bsz, MLA_LATENT, nk, tk).transpose(0, 2, 1, 3)
    kidx = ki_t.transpose(0, 2, 1).reshape(bsz, nk, tk, IDX_DIM)
    wuv_t = w_uv.transpose(0, 2, 1).astype(BF16)
    return _dsa(h, g, q, qi_t, wi_t, kcat, ckt, kidx, wuv_t, w_out.astype(BF16), seq)


def kernel(x, p, norm_g, ffn1_w_in, ffn1_w_out, ffn2_w_in, ffn2_w_out, ple_w_gate, ple_w_proj,
           ev_w_in, ev_w_out, s5_a_re, s5_a_im, s5_b_re, s5_b_im, s5_c_re, s5_c_im, s5_log_dt, s5_d,
           s5_w_glu, nsa_pe_k, nsa_pe_v, nsa_wk1, nsa_wk2, nsa_wv1, nsa_wv2,
           od_w_in, od_kv_norm, od_w_uv, od_w_out):
    bsz, seq, d = x.shape
    depth = norm_g.shape[0]
    tabs = {64: _rope_tables(seq, 64), 32: _rope_tables(seq, 32)}
    h = x.reshape(bsz * seq, d)
    for i in range(depth):
        g = norm_g[i]
        j = i // 2
        h = _ffn(h, g, ffn1_w_in[i].astype(BF16), ffn1_w_out[i].astype(BF16), 0)
        if i % 2 == 0:
            s5 = (s5_a_re[j], s5_a_im[j], s5_b_re[j], s5_b_im[j], s5_c_re[j], s5_c_im[j],
                  s5_log_dt[j], s5_d[j], s5_w_glu[j])
            h = _even_mixer(h, g, bsz, seq, ev_w_in[j], ev_w_out[j], s5, nsa_pe_k[j], nsa_pe_v[j],
                            nsa_wk1[j], nsa_wk2[j], nsa_wv1[j], nsa_wv2[j], tabs)
        else:
            h = _odd_mixer(h, g, bsz, seq, od_w_in[j], od_kv_norm[j], od_w_uv[j], od_w_out[j], tabs)
        h = _ffn_ple(h, p[i].reshape(bsz * seq, -1), g, ffn2_w_in[i].astype(BF16),
                     ffn2_w_out[i].astype(BF16), ple_w_gate[i].astype(BF16),
                     ple_w_proj[i].astype(BF16))
    return h.reshape(bsz, seq, d)
```

```python
import functools
import math

import jax
import jax.numpy as jnp
from jax import lax
from jax.experimental import pallas as pl
from jax.experimental.pallas import tpu as pltpu

F32 = jnp.float32
BF16 = jnp.bfloat16
I32 = jnp.int32

ROPE_THETA = 10000.0
EPS = 1e-6
NEG = -1e30
LOG2E = math.log2(math.e)
D_FF = 2816
S5_WIDTH = 512
S5_GROUP = 16
S5_GROUPS = S5_WIDTH // S5_GROUP
S5_STATE = 64
NSA_HEADS = 8
NSA_KV_HEADS = 2
NSA_GQA = NSA_HEADS // NSA_KV_HEADS
NSA_HEAD_DIM = 64
CMP_LEN = 32
CMP_STRIDE = 16
SEL_BLOCK = 64
SEL_TOP = 16
WINDOW = 512
FORCE_BONUS = 1000.0
NSA_Q = NSA_HEADS * NSA_HEAD_DIM
NSA_KV = NSA_KV_HEADS * NSA_HEAD_DIM
MLA_HEADS = 16
MLA_LATENT = 256
MLA_ROPE = 32
MLA_QK = MLA_LATENT + MLA_ROPE
MLA_V_DIM = 64
IDX_HEADS = 8
IDX_DIM = 64
DSA_TOP = 256

V7X_VMEM_BYTES = 64 * 2**20
VMEM_LIMIT = V7X_VMEM_BYTES - 8 * 2**20
LANE = 128
TM_FFN = 512
TQ_PROJ = 256
TQ_NSA = 128
TK_NSA = 512
TQ_DSA = 256
S5_CHUNK = 64


def _cparams(sem):
    return pltpu.CompilerParams(dimension_semantics=sem, vmem_limit_bytes=VMEM_LIMIT)


def _resident(shape):
    nd = len(shape)
    return pl.BlockSpec(shape, lambda *_: (0,) * nd, pipeline_mode=pl.Buffered(1))


def _dot(a, b):
    return jnp.dot(a, b, preferred_element_type=F32)


def _dot_nt(a, b):
    return lax.dot_general(a, b, (((1,), (1,)), ((), ())), preferred_element_type=F32)


def _rms_rows(x, g):
    return x * lax.rsqrt(jnp.mean(x * x, axis=-1, keepdims=True) + EPS) * g


def _rope_fmaj(y, cos, sin):
    half = y.shape[1] // 2
    t1 = y[:, :half, :]
    t2 = y[:, half:, :]
    return jnp.concatenate([t1 * cos - t2 * sin, t2 * cos + t1 * sin], axis=1)


def _ffn_tile(x, g_ref, g0, win_ref, wout_ref, n_chunk):
    xn = _rms_rows(x, g_ref[g0:g0 + 1, :]).astype(BF16)
    ck = D_FF // n_chunk
    acc = None
    for c in range(n_chunk):
        a = _dot(xn, win_ref[:, c * ck:(c + 1) * ck])
        u = _dot(xn, win_ref[:, D_FF + c * ck:D_FF + (c + 1) * ck])
        act = (jax.nn.silu(a) * u).astype(BF16)
        y = _dot(act, wout_ref[c * ck:(c + 1) * ck, :])
        acc = y if acc is None else acc + y
    return x + 0.5 * _rms_rows(acc, g_ref[g0 + 1:g0 + 2, :])


def _ffn_kernel(h_ref, g_ref, win_ref, wout_ref, o_ref, *, g0, n_chunk):
    o_ref[...] = _ffn_tile(h_ref[...], g_ref, g0, win_ref, wout_ref, n_chunk)


def _ffn_ple_kernel(h_ref, p_ref, g_ref, win_ref, wout_ref, wg_ref, wp_ref, o_ref, *, n_chunk):
    x = _ffn_tile(h_ref[...], g_ref, 4, win_ref, wout_ref, n_chunk)
    gate = jax.nn.sigmoid(_dot(_rms_rows(x, g_ref[6:7, :]).astype(BF16), wg_ref[...]))
    e = _dot(p_ref[...].astype(BF16), wp_ref[...]) * gate
    o_ref[...] = x + _rms_rows(e, g_ref[7:8, :])


def _ffn_ple(h, p, g, w_in, w_out, w_gate, w_proj):
    t, d = h.shape
    tm = min(TM_FFN, t)
    return pl.pallas_call(
        functools.partial(_ffn_ple_kernel, n_chunk=2),
        grid=(t // tm,),
        in_specs=[pl.BlockSpec((tm, d), lambda i: (i, 0)),
                  pl.BlockSpec((tm, p.shape[1]), lambda i: (i, 0)), _resident(g.shape),
                  _resident(w_in.shape), _resident(w_out.shape), _resident(w_gate.shape),
                  _resident(w_proj.shape)],
        out_specs=pl.BlockSpec((tm, d), lambda i: (i, 0)),
        out_shape=jax.ShapeDtypeStruct((t, d), F32),
        compiler_params=_cparams(("parallel",)),
        name="ffn_ple",
    )(h, p, g, w_in, w_out, w_gate, w_proj)


def _ffn(h, g, w_in, w_out, g0):
    t, d = h.shape
    tm = min(TM_FFN, t)
    return pl.pallas_call(
        functools.partial(_ffn_kernel, g0=g0, n_chunk=2),
        grid=(t // tm,),
        in_specs=[pl.BlockSpec((tm, d), lambda i: (i, 0)), _resident(g.shape),
                  _resident(w_in.shape), _resident(w_out.shape)],
        out_specs=pl.BlockSpec((tm, d), lambda i: (i, 0)),
        out_shape=jax.ShapeDtypeStruct((t, d), F32),
        compiler_params=_cparams(("parallel",)),
        name="ffn",
    )(h, g, w_in, w_out)


def _outproj_even_kernel(h_ref, ys_ref, u_ref, b_ref, g_ref, d_ref, wglu_ref, wa_ref, wb_ref, o_ref):
    y = ys_ref[...] + d_ref[...] * u_ref[...]
    z = jax.nn.gelu(y)
    a = z * jax.nn.sigmoid(_dot(z.astype(BF16), wglu_ref[...]))
    mix = _dot(a.astype(BF16), wa_ref[...]) + _dot(b_ref[...].astype(BF16), wb_ref[...])
    o_ref[...] = h_ref[...] + _rms_rows(mix, g_ref[3:4, :])


def _outproj_even(h, ys, u, b_out, g, d_skip, w_glu, w_a, w_b):
    t, d = h.shape
    tm = min(TM_FFN, t)
    tok = lambda w: pl.BlockSpec((tm, w), lambda i: (i, 0))
    return pl.pallas_call(
        _outproj_even_kernel,
        grid=(t // tm,),
        in_specs=[tok(d), tok(S5_WIDTH), tok(S5_WIDTH), tok(NSA_Q), _resident(g.shape),
                  _resident(d_skip.shape), _resident(w_glu.shape), _resident(w_a.shape),
                  _resident(w_b.shape)],
        out_specs=tok(d),
        out_shape=jax.ShapeDtypeStruct((t, d), F32),
        compiler_params=_cparams(("parallel",)),
        name="outproj_even",
    )(h, ys, u, b_out, g, d_skip, w_glu, w_a, w_b)


def _proj_even_kernel(h_ref, g_ref, wu_ref, wt_ref, cos_ref, sin_ref,
                      u_ref, q_ref, k_ref, v_ref, gt_ref):
    xn = _rms_rows(h_ref[...], g_ref[2:3, :]).astype(BF16)
    u_ref[...] = _dot(xn, wu_ref[...])
    y = _dot_nt(wt_ref[...], xn)
    tq = y.shape[1]
    cos = cos_ref[...]
    sin = sin_ref[...]
    hd = NSA_HEAD_DIM
    q = _rope_fmaj(y[0:NSA_Q].reshape(NSA_HEADS, hd, tq), cos, sin)
    q_ref[...] = (q * (hd ** -0.5 * LOG2E)).reshape(NSA_Q, tq)
    k0 = NSA_Q
    nk = 3 * NSA_KV
    k = _rope_fmaj(y[k0:k0 + nk].reshape(3 * NSA_KV_HEADS, hd, tq), cos, sin)
    k_ref[...] = k.reshape(nk, tq).astype(k_ref.dtype)
    v0 = k0 + nk
    v_ref[...] = y[v0:v0 + nk].astype(v_ref.dtype)
    g0 = v0 + nk
    gt_ref[...] = jax.nn.sigmoid(y[g0:g0 + 3 * NSA_HEADS])


def _proj_even(h, g, wu, wt, cos, sin, bsz, seq):
    d = h.shape[1]
    tq = min(TQ_PROJ, seq)
    nq = seq // tq
    nk = 3 * NSA_KV
    fm = lambda rows: pl.BlockSpec((None, rows, tq), lambda b, i: (b, 0, i))
    return pl.pallas_call(
        _proj_even_kernel,
        grid=(bsz, nq),
        in_specs=[pl.BlockSpec((tq, d), lambda b, i: (b * nq + i, 0)), _resident(g.shape),
                  _resident(wu.shape), _resident(wt.shape),
                  pl.BlockSpec((cos.shape[0], tq), lambda b, i: (0, i)),
                  pl.BlockSpec((sin.shape[0], tq), lambda b, i: (0, i))],
        out_specs=[pl.BlockSpec((tq, S5_WIDTH), lambda b, i: (b * nq + i, 0)),
                   fm(NSA_Q), fm(nk), fm(nk), fm(3 * NSA_HEADS)],
        out_shape=[jax.ShapeDtypeStruct((bsz * seq, S5_WIDTH), F32),
                   jax.ShapeDtypeStruct((bsz, NSA_Q, seq), F32),
                   jax.ShapeDtypeStruct((bsz, nk, seq), BF16),
                   jax.ShapeDtypeStruct((bsz, nk, seq), BF16),
                   jax.ShapeDtypeStruct((bsz, 3 * NSA_HEADS, seq), F32)],
        compiler_params=_cparams(("parallel", "parallel")),
        name="proj_even",
    )(h, g, wu, wt, cos, sin)


def _proj_odd_kv_kernel(h_ref, g_ref, wt_ref, kvn_ref, cos32_ref, sin32_ref, cos16_ref, sin16_ref,
                        ckv_ref, kr_ref, qi_ref, ki_ref, wi_ref):
    xn = _rms_rows(h_ref[...], g_ref[2:3, :]).astype(BF16)
    y = _dot_nt(wt_ref[...], xn)
    tq = y.shape[1]
    c = y[0:MLA_LATENT]
    c = c * lax.rsqrt(jnp.mean(c * c, axis=0, keepdims=True) + EPS) * kvn_ref[...]
    ckv_ref[...] = c.astype(ckv_ref.dtype)
    r0 = MLA_LATENT
    kr = _rope_fmaj(y[r0:r0 + MLA_ROPE].reshape(1, MLA_ROPE, tq), cos16_ref[...], sin16_ref[...])
    kr_ref[...] = kr.reshape(MLA_ROPE, tq).astype(kr_ref.dtype)
    q0 = r0 + MLA_ROPE
    nqi = IDX_HEADS * IDX_DIM
    qi = _rope_fmaj(y[q0:q0 + nqi].reshape(IDX_HEADS, IDX_DIM, tq), cos32_ref[...], sin32_ref[...])
    qi_ref[...] = qi.reshape(nqi, tq).astype(qi_ref.dtype)
    k0 = q0 + nqi
    ki = _rope_fmaj(y[k0:k0 + IDX_DIM].reshape(1, IDX_DIM, tq), cos32_ref[...], sin32_ref[...])
    ki_ref[...] = ki.reshape(IDX_DIM, tq).astype(ki_ref.dtype)
    w0 = k0 + IDX_DIM
    wi_ref[...] = y[w0:w0 + IDX_HEADS]


def _proj_odd_kv(h, g, wt, kvn, cos32, sin32, cos16, sin16, bsz, seq):
    d = h.shape[1]
    tq = min(TQ_PROJ, seq)
    nq = seq // tq
    fm = lambda rows: pl.BlockSpec((None, rows, tq), lambda b, i: (b, 0, i))
    tab = lambda a: pl.BlockSpec((a.shape[0], tq), lambda b, i: (0, i))
    nqi = IDX_HEADS * IDX_DIM
    return pl.pallas_call(
        _proj_odd_kv_kernel,
        grid=(bsz, nq),
        in_specs=[pl.BlockSpec((tq, d), lambda b, i: (b * nq + i, 0)), _resident(g.shape),
                  _resident(wt.shape), _resident(kvn.shape),
                  tab(cos32), tab(sin32), tab(cos16), tab(sin16)],
        out_specs=[fm(MLA_LATENT), fm(MLA_ROPE), fm(nqi), fm(IDX_DIM), fm(IDX_HEADS)],
        out_shape=[jax.ShapeDtypeStruct((bsz, MLA_LATENT, seq), BF16),
                   jax.ShapeDtypeStruct((bsz, MLA_ROPE, seq), BF16),
                   jax.ShapeDtypeStruct((bsz, nqi, seq), BF16),
                   jax.ShapeDtypeStruct((bsz, IDX_DIM, seq), BF16),
                   jax.ShapeDtypeStruct((bsz, IDX_HEADS, seq), F32)],
        compiler_params=_cparams(("parallel", "parallel")),
        name="proj_odd_kv",
    )(h, g, wt, kvn, cos32, sin32, cos16, sin16)


def _proj_odd_q_kernel(h_ref, g_ref, wt_ref, cos16_ref, sin16_ref, q_ref):
    xn = _rms_rows(h_ref[...], g_ref[2:3, :]).astype(BF16)
    tq = xn.shape[0]
    cos = cos16_ref[...]
    sin = sin16_ref[...]
    for hh in range(MLA_HEADS):
        y = _dot_nt(wt_ref[hh * MLA_QK:(hh + 1) * MLA_QK, :], xn)
        r = _rope_fmaj(y[MLA_LATENT:].reshape(1, MLA_ROPE, tq), cos, sin)
        q = jnp.concatenate([y[:MLA_LATENT], r.reshape(MLA_ROPE, tq)], axis=0)
        q_ref[:, hh * tq:(hh + 1) * tq] = (q * (MLA_QK ** -0.5 * LOG2E)).astype(q_ref.dtype)


def _proj_odd_q(h, g, wt, cos16, sin16, bsz, seq):
    d = h.shape[1]
    tq = min(TQ_DSA, seq)
    nq = seq // tq
    return pl.pallas_call(
        _proj_odd_q_kernel,
        grid=(bsz, nq),
        in_specs=[pl.BlockSpec((tq, d), lambda b, i: (b * nq + i, 0)), _resident(g.shape),
                  _resident(wt.shape),
                  pl.BlockSpec((cos16.shape[0], tq), lambda b, i: (0, i)),
                  pl.BlockSpec((sin16.shape[0], tq), lambda b, i: (0, i))],
        out_specs=pl.BlockSpec((None, None, MLA_QK, MLA_HEADS * tq), lambda b, i: (b, i, 0, 0)),
        out_shape=jax.ShapeDtypeStruct((bsz, nq, MLA_QK, MLA_HEADS * tq), BF16),
        compiler_params=_cparams(("parallel", "parallel")),
        name="proj_odd_q",
    )(h, g, wt, cos16, sin16)


def _s5_state_kernel(u_ref, bc_ref, s_ref):
    s_ref[...] = _dot(u_ref[...], bc_ref[...])


def _s5_scan_kernel(sr_ref, si_ref, lr_ref, li_ref, xr_ref, xi_ref):
    n_chunk = sr_ref.shape[0]
    lr = lr_ref[...]
    li = li_ref[...]

    def body(c, carry):
        xr, xi = carry
        xr_ref[c] = xr
        xi_ref[c] = xi
        return (lr * xr - li * xi + sr_ref[c], lr * xi + li * xr + si_ref[c])

    zero = jnp.zeros(sr_ref.shape[1:], F32)
    lax.fori_loop(0, n_chunk, body, (zero, zero))


def _s5_out_kernel(u_ref, x_ref, m_ref, cc_ref, y_ref):
    x = x_ref[...]
    hi = x.astype(BF16)
    lo = (x - hi.astype(F32)).astype(BF16)
    cc = cc_ref[...]
    y_ref[...] = _dot(u_ref[...], m_ref[...]) + _dot(hi, cc) + _dot(lo, cc)


def _s5_matrices(a_re, a_im, b_re, b_im, c_re, c_im, log_dt, tc):
    hp = lax.Precision.HIGHEST
    dt = jnp.exp(log_dt)[:, None]
    lam = lax.complex(a_re, a_im)
    lam_dt = lam * dt
    lam_bar = jnp.exp(lam_dt)
    b_bar = ((lam_bar - 1.0) / lam)[..., None] * lax.complex(b_re, b_im)
    k = jnp.arange(tc + 1, dtype=F32)[:, None, None]
    pw = jnp.exp(lam_dt[None] * k)
    pr, pi = jnp.real(pw), jnp.imag(pw)
    bbr, bbi = jnp.real(b_bar), jnp.imag(b_bar)
    cpr = c_re[None] * pr[:, :, None, :] - c_im[None] * pi[:, :, None, :]
    cpi = c_re[None] * pi[:, :, None, :] + c_im[None] * pr[:, :, None, :]
    kk = (jnp.einsum('kgpn,gnq->kgpq', cpr[:tc], bbr, precision=hp)
          - jnp.einsum('kgpn,gnq->kgpq', cpi[:tc], bbi, precision=hp))
    n_g, n_p = a_re.shape[0], b_re.shape[2]
    kq = jnp.pad(kk.transpose(1, 3, 0, 2).astype(BF16), ((0, 0), (0, 0), (tc, 0), (0, 0)))
    lag = tc + jnp.arange(tc)[None, :] - jnp.arange(tc)[:, None]
    m = kq[:, :, lag, :].reshape(n_g, n_p * tc, tc * n_p)
    rev = pw[tc - 1 - jnp.arange(tc)]
    bc = rev[:, :, :, None] * b_bar[None]
    bc = bc.transpose(1, 3, 0, 2).reshape(n_g, n_p * tc, -1)
    bc = jnp.concatenate([jnp.real(bc), jnp.imag(bc)], axis=-1)
    mr = cpr[1:tc + 1].transpose(1, 3, 0, 2).reshape(n_g, -1, tc * n_p)
    mi = cpi[1:tc + 1].transpose(1, 3, 0, 2).reshape(n_g, -1, tc * n_p)
    cc = jnp.concatenate([mr, -mi], axis=1)
    ltc = pw[tc].reshape(1, -1)
    return m, bc.astype(BF16), cc.astype(BF16), jnp.real(ltc), jnp.imag(ltc)


def _s5_scan(u, mats, bsz, seq):
    m, bc, cc, lr, li = mats
    tc = S5_CHUNK
    n_c = seq // tc
    n_g, n_p, n_s = S5_GROUPS, S5_GROUP, S5_STATE
    rows = bsz * n_c
    kd = tc * n_p
    ug = (u.reshape(bsz, n_c, tc, n_g, n_p).transpose(3, 0, 1, 4, 2)
          .reshape(n_g, rows, kd).astype(BF16))
    grp = lambda a, b: pl.BlockSpec((None, a, b), lambda gi: (gi, 0, 0))
    s = pl.pallas_call(
        _s5_state_kernel,
        grid=(n_g,),
        in_specs=[grp(rows, kd), grp(kd, 2 * n_s)],
        out_specs=grp(rows, 2 * n_s),
        out_shape=jax.ShapeDtypeStruct((n_g, rows, 2 * n_s), F32),
        compiler_params=_cparams(("parallel",)),
        name="s5_state",
    )(ug, bc)
    s5 = s.reshape(n_g, bsz, n_c, 2, n_s).transpose(3, 2, 1, 0, 4).reshape(2, n_c, bsz, n_g * n_s)
    full = lambda shp: pl.BlockSpec(shp, lambda: (0,) * len(shp))
    xr, xi = pl.pallas_call(
        _s5_scan_kernel,
        in_specs=[full(s5.shape[1:]), full(s5.shape[1:]), full(lr.shape), full(li.shape)],
        out_specs=[full(s5.shape[1:]), full(s5.shape[1:])],
        out_shape=[jax.ShapeDtypeStruct(s5.shape[1:], F32)] * 2,
        compiler_params=pltpu.CompilerParams(vmem_limit_bytes=VMEM_LIMIT),
        name="s5_scan",
    )(s5[0], s5[1], lr, li)
    x = jnp.stack([xr, xi]).reshape(2, n_c, bsz, n_g, n_s).transpose(3, 2, 1, 0, 4)
    x = x.reshape(n_g, rows, 2 * n_s)
    y = pl.pallas_call(
        _s5_out_kernel,
        grid=(n_g,),
        in_specs=[grp(rows, kd), grp(rows, 2 * n_s), grp(kd, kd), grp(2 * n_s, kd)],
        out_specs=grp(rows, kd),
        out_shape=jax.ShapeDtypeStruct((n_g, rows, kd), F32),
        compiler_params=_cparams(("parallel",)),
        name="s5_out",
    )(ug, x, m, cc)
    return (y.reshape(n_g, bsz, n_c, tc, n_p).transpose(1, 2, 3, 0, 4)
            .reshape(bsz * seq, n_g * n_p))


def _compress_kernel(x_ref, pe_ref, w1_ref, w2_ref, o_ref):
    x = x_ref[...].astype(F32)
    nb = x.shape[0]
    a = _dot((x + pe_ref[0:1, :]).astype(BF16), w1_ref[0])
    b = _dot((x + pe_ref[1:2, :]).astype(BF16), w1_ref[1])
    pre = a + pltpu.roll(b, nb - 1, 0)
    o_ref[...] = _dot(jax.nn.gelu(pre).astype(BF16), w2_ref[...]).astype(o_ref.dtype)


def _compress(x, pe, w1, w2):
    bsz, hk, nb, kd = x.shape
    hd = w2.shape[1]
    return pl.pallas_call(
        _compress_kernel,
        grid=(bsz, hk),
        in_specs=[pl.BlockSpec((None, None, nb, kd), lambda b, h: (b, h, 0, 0)),
                  _resident(pe.shape), _resident(w1.shape), _resident(w2.shape)],
        out_specs=pl.BlockSpec((None, None, nb, hd), lambda b, h: (b, h, 0, 0)),
        out_shape=jax.ShapeDtypeStruct((bsz, hk, nb, hd), BF16),
        compiler_params=_cparams(("parallel", "parallel")),
        name="nsa_compress",
    )(x, pe, w1, w2)


def _nsa_kernel(q_ref, g_ref, kc_ref, vct_ref, ks_ref, vst_ref, kw_ref, vwt_ref, ov_ref,
                o_ref, sel_ref, zero_ref, sa_ref, sb_ref, *, tq, tk, top_n):
    gq, hd = NSA_GQA, NSA_HEAD_DIM
    n_sb = sel_ref.shape[0]
    nb = kc_ref.shape[0]
    wd = gq * tq
    i = pl.program_id(2)
    s0 = i * tq
    q = q_ref[...]
    qt = jnp.concatenate([q[g * hd:(g + 1) * hd, :] for g in range(gq)], axis=1).astype(BF16)
    t1 = s0 + lax.broadcasted_iota(I32, (1, tq), 1)
    t4 = s0 + (lax.broadcasted_iota(I32, (1, wd), 1) & (tq - 1))

    sc = _dot(kc_ref[...], qt)
    c_last = lax.broadcasted_iota(I32, (nb, 1), 0) * CMP_STRIDE + (CMP_LEN - 1)
    bias_c = jnp.where(c_last <= t1, 0.0, NEG)
    sm = sc + jnp.concatenate([bias_c] * gq, axis=1)
    e = jnp.exp2(sm - jnp.max(sm, axis=0, keepdims=True))
    p_c = e * jnp.where(t4 >= CMP_LEN - 1, 1.0 / jnp.sum(e, axis=0, keepdims=True), 0.0)
    o_c = _dot(vct_ref[...], p_c.astype(BF16))
    psum = p_c[:, 0:tq]
    for g in range(1, gq):
        psum = psum + p_c[:, g * tq:(g + 1) * tq]
    hi = psum.astype(BF16)
    lo = (psum - hi.astype(F32)).astype(BF16)
    imp = _dot(ov_ref[...], hi) + _dot(ov_ref[...], lo)
    blk = lax.broadcasted_iota(I32, (n_sb, 1), 0)
    cur = t1 >> int(math.log2(SEL_BLOCK))
    forced = (blk == 0) | (blk == cur) | (blk == cur - 1)
    imp = imp + jnp.where(forced, FORCE_BONUS, 0.0)
    imp = jnp.where(blk * SEL_BLOCK <= t1, imp, NEG)

    rowf = lax.broadcasted_iota(I32, (n_sb, tq), 0).astype(F32)
    sel = jnp.zeros((n_sb, tq), F32)
    x = imp
    for _ in range(top_n):
        mx = jnp.max(x, axis=0, keepdims=True)
        first = jnp.min(jnp.where(x == mx, rowf, float(n_sb)), axis=0, keepdims=True)
        hit = rowf == first
        sel = jnp.where(hit, 1.0, sel)
        x = jnp.where(hit, -jnp.inf, x)
    sel_ref[...] = sel

    zero_ref[...] = jnp.zeros(zero_ref.shape, F32)
    init = (jnp.full((1, wd), NEG, F32), jnp.zeros((1, wd), F32), zero_ref[...])

    bpt = tk // SEL_BLOCK
    key_row = lax.broadcasted_iota(I32, (tk, 1), 0)

    last_tile = ks_ref.shape[0] - 1

    def scores(j):
        return _dot(ks_ref[jnp.minimum(j, last_tile)], qt)

    def consume(j, s_ref, carry):
        m, l, acc, pv = carry
        jc = jnp.minimum(j, last_tile)
        rows = [jnp.broadcast_to(sel_ref[pl.ds(jc * bpt + b, 1), :], (SEL_BLOCK, tq))
                for b in range(bpt)]
        keep = (jnp.concatenate(rows, axis=0) > 0.5) & (j * tk + key_row <= t1)
        bias = jnp.where(keep, 0.0, NEG)
        s = s_ref[...] + jnp.concatenate([bias] * gq, axis=1)
        m_new = jnp.maximum(m, jnp.max(s, axis=0, keepdims=True))
        alpha = jnp.exp2(m - m_new)
        p = jnp.exp2(s - m_new)
        l = alpha * l + jnp.sum(p, axis=0, keepdims=True)
        return m_new, l, alpha * (acc + pv), _dot(vst_ref[jc], p.astype(BF16))

    def tile_pair(k, carry):
        j = 2 * k
        sb_ref[...] = scores(j + 1)
        carry = consume(j, sa_ref, carry)
        sa_ref[...] = scores(j + 2)
        return consume(j + 1, sb_ref, carry)

    sa_ref[...] = scores(0)
    _, l_s, a_s, pv_s = lax.fori_loop(0, (s0 // tk + 2) // 2, tile_pair, init + (zero_ref[...],))
    o_s = (a_s + pv_s) / l_s

    n_wt = (WINDOW + tq) // LANE
    j_lo = jnp.clip(i * (tq // LANE) - WINDOW // LANE, 0, kw_ref.shape[0] - n_wt)
    kwin = jnp.concatenate([kw_ref[j_lo + w] for w in range(n_wt)], axis=0)
    vwin = jnp.concatenate([vwt_ref[j_lo + w] for w in range(n_wt)], axis=1)
    sw = _dot(kwin, qt)
    diff = t1 - (j_lo * LANE + lax.broadcasted_iota(I32, (n_wt * LANE, 1), 0))
    sw = sw + jnp.concatenate([jnp.where((diff >= 0) & (diff < WINDOW), 0.0, NEG)] * gq, axis=1)
    pw = jnp.exp2(sw - jnp.max(sw, axis=0, keepdims=True))
    o_w = _dot(vwin, pw.astype(BF16)) / jnp.sum(pw, axis=0, keepdims=True)

    gt = g_ref[...]
    outs = []
    for g in range(gq):
        cs = slice(g * tq, (g + 1) * tq)
        outs.append(gt[3 * g:3 * g + 1, :] * o_c[:, cs] + gt[3 * g + 1:3 * g + 2, :] * o_s[:, cs]
                    + gt[3 * g + 2:3 * g + 3, :] * o_w[:, cs])
    o_ref[...] = jnp.concatenate(outs, axis=0).T


def _nsa(q, gates, kcmp, vcmp_t, ks, vs_t, kw, vw_t, ov_t, seq):
    bsz = q.shape[0]
    tq = min(TQ_NSA, seq)
    tk = min(TK_NSA, seq)
    n_sb = seq // SEL_BLOCK
    gq, hd = NSA_GQA, NSA_HEAD_DIM
    per_head = lambda a: pl.BlockSpec((None, None) + a.shape[2:],
                                      lambda b, h, i: (b, h) + (0,) * (a.ndim - 2),
                                      pipeline_mode=pl.Buffered(1))
    return pl.pallas_call(
        functools.partial(_nsa_kernel, tq=tq, tk=tk, top_n=min(SEL_TOP, n_sb)),
        grid=(bsz, NSA_KV_HEADS, seq // tq),
        in_specs=[pl.BlockSpec((None, gq * hd, tq), lambda b, h, i: (b, h, i)),
                  pl.BlockSpec((None, None, 3 * gq, tq), lambda b, h, i: (b, h, 0, i)),
                  per_head(kcmp), per_head(vcmp_t), per_head(ks), per_head(vs_t),
                  per_head(kw), per_head(vw_t), _resident(ov_t.shape)],
        out_specs=pl.BlockSpec((tq, gq * hd), lambda b, h, i: (b * (seq // tq) + i, h)),
        out_shape=jax.ShapeDtypeStruct((bsz * seq, NSA_Q), F32),
        scratch_shapes=[pltpu.VMEM((n_sb, tq), F32), pltpu.VMEM((hd, gq * tq), F32),
                        pltpu.VMEM((tk, gq * tq), F32), pltpu.VMEM((tk, gq * tq), F32)],
        compiler_params=_cparams(("parallel", "parallel", "arbitrary")),
        name="nsa",
    )(q, gates, kcmp, vcmp_t, ks, vs_t, kw, vw_t, ov_t)


def _dsa_kernel(h_ref, g_ref, q_ref, qi_ref, wi_ref, kcat_ref, ckt_ref, kidx_ref, wuv_ref, wout_ref,
                o_ref, ihi, ilo, acc_ref, m_ref, l_ref, tie_cap, *, tq, top):
    tk = tq
    n_h = MLA_HEADS
    i16 = jnp.int16
    i = pl.program_id(1)
    row = lax.broadcasted_iota(I32, (tk, 1), 0)
    col = lax.broadcasted_iota(I32, (1, tq), 1)
    causal = row <= col

    def tile(ref, j):
        return ref.at[pl.ds(pl.multiple_of(j * tk, tk), tk), :]

    qi = qi_ref[...]
    wi = wi_ref[...] * (IDX_HEADS ** -0.5) * (IDX_DIM ** -0.5)

    def idx_scores(j):
        kj = kidx_ref[j]
        sc = jnp.zeros((tk, tq), F32)
        for h in range(IDX_HEADS):
            d = _dot(kj, qi[h * IDX_DIM:(h + 1) * IDX_DIM, :])
            sc = sc + wi[h:h + 1, :] * jnp.maximum(d, 0.0)
        return sc

    def store_keys(j, sc):
        b = pltpu.bitcast(sc, I32)
        key = b ^ ((b >> 31) & 0x7FFFFFFF)
        tile(ihi, j)[...] = (key >> 16).astype(i16)
        tile(ilo, j)[...] = ((key & 0xFFFF) - 32768).astype(i16)

    def idx_tile(j, _):
        store_keys(j, idx_scores(j))
        return 0

    lax.fori_loop(0, i, idx_tile, 0)
    store_keys(i, jnp.where(causal, idx_scores(i), NEG))

    def count(ref, pred):
        def body(j, c):
            hit = jnp.where(pred(tile(ref, j)[...]), jnp.ones((), i16), jnp.zeros((), i16))
            parts = [hit[k * 16:(k + 1) * 16] for k in range(tk // 16)]
            while len(parts) > 1:
                parts = [a + b for a, b in zip(parts[0::2], parts[1::2])]
            return c + parts[0]
        c16 = lax.fori_loop(0, i + 1, body, jnp.zeros((16, tq), i16))
        return jnp.sum(c16.astype(I32), axis=0, keepdims=True)

    def search(ref, need):
        def bit_step(b, thr):
            cand = thr + (jnp.int32(1) << (15 - b))
            cand16 = cand.astype(i16)
            return jnp.where(count(ref, lambda t: t >= cand16) >= need, cand, thr)
        return lax.fori_loop(0, 16, bit_step, jnp.full((1, tq), -32768, I32))

    thr_hi = search(ihi, top)
    thr_hi16 = thr_hi.astype(i16)
    need = top - count(ihi, lambda t: t > thr_hi16)

    def mask_lo(j, _):
        lo = tile(ilo, j)
        lo[...] = jnp.where(tile(ihi, j)[...] == thr_hi16, lo[...], jnp.full((), -32768, i16))
        return 0

    lax.fori_loop(0, i + 1, mask_lo, 0)
    thr_lo = search(ilo, need)

    n_eq_hi = count(ihi, lambda t: t == thr_hi16)
    thr_lo16 = thr_lo.astype(i16)
    n_sel = (top - need) + jnp.where(thr_lo == -32768, n_eq_hi, count(ilo, lambda t: t >= thr_lo16))
    tie_cap[...] = jnp.full((1, tq), ihi.shape[0], I32)
    excess = n_sel - top
    max_excess = jnp.max(excess)
    n_bits = (ihi.shape[0] - 1).bit_length()

    def eq_tile(j):
        return ((tile(ihi, j)[...].astype(I32) == thr_hi)
                & (tile(ilo, j)[...].astype(I32) == thr_lo))

    @pl.when((max_excess > 0) & (max_excess <= n_bits))
    def _():
        def drop(k, cap):
            def body(j, m8):
                idx = j * tk + row
                v = jnp.where(eq_tile(j) & (idx <= cap), idx, -1)
                return jnp.maximum(m8, jnp.max(v.reshape(tk // 8, 8, tq), axis=0))
            m8 = lax.fori_loop(0, i + 1, body, jnp.full((8, tq), -1, I32))
            return jnp.where(excess > k, jnp.max(m8, axis=0, keepdims=True) - 1, cap)

        tie_cap[...] = lax.fori_loop(0, max_excess, drop, tie_cap[...])

    @pl.when(max_excess > n_bits)
    def _():
        def count32(pred):
            def body(j, c):
                return c + jnp.sum(jnp.where(pred(j), 1, 0).reshape(tk // 8, 8, tq), axis=0)
            c8 = lax.fori_loop(0, i + 1, body, jnp.zeros((8, tq), I32))
            return jnp.sum(c8, axis=0, keepdims=True)

        need_eq = count32(eq_tile) - excess

        def bit_step(b, cap):
            cand = cap + (jnp.int32(1) << (n_bits - 1 - b))
            below = count32(lambda j: eq_tile(j) & (j * tk + row < cand))
            return jnp.where(below >= need_eq, cap, cand)

        tie_cap[...] = lax.fori_loop(0, n_bits, bit_step, jnp.zeros((1, tq), I32))

    m_ref[...] = jnp.full(m_ref.shape, NEG, F32)
    l_ref[...] = jnp.zeros(l_ref.shape, F32)
    acc_ref[...] = jnp.zeros(acc_ref.shape, F32)
    cap = tie_cap[...]

    def att_tile(j, diag):
        hi = tile(ihi, j)[...].astype(I32)
        lo = tile(ilo, j)[...].astype(I32)
        keep = (hi > thr_hi) | ((hi == thr_hi) & ((lo > thr_lo)
                                                  | ((lo == thr_lo) & (j * tk + row <= cap))))
        bias = jnp.where(keep, 0.0, NEG)
        if diag:
            bias = jnp.where(causal, bias, NEG)
        s_all = _dot(kcat_ref[j], q_ref[...])
        ps, alphas = [], []
        for h in range(n_h):
            cs = slice(h * tq, (h + 1) * tq)
            s = s_all[:, cs] + bias
            m_old = m_ref[:, cs]
            m_new = jnp.maximum(m_old, jnp.max(s, axis=0, keepdims=True))
            alpha = jnp.exp2(m_old - m_new)
            p = jnp.exp2(s - m_new)
            l_ref[:, cs] = alpha * l_ref[:, cs] + jnp.sum(p, axis=0, keepdims=True)
            m_ref[:, cs] = m_new
            ps.append(p.astype(BF16))
            alphas.append(alpha)
        p_all = jnp.concatenate(ps, axis=1)
        acc_ref[...] = acc_ref[...] * jnp.concatenate(alphas, axis=1) + _dot(ckt_ref[j], p_all)

    def att_body(j, _):
        att_tile(j, False)
        return 0

    lax.fori_loop(0, i, att_body, 0)
    att_tile(i, True)

    outs = []
    for h in range(n_h):
        cs = slice(h * tq, (h + 1) * tq)
        o = acc_ref[:, cs] / l_ref[:, cs]
        outs.append(_dot(wuv_ref[h], o.astype(BF16)))
    mix = jnp.concatenate(outs, axis=0).T
    y = _dot(mix.astype(BF16), wout_ref[...])
    o_ref[...] = h_ref[...] + _rms_rows(y, g_ref[3:4, :])


def _dsa(h, g, q, qi, wi, kcat, ckt, kidx, wuv_t, w_out, seq):
    bsz = q.shape[0]
    d = h.shape[1]
    tq = min(TQ_DSA, seq)
    nq = seq // tq
    top = min(DSA_TOP, seq // 4)
    wd = MLA_HEADS * tq
    per_b = lambda a: pl.BlockSpec((None,) + a.shape[1:], lambda b, i: (b,) + (0,) * (a.ndim - 1),
                                   pipeline_mode=pl.Buffered(1))
    return pl.pallas_call(
        functools.partial(_dsa_kernel, tq=tq, top=top),
        grid=(bsz, nq),
        in_specs=[pl.BlockSpec((tq, d), lambda b, i: (b * nq + i, 0)), _resident(g.shape),
                  pl.BlockSpec((None, None, MLA_QK, wd), lambda b, i: (b, i, 0, 0)),
                  pl.BlockSpec((None, IDX_HEADS * IDX_DIM, tq), lambda b, i: (b, 0, i)),
                  pl.BlockSpec((None, IDX_HEADS, tq), lambda b, i: (b, 0, i)),
                  per_b(kcat), per_b(ckt), per_b(kidx), _resident(wuv_t.shape),
                  _resident(w_out.shape)],
        out_specs=pl.BlockSpec((tq, d), lambda b, i: (b * nq + i, 0)),
        out_shape=jax.ShapeDtypeStruct(h.shape, F32),
        scratch_shapes=[pltpu.VMEM((seq, tq), jnp.int16),
                        pltpu.VMEM((seq, tq), jnp.int16),
                        pltpu.VMEM((MLA_LATENT, wd), F32),
                        pltpu.VMEM((1, wd), F32),
                        pltpu.VMEM((1, wd), F32),
                        pltpu.VMEM((1, tq), I32)],
        compiler_params=_cparams(("parallel", "arbitrary")),
        name="dsa",
    )(h, g, q, qi, wi, kcat, ckt, kidx, wuv_t, w_out)


def _rope_tables(seq, dim):
    half = dim // 2
    inv = ROPE_THETA ** (-jnp.arange(half, dtype=F32) / half)
    ang = inv[:, None] * jnp.arange(seq, dtype=F32)[None, :]
    return jnp.cos(ang), jnp.sin(ang)


def _even_mixer(h, g, bsz, seq, w_in, w_out, s5, pe_k, pe_v, wk1, wk2, wv1, wv2, tabs):
    cos32, sin32 = tabs[64]
    hd, hk = NSA_HEAD_DIM, NSA_KV_HEADS
    o = 0
    cols = {}
    for name, size in (("u", S5_WIDTH), ("q", NSA_Q), ("kc", NSA_KV), ("vc", NSA_KV), ("ks", NSA_KV),
                       ("vs", NSA_KV), ("kw", NSA_KV), ("vw", NSA_KV), ("gt", 3 * NSA_HEADS)):
        cols[name] = w_in[:, o:o + size]
        o += size
    wu = cols["u"].astype(BF16)
    wt = jnp.concatenate([cols[n] for n in ("q", "kc", "ks", "kw", "vc", "vs", "vw", "gt")], axis=1).T
    wt = jnp.pad(wt, ((0, (-wt.shape[0]) % 16), (0, 0))).astype(BF16)
    u, q_t, k_t, v_t, gt_t = _proj_even(h, g, wu, wt, cos32, sin32, bsz, seq)

    a_re, a_im, b_re, b_im, c_re, c_im, log_dt, d_skip, w_glu = s5
    mats = _s5_matrices(a_re, a_im, b_re, b_im, c_re, c_im, log_dt, S5_CHUNK)
    ys = _s5_scan(u, mats, bsz, seq)

    k5 = k_t.reshape(bsz, 3, hk, hd, seq)
    v5 = v_t.reshape(bsz, 3, hk, hd, seq)
    nb = seq // CMP_STRIDE
    half_blk = lambda a: (a.reshape(bsz, hk, hd, nb, CMP_STRIDE).transpose(0, 1, 3, 4, 2)
                          .reshape(bsz, hk, nb, CMP_STRIDE * hd))
    pe2 = lambda pe: pe.reshape(2, CMP_STRIDE * hd)
    w1s = lambda w: w.reshape(2, CMP_STRIDE * hd, hd).astype(BF16)
    kcmp = _compress(half_blk(k5[:, 0]), pe2(pe_k), w1s(wk1), wk2.astype(BF16))
    vcmp = _compress(half_blk(v5[:, 0]), pe2(pe_v), w1s(wv1), wv2.astype(BF16))
    vcmp_t = vcmp.transpose(0, 1, 3, 2)
    tk = min(TK_NSA, seq)
    ks = k5[:, 1].reshape(bsz, hk, hd, seq // tk, tk).transpose(0, 1, 3, 4, 2)
    vs_t = v5[:, 1].reshape(bsz, hk, hd, seq // tk, tk).transpose(0, 1, 3, 2, 4)
    kw = k5[:, 2].reshape(bsz, hk, hd, seq // LANE, LANE).transpose(0, 1, 3, 4, 2)
    vw_t = v5[:, 2].reshape(bsz, hk, hd, seq // LANE, LANE).transpose(0, 1, 3, 2, 4)
    n_sb = seq // SEL_BLOCK
    c_start = jnp.arange(nb) * CMP_STRIDE
    b_start = jnp.arange(n_sb) * SEL_BLOCK
    ov_t = ((c_start[None, :] < b_start[:, None] + SEL_BLOCK)
            & (c_start[None, :] + CMP_LEN > b_start[:, None])
            & (jnp.arange(nb)[None, :] < nb - 1)).astype(BF16)
    gates = gt_t.reshape(bsz, hk, 3 * NSA_GQA, seq)
    b_out = _nsa(q_t, gates, kcmp, vcmp_t, ks, vs_t, kw, vw_t, ov_t, seq)
    return _outproj_even(h, ys, u, b_out, g, d_skip.reshape(1, -1), w_glu.astype(BF16),
                         w_out[:S5_WIDTH].astype(BF16), w_out[S5_WIDTH:].astype(BF16))


def _odd_mixer(h, g, bsz, seq, w_in, kv_norm, w_uv, w_out, tabs):
    cos32, sin32 = tabs[64]
    cos16, sin16 = tabs[32]
    d = h.shape[1]
    sizes = (MLA_HEADS * MLA_LATENT, MLA_HEADS * MLA_ROPE, MLA_LATENT, MLA_ROPE,
             IDX_HEADS * IDX_DIM, IDX_DIM, IDX_HEADS)
    parts = []
    o = 0
    for s in sizes:
        parts.append(w_in[:, o:o + s])
        o += s
    w_ql, w_qr, w_c, w_kr, w_qi, w_ki, w_wi = parts
    wq = jnp.concatenate([w_ql.reshape(d, MLA_HEADS, MLA_LATENT), w_qr.reshape(d, MLA_HEADS, MLA_ROPE)],
                         axis=2).reshape(d, MLA_HEADS * MLA_QK).T.astype(BF16)
    wkv = jnp.concatenate([w_c, w_kr, w_qi, w_ki, w_wi], axis=1).T
    wkv = jnp.pad(wkv, ((0, (-wkv.shape[0]) % 16), (0, 0))).astype(BF16)
    ckv_t, kr_t, qi_t, ki_t, wi_t = _proj_odd_kv(h, g, wkv, kv_norm.reshape(-1, 1), cos32, sin32,
                                                 cos16, sin16, bsz, seq)
    q = _proj_odd_q(h, g, wq, cos16, sin16, bsz, seq)
    tk = min(TQ_DSA, seq)
    nk = seq // tk
    kcat = jnp.concatenate([ckv_t, kr_t], axis=1).transpose(0, 2, 1).reshape(bsz, nk, tk, MLA_QK)
    ckt = ckv_t.reshape(bsz, MLA_LATENT, nk, tk).transpose(0, 2, 1, 3)
    kidx = ki_t.transpose(0, 2, 1).reshape(bsz, nk, tk, IDX_DIM)
    wuv_t = w_uv.transpose(0, 2, 1).astype(BF16)
    return _dsa(h, g, q, qi_t, wi_t, kcat, ckt, kidx, wuv_t, w_out.astype(BF16), seq)


def kernel(x, p, norm_g, ffn1_w_in, ffn1_w_out, ffn2_w_in, ffn2_w_out, ple_w_gate, ple_w_proj,
           ev_w_in, ev_w_out, s5_a_re, s5_a_im, s5_b_re, s5_b_im, s5_c_re, s5_c_im, s5_log_dt, s5_d,
           s5_w_glu, nsa_pe_k, nsa_pe_v, nsa_wk1, nsa_wk2, nsa_wv1, nsa_wv2,
           od_w_in, od_kv_norm, od_w_uv, od_w_out):
    bsz, seq, d = x.shape
    depth = norm_g.shape[0]
    tabs = {64: _rope_tables(seq, 64), 32: _rope_tables(seq, 32)}
    h = x.reshape(bsz * seq, d)
    for i in range(depth):
        g = norm_g[i]
        j = i // 2
        h = _ffn(h, g, ffn1_w_in[i].astype(BF16), ffn1_w_out[i].astype(BF16), 0)
        if i % 2 == 0:
            s5 = (s5_a_re[j], s5_a_im[j], s5_b_re[j], s5_b_im[j], s5_c_re[j], s5_c_im[j],
                  s5_log_dt[j], s5_d[j], s5_w_glu[j])
            h = _even_mixer(h, g, bsz, seq, ev_w_in[j], ev_w_out[j], s5, nsa_pe_k[j], nsa_pe_v[j],
                            nsa_wk1[j], nsa_wk2[j], nsa_wv1[j], nsa_wv2[j], tabs)
        else:
            h = _odd_mixer(h, g, bsz, seq, od_w_in[j], od_kv_norm[j], od_w_uv[j], od_w_out[j], tabs)
        h = _ffn_ple(h, p[i].reshape(bsz * seq, -1), g, ffn2_w_in[i].astype(BF16),
                     ffn2_w_out[i].astype(BF16), ple_w_gate[i].astype(BF16),
                     ple_w_proj[i].astype(BF16))
    return h.reshape(bsz, seq, d)
```

```python
import functools
import math

import jax
import jax.numpy as jnp
from jax import lax
from jax.experimental import pallas as pl
from jax.experimental.pallas import tpu as pltpu

F32 = jnp.float32
BF16 = jnp.bfloat16
I32 = jnp.int32

ROPE_THETA = 10000.0
EPS = 1e-6
NEG = -1e30
LOG2E = math.log2(math.e)
D_FF = 2816
S5_WIDTH = 512
S5_GROUP = 16
S5_GROUPS = S5_WIDTH // S5_GROUP
S5_STATE = 64
NSA_HEADS = 8
NSA_KV_HEADS = 2
NSA_GQA = NSA_HEADS // NSA_KV_HEADS
NSA_HEAD_DIM = 64
CMP_LEN = 32
CMP_STRIDE = 16
SEL_BLOCK = 64
SEL_TOP = 16
WINDOW = 512
FORCE_BONUS = 1000.0
NSA_Q = NSA_HEADS * NSA_HEAD_DIM
NSA_KV = NSA_KV_HEADS * NSA_HEAD_DIM
MLA_HEADS = 16
MLA_LATENT = 256
MLA_ROPE = 32
MLA_QK = MLA_LATENT + MLA_ROPE
MLA_V_DIM = 64
IDX_HEADS = 8
IDX_DIM = 64
DSA_TOP = 256

V7X_VMEM_BYTES = 64 * 2**20
VMEM_LIMIT = V7X_VMEM_BYTES - 8 * 2**20
LANE = 128
TM_FFN = 1024
TQ_PROJ = 256
TQ_NSA = 128
TK_NSA = 512
TQ_DSA = 256
TA_DSA = 512
S5_CHUNK = 64


def _cparams(sem):
    return pltpu.CompilerParams(dimension_semantics=sem, vmem_limit_bytes=VMEM_LIMIT)


def _resident(shape):
    nd = len(shape)
    return pl.BlockSpec(shape, lambda *_: (0,) * nd, pipeline_mode=pl.Buffered(1))


def _dot(a, b):
    return jnp.dot(a, b, preferred_element_type=F32)


def _dot_nt(a, b):
    return lax.dot_general(a, b, (((1,), (1,)), ((), ())), preferred_element_type=F32)


def _rms_rows(x, g):
    return x * lax.rsqrt(jnp.mean(x * x, axis=-1, keepdims=True) + EPS) * g


def _rope_fmaj(y, cos, sin):
    half = y.shape[1] // 2
    t1 = y[:, :half, :]
    t2 = y[:, half:, :]
    return jnp.concatenate([t1 * cos - t2 * sin, t2 * cos + t1 * sin], axis=1)


def _ffn_tile(x, g_ref, g0, win_ref, wout_ref, n_chunk):
    xn = _rms_rows(x, g_ref[g0:g0 + 1, :]).astype(BF16)
    ck = D_FF // n_chunk
    acc = None
    for c in range(n_chunk):
        a = _dot(xn, win_ref[:, c * ck:(c + 1) * ck])
        u = _dot(xn, win_ref[:, D_FF + c * ck:D_FF + (c + 1) * ck])
        act = (jax.nn.silu(a) * u).astype(BF16)
        y = _dot(act, wout_ref[c * ck:(c + 1) * ck, :])
        acc = y if acc is None else acc + y
    return x + 0.5 * _rms_rows(acc, g_ref[g0 + 1:g0 + 2, :])


def _ffn_kernel(h_ref, g_ref, win_ref, wout_ref, o_ref, *, g0, n_chunk):
    o_ref[...] = _ffn_tile(h_ref[...], g_ref, g0, win_ref, wout_ref, n_chunk)


def _ffn_ple_kernel(h_ref, p_ref, g_ref, win_ref, wout_ref, wg_ref, wp_ref, o_ref, *, n_chunk):
    x = _ffn_tile(h_ref[...], g_ref, 4, win_ref, wout_ref, n_chunk)
    gate = jax.nn.sigmoid(_dot(_rms_rows(x, g_ref[6:7, :]).astype(BF16), wg_ref[...]))
    e = _dot(p_ref[...].astype(BF16), wp_ref[...]) * gate
    o_ref[...] = x + _rms_rows(e, g_ref[7:8, :])


def _ffn_ple(h, p, g, w_in, w_out, w_gate, w_proj):
    t, d = h.shape
    tm = min(TM_FFN, t)
    return pl.pallas_call(
        functools.partial(_ffn_ple_kernel, n_chunk=2),
        grid=(t // tm,),
        in_specs=[pl.BlockSpec((tm, d), lambda i: (i, 0)),
                  pl.BlockSpec((tm, p.shape[1]), lambda i: (i, 0)), _resident(g.shape),
                  _resident(w_in.shape), _resident(w_out.shape), _resident(w_gate.shape),
                  _resident(w_proj.shape)],
        out_specs=pl.BlockSpec((tm, d), lambda i: (i, 0)),
        out_shape=jax.ShapeDtypeStruct((t, d), F32),
        compiler_params=_cparams(("parallel",)),
        name="ffn_ple",
    )(h, p, g, w_in, w_out, w_gate, w_proj)


def _ffn(h, g, w_in, w_out, g0):
    t, d = h.shape
    tm = min(TM_FFN, t)
    return pl.pallas_call(
        functools.partial(_ffn_kernel, g0=g0, n_chunk=2),
        grid=(t // tm,),
        in_specs=[pl.BlockSpec((tm, d), lambda i: (i, 0)), _resident(g.shape),
                  _resident(w_in.shape), _resident(w_out.shape)],
        out_specs=pl.BlockSpec((tm, d), lambda i: (i, 0)),
        out_shape=jax.ShapeDtypeStruct((t, d), F32),
        compiler_params=_cparams(("parallel",)),
        name="ffn",
    )(h, g, w_in, w_out)


def _outproj_even_kernel(h_ref, ys_ref, u_ref, b_ref, g_ref, d_ref, wglu_ref, wa_ref, wb_ref, o_ref):
    y = ys_ref[...] + d_ref[...] * u_ref[...]
    z = jax.nn.gelu(y)
    a = z * jax.nn.sigmoid(_dot(z.astype(BF16), wglu_ref[...]))
    mix = _dot(a.astype(BF16), wa_ref[...]) + _dot(b_ref[...].astype(BF16), wb_ref[...])
    o_ref[...] = h_ref[...] + _rms_rows(mix, g_ref[3:4, :])


def _outproj_even(h, ys, u, b_out, g, d_skip, w_glu, w_a, w_b):
    t, d = h.shape
    tm = min(TM_FFN, t)
    tok = lambda w: pl.BlockSpec((tm, w), lambda i: (i, 0))
    return pl.pallas_call(
        _outproj_even_kernel,
        grid=(t // tm,),
        in_specs=[tok(d), tok(S5_WIDTH), tok(S5_WIDTH), tok(NSA_Q), _resident(g.shape),
                  _resident(d_skip.shape), _resident(w_glu.shape), _resident(w_a.shape),
                  _resident(w_b.shape)],
        out_specs=tok(d),
        out_shape=jax.ShapeDtypeStruct((t, d), F32),
        compiler_params=_cparams(("parallel",)),
        name="outproj_even",
    )(h, ys, u, b_out, g, d_skip, w_glu, w_a, w_b)


def _proj_even_kernel(h_ref, g_ref, wu_ref, wt_ref, cos_ref, sin_ref,
                      u_ref, q_ref, k_ref, v_ref, gt_ref):
    xn = _rms_rows(h_ref[...], g_ref[2:3, :]).astype(BF16)
    u_ref[...] = _dot(xn, wu_ref[...])
    y = _dot_nt(wt_ref[...], xn)
    tq = y.shape[1]
    cos = cos_ref[...]
    sin = sin_ref[...]
    hd = NSA_HEAD_DIM
    q = _rope_fmaj(y[0:NSA_Q].reshape(NSA_HEADS, hd, tq), cos, sin)
    q_ref[...] = (q * (hd ** -0.5 * LOG2E)).reshape(NSA_Q, tq)
    k0 = NSA_Q
    nk = 3 * NSA_KV
    k = _rope_fmaj(y[k0:k0 + nk].reshape(3 * NSA_KV_HEADS, hd, tq), cos, sin)
    k_ref[...] = k.reshape(nk, tq).astype(k_ref.dtype)
    v0 = k0 + nk
    v_ref[...] = y[v0:v0 + nk].astype(v_ref.dtype)
    g0 = v0 + nk
    gt_ref[...] = jax.nn.sigmoid(y[g0:g0 + 3 * NSA_HEADS])


def _proj_even(h, g, wu, wt, cos, sin, bsz, seq):
    d = h.shape[1]
    tq = min(TQ_PROJ, seq)
    nq = seq // tq
    nk = 3 * NSA_KV
    fm = lambda rows: pl.BlockSpec((None, rows, tq), lambda b, i: (b, 0, i))
    return pl.pallas_call(
        _proj_even_kernel,
        grid=(bsz, nq),
        in_specs=[pl.BlockSpec((tq, d), lambda b, i: (b * nq + i, 0)), _resident(g.shape),
                  _resident(wu.shape), _resident(wt.shape),
                  pl.BlockSpec((cos.shape[0], tq), lambda b, i: (0, i)),
                  pl.BlockSpec((sin.shape[0], tq), lambda b, i: (0, i))],
        out_specs=[pl.BlockSpec((tq, S5_WIDTH), lambda b, i: (b * nq + i, 0)),
                   fm(NSA_Q), fm(nk), fm(nk), fm(3 * NSA_HEADS)],
        out_shape=[jax.ShapeDtypeStruct((bsz * seq, S5_WIDTH), F32),
                   jax.ShapeDtypeStruct((bsz, NSA_Q, seq), F32),
                   jax.ShapeDtypeStruct((bsz, nk, seq), BF16),
                   jax.ShapeDtypeStruct((bsz, nk, seq), BF16),
                   jax.ShapeDtypeStruct((bsz, 3 * NSA_HEADS, seq), F32)],
        compiler_params=_cparams(("parallel", "parallel")),
        name="proj_even",
    )(h, g, wu, wt, cos, sin)


def _proj_odd_kv_kernel(h_ref, g_ref, wt_ref, kvn_ref, cos32_ref, sin32_ref, cos16_ref, sin16_ref,
                        ckv_ref, kr_ref, qi_ref, ki_ref, wi_ref):
    xn = _rms_rows(h_ref[...], g_ref[2:3, :]).astype(BF16)
    y = _dot_nt(wt_ref[...], xn)
    tq = y.shape[1]
    c = y[0:MLA_LATENT]
    c = c * lax.rsqrt(jnp.mean(c * c, axis=0, keepdims=True) + EPS) * kvn_ref[...]
    ckv_ref[...] = c.astype(ckv_ref.dtype)
    r0 = MLA_LATENT
    kr = _rope_fmaj(y[r0:r0 + MLA_ROPE].reshape(1, MLA_ROPE, tq), cos16_ref[...], sin16_ref[...])
    kr_ref[...] = kr.reshape(MLA_ROPE, tq).astype(kr_ref.dtype)
    q0 = r0 + MLA_ROPE
    nqi = IDX_HEADS * IDX_DIM
    qi = _rope_fmaj(y[q0:q0 + nqi].reshape(IDX_HEADS, IDX_DIM, tq), cos32_ref[...], sin32_ref[...])
    qi_ref[...] = qi.reshape(nqi, tq).astype(qi_ref.dtype)
    k0 = q0 + nqi
    ki = _rope_fmaj(y[k0:k0 + IDX_DIM].reshape(1, IDX_DIM, tq), cos32_ref[...], sin32_ref[...])
    ki_ref[...] = ki.reshape(IDX_DIM, tq).astype(ki_ref.dtype)
    w0 = k0 + IDX_DIM
    wi_ref[...] = y[w0:w0 + IDX_HEADS]


def _proj_odd_kv(h, g, wt, kvn, cos32, sin32, cos16, sin16, bsz, seq):
    d = h.shape[1]
    tq = min(TQ_PROJ, seq)
    nq = seq // tq
    fm = lambda rows: pl.BlockSpec((None, rows, tq), lambda b, i: (b, 0, i))
    tab = lambda a: pl.BlockSpec((a.shape[0], tq), lambda b, i: (0, i))
    nqi = IDX_HEADS * IDX_DIM
    return pl.pallas_call(
        _proj_odd_kv_kernel,
        grid=(bsz, nq),
        in_specs=[pl.BlockSpec((tq, d), lambda b, i: (b * nq + i, 0)), _resident(g.shape),
                  _resident(wt.shape), _resident(kvn.shape),
                  tab(cos32), tab(sin32), tab(cos16), tab(sin16)],
        out_specs=[fm(MLA_LATENT), fm(MLA_ROPE), fm(nqi), fm(IDX_DIM), fm(IDX_HEADS)],
        out_shape=[jax.ShapeDtypeStruct((bsz, MLA_LATENT, seq), BF16),
                   jax.ShapeDtypeStruct((bsz, MLA_ROPE, seq), BF16),
                   jax.ShapeDtypeStruct((bsz, nqi, seq), BF16),
                   jax.ShapeDtypeStruct((bsz, IDX_DIM, seq), BF16),
                   jax.ShapeDtypeStruct((bsz, IDX_HEADS, seq), F32)],
        compiler_params=_cparams(("parallel", "parallel")),
        name="proj_odd_kv",
    )(h, g, wt, kvn, cos32, sin32, cos16, sin16)


def _proj_odd_q_kernel(h_ref, g_ref, wt_ref, cos16_ref, sin16_ref, q_ref):
    xn = _rms_rows(h_ref[...], g_ref[2:3, :]).astype(BF16)
    tq = xn.shape[0]
    cos = cos16_ref[...]
    sin = sin16_ref[...]
    for hh in range(MLA_HEADS):
        y = _dot_nt(wt_ref[hh * MLA_QK:(hh + 1) * MLA_QK, :], xn)
        r = _rope_fmaj(y[MLA_LATENT:].reshape(1, MLA_ROPE, tq), cos, sin)
        q = jnp.concatenate([y[:MLA_LATENT], r.reshape(MLA_ROPE, tq)], axis=0)
        q_ref[:, hh * tq:(hh + 1) * tq] = (q * (MLA_QK ** -0.5 * LOG2E)).astype(q_ref.dtype)


def _proj_odd_q(h, g, wt, cos16, sin16, bsz, seq):
    d = h.shape[1]
    tq = min(TQ_DSA, seq)
    nq = seq // tq
    return pl.pallas_call(
        _proj_odd_q_kernel,
        grid=(bsz, nq),
        in_specs=[pl.BlockSpec((tq, d), lambda b, i: (b * nq + i, 0)), _resident(g.shape),
                  _resident(wt.shape),
                  pl.BlockSpec((cos16.shape[0], tq), lambda b, i: (0, i)),
                  pl.BlockSpec((sin16.shape[0], tq), lambda b, i: (0, i))],
        out_specs=pl.BlockSpec((None, None, MLA_QK, MLA_HEADS * tq), lambda b, i: (b, i, 0, 0)),
        out_shape=jax.ShapeDtypeStruct((bsz, nq, MLA_QK, MLA_HEADS * tq), BF16),
        compiler_params=_cparams(("parallel", "parallel")),
        name="proj_odd_q",
    )(h, g, wt, cos16, sin16)


def _s5_state_kernel(u_ref, bc_ref, s_ref):
    s_ref[...] = _dot(u_ref[...], bc_ref[...])


def _s5_scan_kernel(sr_ref, si_ref, lr_ref, li_ref, xr_ref, xi_ref):
    n_chunk = sr_ref.shape[0]
    lr = lr_ref[...]
    li = li_ref[...]

    def body(c, carry):
        xr, xi = carry
        xr_ref[c] = xr
        xi_ref[c] = xi
        return (lr * xr - li * xi + sr_ref[c], lr * xi + li * xr + si_ref[c])

    zero = jnp.zeros(sr_ref.shape[1:], F32)
    lax.fori_loop(0, n_chunk, body, (zero, zero))


def _s5_out_kernel(u_ref, x_ref, m_ref, cc_ref, y_ref):
    x = x_ref[...]
    hi = x.astype(BF16)
    lo = (x - hi.astype(F32)).astype(BF16)
    cc = cc_ref[...]
    y_ref[...] = _dot(u_ref[...], m_ref[...]) + _dot(hi, cc) + _dot(lo, cc)


def _s5_matrices(a_re, a_im, b_re, b_im, c_re, c_im, log_dt, tc):
    hp = lax.Precision.HIGHEST
    dt = jnp.exp(log_dt)[:, None]
    lam = lax.complex(a_re, a_im)
    lam_dt = lam * dt
    lam_bar = jnp.exp(lam_dt)
    b_bar = ((lam_bar - 1.0) / lam)[..., None] * lax.complex(b_re, b_im)
    k = jnp.arange(tc + 1, dtype=F32)[:, None, None]
    pw = jnp.exp(lam_dt[None] * k)
    pr, pi = jnp.real(pw), jnp.imag(pw)
    bbr, bbi = jnp.real(b_bar), jnp.imag(b_bar)
    cpr = c_re[None] * pr[:, :, None, :] - c_im[None] * pi[:, :, None, :]
    cpi = c_re[None] * pi[:, :, None, :] + c_im[None] * pr[:, :, None, :]
    kk = (jnp.einsum('kgpn,gnq->kgpq', cpr[:tc], bbr, precision=hp)
          - jnp.einsum('kgpn,gnq->kgpq', cpi[:tc], bbi, precision=hp))
    n_g, n_p = a_re.shape[0], b_re.shape[2]
    kq = jnp.pad(kk.transpose(1, 3, 0, 2).astype(BF16), ((0, 0), (0, 0), (tc, 0), (0, 0)))
    lag = tc + jnp.arange(tc)[None, :] - jnp.arange(tc)[:, None]
    m = kq[:, :, lag, :].reshape(n_g, n_p * tc, tc * n_p)
    rev = pw[tc - 1 - jnp.arange(tc)]
    bc = rev[:, :, :, None] * b_bar[None]
    bc = bc.transpose(1, 3, 0, 2).reshape(n_g, n_p * tc, -1)
    bc = jnp.concatenate([jnp.real(bc), jnp.imag(bc)], axis=-1)
    mr = cpr[1:tc + 1].transpose(1, 3, 0, 2).reshape(n_g, -1, tc * n_p)
    mi = cpi[1:tc + 1].transpose(1, 3, 0, 2).reshape(n_g, -1, tc * n_p)
    cc = jnp.concatenate([mr, -mi], axis=1)
    ltc = pw[tc].reshape(1, -1)
    return m, bc.astype(BF16), cc.astype(BF16), jnp.real(ltc), jnp.imag(ltc)


def _s5_scan(u, mats, bsz, seq):
    m, bc, cc, lr, li = mats
    tc = S5_CHUNK
    n_c = seq // tc
    n_g, n_p, n_s = S5_GROUPS, S5_GROUP, S5_STATE
    rows = bsz * n_c
    kd = tc * n_p
    ug = (u.reshape(bsz, n_c, tc, n_g, n_p).transpose(3, 0, 1, 4, 2)
          .reshape(n_g, rows, kd).astype(BF16))
    grp = lambda a, b: pl.BlockSpec((None, a, b), lambda gi: (gi, 0, 0))
    s = pl.pallas_call(
        _s5_state_kernel,
        grid=(n_g,),
        in_specs=[grp(rows, kd), grp(kd, 2 * n_s)],
        out_specs=grp(rows, 2 * n_s),
        out_shape=jax.ShapeDtypeStruct((n_g, rows, 2 * n_s), F32),
        compiler_params=_cparams(("parallel",)),
        name="s5_state",
    )(ug, bc)
    s5 = s.reshape(n_g, bsz, n_c, 2, n_s).transpose(3, 2, 1, 0, 4).reshape(2, n_c, bsz, n_g * n_s)
    full = lambda shp: pl.BlockSpec(shp, lambda: (0,) * len(shp))
    xr, xi = pl.pallas_call(
        _s5_scan_kernel,
        in_specs=[full(s5.shape[1:]), full(s5.shape[1:]), full(lr.shape), full(li.shape)],
        out_specs=[full(s5.shape[1:]), full(s5.shape[1:])],
        out_shape=[jax.ShapeDtypeStruct(s5.shape[1:], F32)] * 2,
        compiler_params=pltpu.CompilerParams(vmem_limit_bytes=VMEM_LIMIT),
        name="s5_scan",
    )(s5[0], s5[1], lr, li)
    x = jnp.stack([xr, xi]).reshape(2, n_c, bsz, n_g, n_s).transpose(3, 2, 1, 0, 4)
    x = x.reshape(n_g, rows, 2 * n_s)
    y = pl.pallas_call(
        _s5_out_kernel,
        grid=(n_g,),
        in_specs=[grp(rows, kd), grp(rows, 2 * n_s), grp(kd, kd), grp(2 * n_s, kd)],
        out_specs=grp(rows, kd),
        out_shape=jax.ShapeDtypeStruct((n_g, rows, kd), F32),
        compiler_params=_cparams(("parallel",)),
        name="s5_out",
    )(ug, x, m, cc)
    return (y.reshape(n_g, bsz, n_c, tc, n_p).transpose(1, 2, 3, 0, 4)
            .reshape(bsz * seq, n_g * n_p))


def _compress_kernel(x_ref, pe_ref, w1_ref, w2_ref, o_ref):
    x = x_ref[...].astype(F32)
    nb = x.shape[0]
    a = _dot((x + pe_ref[0:1, :]).astype(BF16), w1_ref[0])
    b = _dot((x + pe_ref[1:2, :]).astype(BF16), w1_ref[1])
    pre = a + pltpu.roll(b, nb - 1, 0)
    o_ref[...] = _dot(jax.nn.gelu(pre).astype(BF16), w2_ref[...]).astype(o_ref.dtype)


def _compress(x, pe, w1, w2):
    bsz, hk, nb, kd = x.shape
    hd = w2.shape[1]
    return pl.pallas_call(
        _compress_kernel,
        grid=(bsz, hk),
        in_specs=[pl.BlockSpec((None, None, nb, kd), lambda b, h: (b, h, 0, 0)),
                  _resident(pe.shape), _resident(w1.shape), _resident(w2.shape)],
        out_specs=pl.BlockSpec((None, None, nb, hd), lambda b, h: (b, h, 0, 0)),
        out_shape=jax.ShapeDtypeStruct((bsz, hk, nb, hd), BF16),
        compiler_params=_cparams(("parallel", "parallel")),
        name="nsa_compress",
    )(x, pe, w1, w2)


def _nsa_kernel(q_ref, g_ref, kc_ref, vct_ref, ks_ref, vst_ref, kw_ref, vwt_ref, ov_ref,
                o_ref, sel_ref, zero_ref, sa_ref, sb_ref, *, tq, tk, top_n):
    gq, hd = NSA_GQA, NSA_HEAD_DIM
    n_sb = sel_ref.shape[0]
    nb = kc_ref.shape[0]
    wd = gq * tq
    i = pl.program_id(2)
    s0 = i * tq
    q = q_ref[...]
    qt = jnp.concatenate([q[g * hd:(g + 1) * hd, :] for g in range(gq)], axis=1).astype(BF16)
    t1 = s0 + lax.broadcasted_iota(I32, (1, tq), 1)
    t4 = s0 + (lax.broadcasted_iota(I32, (1, wd), 1) & (tq - 1))

    sc = _dot(kc_ref[...], qt)
    c_last = lax.broadcasted_iota(I32, (nb, 1), 0) * CMP_STRIDE + (CMP_LEN - 1)
    bias_c = jnp.where(c_last <= t1, 0.0, NEG)
    sm = sc + jnp.concatenate([bias_c] * gq, axis=1)
    e = jnp.exp2(sm - jnp.max(sm, axis=0, keepdims=True))
    p_c = e * jnp.where(t4 >= CMP_LEN - 1, 1.0 / jnp.sum(e, axis=0, keepdims=True), 0.0)
    o_c = _dot(vct_ref[...], p_c.astype(BF16))
    psum = p_c[:, 0:tq]
    for g in range(1, gq):
        psum = psum + p_c[:, g * tq:(g + 1) * tq]
    hi = psum.astype(BF16)
    lo = (psum - hi.astype(F32)).astype(BF16)
    imp = _dot(ov_ref[...], hi) + _dot(ov_ref[...], lo)
    blk = lax.broadcasted_iota(I32, (n_sb, 1), 0)
    cur = t1 >> int(math.log2(SEL_BLOCK))
    forced = (blk == 0) | (blk == cur) | (blk == cur - 1)
    imp = imp + jnp.where(forced, FORCE_BONUS, 0.0)
    imp = jnp.where(blk * SEL_BLOCK <= t1, imp, NEG)

    rowf = lax.broadcasted_iota(I32, (n_sb, tq), 0).astype(F32)
    sel = jnp.zeros((n_sb, tq), F32)
    x = imp
    for _ in range(top_n):
        mx = jnp.max(x, axis=0, keepdims=True)
        first = jnp.min(jnp.where(x == mx, rowf, float(n_sb)), axis=0, keepdims=True)
        hit = rowf == first
        sel = jnp.where(hit, 1.0, sel)
        x = jnp.where(hit, -jnp.inf, x)
    sel_ref[...] = sel

    zero_ref[...] = jnp.zeros(zero_ref.shape, F32)
    init = (jnp.full((1, wd), NEG, F32), jnp.zeros((1, wd), F32), zero_ref[...])

    bpt = tk // SEL_BLOCK
    key_row = lax.broadcasted_iota(I32, (tk, 1), 0)

    last_tile = ks_ref.shape[0] - 1

    def scores(j):
        return _dot(ks_ref[jnp.minimum(j, last_tile)], qt)

    def consume(j, s_ref, carry):
        m, l, acc, pv = carry
        jc = jnp.minimum(j, last_tile)
        rows = [jnp.broadcast_to(sel_ref[pl.ds(jc * bpt + b, 1), :], (SEL_BLOCK, tq))
                for b in range(bpt)]
        keep = (jnp.concatenate(rows, axis=0) > 0.5) & (j * tk + key_row <= t1)
        bias = jnp.where(keep, 0.0, NEG)
        s = s_ref[...] + jnp.concatenate([bias] * gq, axis=1)
        m_new = jnp.maximum(m, jnp.max(s, axis=0, keepdims=True))
        alpha = jnp.exp2(m - m_new)
        p = jnp.exp2(s - m_new)
        l = alpha * l + jnp.sum(p, axis=0, keepdims=True)
        return m_new, l, alpha * (acc + pv), _dot(vst_ref[jc], p.astype(BF16))

    def tile_pair(k, carry):
        j = 2 * k
        sb_ref[...] = scores(j + 1)
        carry = consume(j, sa_ref, carry)
        sa_ref[...] = scores(j + 2)
        return consume(j + 1, sb_ref, carry)

    sa_ref[...] = scores(0)
    _, l_s, a_s, pv_s = lax.fori_loop(0, (s0 // tk + 2) // 2, tile_pair, init + (zero_ref[...],))
    o_s = (a_s + pv_s) / l_s

    n_wt = (WINDOW + tq) // LANE
    j_lo = jnp.clip(i * (tq // LANE) - WINDOW // LANE, 0, kw_ref.shape[0] - n_wt)
    kwin = jnp.concatenate([kw_ref[j_lo + w] for w in range(n_wt)], axis=0)
    vwin = jnp.concatenate([vwt_ref[j_lo + w] for w in range(n_wt)], axis=1)
    sw = _dot(kwin, qt)
    diff = t1 - (j_lo * LANE + lax.broadcasted_iota(I32, (n_wt * LANE, 1), 0))
    sw = sw + jnp.concatenate([jnp.where((diff >= 0) & (diff < WINDOW), 0.0, NEG)] * gq, axis=1)
    pw = jnp.exp2(sw - jnp.max(sw, axis=0, keepdims=True))
    o_w = _dot(vwin, pw.astype(BF16)) / jnp.sum(pw, axis=0, keepdims=True)

    gt = g_ref[...]
    outs = []
    for g in range(gq):
        cs = slice(g * tq, (g + 1) * tq)
        outs.append(gt[3 * g:3 * g + 1, :] * o_c[:, cs] + gt[3 * g + 1:3 * g + 2, :] * o_s[:, cs]
                    + gt[3 * g + 2:3 * g + 3, :] * o_w[:, cs])
    o_ref[...] = jnp.concatenate(outs, axis=0).T


def _nsa(q, gates, kcmp, vcmp_t, ks, vs_t, kw, vw_t, ov_t, seq):
    bsz = q.shape[0]
    tq = min(TQ_NSA, seq)
    tk = min(TK_NSA, seq)
    n_sb = seq // SEL_BLOCK
    gq, hd = NSA_GQA, NSA_HEAD_DIM
    per_head = lambda a: pl.BlockSpec((None, None) + a.shape[2:],
                                      lambda b, h, i: (b, h) + (0,) * (a.ndim - 2),
                                      pipeline_mode=pl.Buffered(1))
    return pl.pallas_call(
        functools.partial(_nsa_kernel, tq=tq, tk=tk, top_n=min(SEL_TOP, n_sb)),
        grid=(bsz, NSA_KV_HEADS, seq // tq),
        in_specs=[pl.BlockSpec((None, gq * hd, tq), lambda b, h, i: (b, h, i)),
                  pl.BlockSpec((None, None, 3 * gq, tq), lambda b, h, i: (b, h, 0, i)),
                  per_head(kcmp), per_head(vcmp_t), per_head(ks), per_head(vs_t),
                  per_head(kw), per_head(vw_t), _resident(ov_t.shape)],
        out_specs=pl.BlockSpec((tq, gq * hd), lambda b, h, i: (b * (seq // tq) + i, h)),
        out_shape=jax.ShapeDtypeStruct((bsz * seq, NSA_Q), F32),
        scratch_shapes=[pltpu.VMEM((n_sb, tq), F32), pltpu.VMEM((hd, gq * tq), F32),
                        pltpu.VMEM((tk, gq * tq), F32), pltpu.VMEM((tk, gq * tq), F32)],
        compiler_params=_cparams(("parallel", "parallel", "arbitrary")),
        name="nsa",
    )(q, gates, kcmp, vcmp_t, ks, vs_t, kw, vw_t, ov_t)


def _dsa_kernel(h_ref, g_ref, q_ref, qi_ref, wi_ref, kcat_ref, ckt_ref, kidx_ref, wuv_ref, wout_ref,
                o_ref, ihi, ilo, acc_ref, m_ref, l_ref, tie_cap, *, tq, top):
    tk = tq
    n_h = MLA_HEADS
    i16 = jnp.int16
    i = pl.program_id(1)
    row = lax.broadcasted_iota(I32, (tk, 1), 0)
    col = lax.broadcasted_iota(I32, (1, tq), 1)
    causal = row <= col

    def tile(ref, j):
        return ref.at[pl.ds(pl.multiple_of(j * tk, tk), tk), :]

    qi = qi_ref[...]
    wi = wi_ref[...] * (IDX_HEADS ** -0.5) * (IDX_DIM ** -0.5)

    def idx_scores(j):
        kj = kidx_ref[j]
        sc = jnp.zeros((tk, tq), F32)
        for h in range(IDX_HEADS):
            d = _dot(kj, qi[h * IDX_DIM:(h + 1) * IDX_DIM, :])
            sc = sc + wi[h:h + 1, :] * jnp.maximum(d, 0.0)
        return sc

    def store_keys(j, sc):
        b = pltpu.bitcast(sc, I32)
        key = b ^ ((b >> 31) & 0x7FFFFFFF)
        tile(ihi, j)[...] = (key >> 16).astype(i16)
        tile(ilo, j)[...] = ((key & 0xFFFF) - 32768).astype(i16)

    def idx_tile(j, _):
        store_keys(j, idx_scores(j))
        return 0

    lax.fori_loop(0, i, idx_tile, 0)
    store_keys(i, jnp.where(causal, idx_scores(i), NEG))

    @pl.when(((i + 1) * tq) % kcat_ref.shape[1] != 0)
    def _():
        store_keys(i + 1, jnp.full((tk, tq), NEG, F32))

    def count(ref, pred):
        def body(j, c):
            hit = jnp.where(pred(tile(ref, j)[...]), jnp.ones((), i16), jnp.zeros((), i16))
            parts = [hit[k * 16:(k + 1) * 16] for k in range(tk // 16)]
            while len(parts) > 1:
                parts = [a + b for a, b in zip(parts[0::2], parts[1::2])]
            return c + parts[0]
        c16 = lax.fori_loop(0, i + 1, body, jnp.zeros((16, tq), i16))
        return jnp.sum(c16.astype(I32), axis=0, keepdims=True)

    def search(ref, need, n_all):
        def bit_step(b, st):
            thr, c_ge, c_gt = st
            cand = thr + (jnp.int32(1) << (15 - b))
            cand16 = cand.astype(i16)
            cnt = count(ref, lambda t: t >= cand16)
            ok = cnt >= need
            return jnp.where(ok, cand, thr), jnp.where(ok, cnt, c_ge), jnp.where(ok, c_gt, cnt)
        init = (jnp.full((1, tq), -32768, I32), n_all, jnp.zeros((1, tq), I32))
        return lax.fori_loop(0, 16, bit_step, init)

    thr_hi, n_ge_hi, n_gt_hi = search(ihi, top, jnp.full((1, tq), 1, I32) * ((i + 1) * tk))
    thr_hi16 = thr_hi.astype(i16)
    need = top - n_gt_hi
    n_eq_hi = n_ge_hi - n_gt_hi

    def mask_lo(j, _):
        lo = tile(ilo, j)
        lo[...] = jnp.where(tile(ihi, j)[...] == thr_hi16, lo[...], jnp.full((), -32768, i16))
        return 0

    lax.fori_loop(0, i + 1, mask_lo, 0)
    thr_lo, n_ge_lo, _ = search(ilo, need, n_eq_hi)

    n_sel = n_gt_hi + n_ge_lo
    tie_cap[...] = jnp.full((1, tq), ihi.shape[0], I32)
    excess = n_sel - top
    max_excess = jnp.max(excess)
    n_bits = (ihi.shape[0] - 1).bit_length()

    def eq_tile(j):
        return ((tile(ihi, j)[...].astype(I32) == thr_hi)
                & (tile(ilo, j)[...].astype(I32) == thr_lo))

    @pl.when((max_excess > 0) & (max_excess <= n_bits))
    def _():
        def drop(k, cap):
            def body(j, m8):
                idx = j * tk + row
                v = jnp.where(eq_tile(j) & (idx <= cap), idx, -1)
                return jnp.maximum(m8, jnp.max(v.reshape(tk // 8, 8, tq), axis=0))
            m8 = lax.fori_loop(0, i + 1, body, jnp.full((8, tq), -1, I32))
            return jnp.where(excess > k, jnp.max(m8, axis=0, keepdims=True) - 1, cap)

        tie_cap[...] = lax.fori_loop(0, max_excess, drop, tie_cap[...])

    @pl.when(max_excess > n_bits)
    def _():
        def count32(pred):
            def body(j, c):
                return c + jnp.sum(jnp.where(pred(j), 1, 0).reshape(tk // 8, 8, tq), axis=0)
            c8 = lax.fori_loop(0, i + 1, body, jnp.zeros((8, tq), I32))
            return jnp.sum(c8, axis=0, keepdims=True)

        need_eq = count32(eq_tile) - excess

        def bit_step(b, cap):
            cand = cap + (jnp.int32(1) << (n_bits - 1 - b))
            below = count32(lambda j: eq_tile(j) & (j * tk + row < cand))
            return jnp.where(below >= need_eq, cap, cand)

        tie_cap[...] = lax.fori_loop(0, n_bits, bit_step, jnp.zeros((1, tq), I32))

    m_ref[...] = jnp.full(m_ref.shape, NEG, F32)
    l_ref[...] = jnp.zeros(l_ref.shape, F32)
    acc_ref[...] = jnp.zeros(acc_ref.shape, F32)
    cap = tie_cap[...]

    ta = kcat_ref.shape[1]
    a_row = lax.broadcasted_iota(I32, (ta, 1), 0)
    q_pos = i * tq + col

    def att_tile(j, _):
        rows = pl.ds(pl.multiple_of(j * ta, ta), ta)
        hi = ihi[rows, :].astype(I32)
        lo = ilo[rows, :].astype(I32)
        k_pos = j * ta + a_row
        keep = (hi > thr_hi) | ((hi == thr_hi) & ((lo > thr_lo) | ((lo == thr_lo) & (k_pos <= cap))))
        bias = jnp.where(keep & (k_pos <= q_pos), 0.0, NEG)
        s_all = _dot(kcat_ref[j], q_ref[...])
        ps, alphas = [], []
        for h in range(n_h):
            cs = slice(h * tq, (h + 1) * tq)
            s = s_all[:, cs] + bias
            m_old = m_ref[:, cs]
            m_new = jnp.maximum(m_old, jnp.max(s, axis=0, keepdims=True))
            alpha = jnp.exp2(m_old - m_new)
            p = jnp.exp2(s - m_new)
            l_ref[:, cs] = alpha * l_ref[:, cs] + jnp.sum(p, axis=0, keepdims=True)
            m_ref[:, cs] = m_new
            ps.append(p.astype(BF16))
            alphas.append(alpha)
        p_all = jnp.concatenate(ps, axis=1)
        acc_ref[...] = acc_ref[...] * jnp.concatenate(alphas, axis=1) + _dot(ckt_ref[j], p_all)
        return 0

    lax.fori_loop(0, (i * tq) // ta + 1, att_tile, 0)

    outs = []
    for h in range(n_h):
        cs = slice(h * tq, (h + 1) * tq)
        o = acc_ref[:, cs] / l_ref[:, cs]
        outs.append(_dot(wuv_ref[h], o.astype(BF16)))
    mix = jnp.concatenate(outs, axis=0).T
    y = _dot(mix.astype(BF16), wout_ref[...])
    o_ref[...] = h_ref[...] + _rms_rows(y, g_ref[3:4, :])


def _dsa(h, g, q, qi, wi, kcat, ckt, kidx, wuv_t, w_out, seq):
    bsz = q.shape[0]
    d = h.shape[1]
    tq = min(TQ_DSA, seq)
    nq = seq // tq
    top = min(DSA_TOP, seq // 4)
    wd = MLA_HEADS * tq
    per_b = lambda a: pl.BlockSpec((None,) + a.shape[1:], lambda b, i: (b,) + (0,) * (a.ndim - 1),
                                   pipeline_mode=pl.Buffered(1))
    return pl.pallas_call(
        functools.partial(_dsa_kernel, tq=tq, top=top),
        grid=(bsz, nq),
        in_specs=[pl.BlockSpec((tq, d), lambda b, i: (b * nq + i, 0)), _resident(g.shape),
                  pl.BlockSpec((None, None, MLA_QK, wd), lambda b, i: (b, i, 0, 0)),
                  pl.BlockSpec((None, IDX_HEADS * IDX_DIM, tq), lambda b, i: (b, 0, i)),
                  pl.BlockSpec((None, IDX_HEADS, tq), lambda b, i: (b, 0, i)),
                  per_b(kcat), per_b(ckt), per_b(kidx), _resident(wuv_t.shape),
                  _resident(w_out.shape)],
        out_specs=pl.BlockSpec((tq, d), lambda b, i: (b * nq + i, 0)),
        out_shape=jax.ShapeDtypeStruct(h.shape, F32),
        scratch_shapes=[pltpu.VMEM((seq, tq), jnp.int16),
                        pltpu.VMEM((seq, tq), jnp.int16),
                        pltpu.VMEM((MLA_LATENT, wd), F32),
                        pltpu.VMEM((1, wd), F32),
                        pltpu.VMEM((1, wd), F32),
                        pltpu.VMEM((1, tq), I32)],
        compiler_params=_cparams(("parallel", "arbitrary")),
        name="dsa",
    )(h, g, q, qi, wi, kcat, ckt, kidx, wuv_t, w_out)


def _rope_tables(seq, dim):
    half = dim // 2
    inv = ROPE_THETA ** (-jnp.arange(half, dtype=F32) / half)
    ang = inv[:, None] * jnp.arange(seq, dtype=F32)[None, :]
    return jnp.cos(ang), jnp.sin(ang)


def _even_mixer(h, g, bsz, seq, w_in, w_out, s5, pe_k, pe_v, wk1, wk2, wv1, wv2, tabs):
    cos32, sin32 = tabs[64]
    hd, hk = NSA_HEAD_DIM, NSA_KV_HEADS
    o = 0
    cols = {}
    for name, size in (("u", S5_WIDTH), ("q", NSA_Q), ("kc", NSA_KV), ("vc", NSA_KV), ("ks", NSA_KV),
                       ("vs", NSA_KV), ("kw", NSA_KV), ("vw", NSA_KV), ("gt", 3 * NSA_HEADS)):
        cols[name] = w_in[:, o:o + size]
        o += size
    wu = cols["u"].astype(BF16)
    wt = jnp.concatenate([cols[n] for n in ("q", "kc", "ks", "kw", "vc", "vs", "vw", "gt")], axis=1).T
    wt = jnp.pad(wt, ((0, (-wt.shape[0]) % 16), (0, 0))).astype(BF16)
    u, q_t, k_t, v_t, gt_t = _proj_even(h, g, wu, wt, cos32, sin32, bsz, seq)

    a_re, a_im, b_re, b_im, c_re, c_im, log_dt, d_skip, w_glu = s5
    mats = _s5_matrices(a_re, a_im, b_re, b_im, c_re, c_im, log_dt, S5_CHUNK)
    ys = _s5_scan(u, mats, bsz, seq)

    k5 = k_t.reshape(bsz, 3, hk, hd, seq)
    v5 = v_t.reshape(bsz, 3, hk, hd, seq)
    nb = seq // CMP_STRIDE
    half_blk = lambda a: (a.reshape(bsz, hk, hd, nb, CMP_STRIDE).transpose(0, 1, 3, 4, 2)
                          .reshape(bsz, hk, nb, CMP_STRIDE * hd))
    pe2 = lambda pe: pe.reshape(2, CMP_STRIDE * hd)
    w1s = lambda w: w.reshape(2, CMP_STRIDE * hd, hd).astype(BF16)
    kcmp = _compress(half_blk(k5[:, 0]), pe2(pe_k), w1s(wk1), wk2.astype(BF16))
    vcmp = _compress(half_blk(v5[:, 0]), pe2(pe_v), w1s(wv1), wv2.astype(BF16))
    vcmp_t = vcmp.transpose(0, 1, 3, 2)
    tk = min(TK_NSA, seq)
    ks = k5[:, 1].reshape(bsz, hk, hd, seq // tk, tk).transpose(0, 1, 3, 4, 2)
    vs_t = v5[:, 1].reshape(bsz, hk, hd, seq // tk, tk).transpose(0, 1, 3, 2, 4)
    kw = k5[:, 2].reshape(bsz, hk, hd, seq // LANE, LANE).transpose(0, 1, 3, 4, 2)
    vw_t = v5[:, 2].reshape(bsz, hk, hd, seq // LANE, LANE).transpose(0, 1, 3, 2, 4)
    n_sb = seq // SEL_BLOCK
    c_start = jnp.arange(nb) * CMP_STRIDE
    b_start = jnp.arange(n_sb) * SEL_BLOCK
    ov_t = ((c_start[None, :] < b_start[:, None] + SEL_BLOCK)
            & (c_start[None, :] + CMP_LEN > b_start[:, None])
            & (jnp.arange(nb)[None, :] < nb - 1)).astype(BF16)
    gates = gt_t.reshape(bsz, hk, 3 * NSA_GQA, seq)
    b_out = _nsa(q_t, gates, kcmp, vcmp_t, ks, vs_t, kw, vw_t, ov_t, seq)
    return _outproj_even(h, ys, u, b_out, g, d_skip.reshape(1, -1), w_glu.astype(BF16),
                         w_out[:S5_WIDTH].astype(BF16), w_out[S5_WIDTH:].astype(BF16))


def _odd_mixer(h, g, bsz, seq, w_in, kv_norm, w_uv, w_out, tabs):
    cos32, sin32 = tabs[64]
    cos16, sin16 = tabs[32]
    d = h.shape[1]
    sizes = (MLA_HEADS * MLA_LATENT, MLA_HEADS * MLA_ROPE, MLA_LATENT, MLA_ROPE,
             IDX_HEADS * IDX_DIM, IDX_DIM, IDX_HEADS)
    parts = []
    o = 0
    for s in sizes:
        parts.append(w_in[:, o:o + s])
        o += s
    w_ql, w_qr, w_c, w_kr, w_qi, w_ki, w_wi = parts
    wq = jnp.concatenate([w_ql.reshape(d, MLA_HEADS, MLA_LATENT), w_qr.reshape(d, MLA_HEADS, MLA_ROPE)],
                         axis=2).reshape(d, MLA_HEADS * MLA_QK).T.astype(BF16)
    wkv = jnp.concatenate([w_c, w_kr, w_qi, w_ki, w_wi], axis=1).T
    wkv = jnp.pad(wkv, ((0, (-wkv.shape[0]) % 16), (0, 0))).astype(BF16)
    ckv_t, kr_t, qi_t, ki_t, wi_t = _proj_odd_kv(h, g, wkv, kv_norm.reshape(-1, 1), cos32, sin32,
                                                 cos16, sin16, bsz, seq)
    q = _proj_odd_q(h, g, wq, cos16, sin16, bsz, seq)
    tk = min(TQ_DSA, seq)
    ta = min(TA_DSA, seq)
    kcat = jnp.concatenate([ckv_t, kr_t], axis=1).transpose(0, 2, 1).reshape(bsz, seq // ta, ta, MLA_QK)
    ckt = ckv_t.reshape(bsz, MLA_LATENT, seq // ta, ta).transpose(0, 2, 1, 3)
    kidx = ki_t.transpose(0, 2, 1).reshape(bsz, seq // tk, tk, IDX_DIM)
    wuv_t = w_uv.transpose(0, 2, 1).astype(BF16)
    return _dsa(h, g, q, qi_t, wi_t, kcat, ckt, kidx, wuv_t, w_out.astype(BF16), seq)


def kernel(x, p, norm_g, ffn1_w_in, ffn1_w_out, ffn2_w_in, ffn2_w_out, ple_w_gate, ple_w_proj,
           ev_w_in, ev_w_out, s5_a_re, s5_a_im, s5_b_re, s5_b_im, s5_c_re, s5_c_im, s5_log_dt, s5_d,
           s5_w_glu, nsa_pe_k, nsa_pe_v, nsa_wk1, nsa_wk2, nsa_wv1, nsa_wv2,
           od_w_in, od_kv_norm, od_w_uv, od_w_out):
    bsz, seq, d = x.shape
    depth = norm_g.shape[0]
    tabs = {64: _rope_tables(seq, 64), 32: _rope_tables(seq, 32)}
    h = x.reshape(bsz * seq, d)
    for i in range(depth):
        g = norm_g[i]
        j = i // 2
        h = _ffn(h, g, ffn1_w_in[i].astype(BF16), ffn1_w_out[i].astype(BF16), 0)
        if i % 2 == 0:
            s5 = (s5_a_re[j], s5_a_im[j], s5_b_re[j], s5_b_im[j], s5_c_re[j], s5_c_im[j],
                  s5_log_dt[j], s5_d[j], s5_w_glu[j])
            h = _even_mixer(h, g, bsz, seq, ev_w_in[j], ev_w_out[j], s5, nsa_pe_k[j], nsa_pe_v[j],
                            nsa_wk1[j], nsa_wk2[j], nsa_wv1[j], nsa_wv2[j], tabs)
        else:
            h = _odd_mixer(h, g, bsz, seq, od_w_in[j], od_kv_norm[j], od_w_uv[j], od_w_out[j], tabs)
        h = _ffn_ple(h, p[i].reshape(bsz * seq, -1), g, ffn2_w_in[i].astype(BF16),
                     ffn2_w_out[i].astype(BF16), ple_w_gate[i].astype(BF16),
                     ple_w_proj[i].astype(BF16))
    return h.reshape(bsz, seq, d)
```

```python
import functools
import math

import jax
import jax.numpy as jnp
from jax import lax
from jax.experimental import pallas as pl
from jax.experimental.pallas import tpu as pltpu

F32 = jnp.float32
BF16 = jnp.bfloat16
I32 = jnp.int32

ROPE_THETA = 10000.0
EPS = 1e-6
NEG = -1e30
LOG2E = math.log2(math.e)
D_FF = 2816
S5_WIDTH = 512
S5_GROUP = 16
S5_GROUPS = S5_WIDTH // S5_GROUP
S5_STATE = 64
NSA_HEADS = 8
NSA_KV_HEADS = 2
NSA_GQA = NSA_HEADS // NSA_KV_HEADS
NSA_HEAD_DIM = 64
CMP_LEN = 32
CMP_STRIDE = 16
SEL_BLOCK = 64
SEL_TOP = 16
WINDOW = 512
FORCE_BONUS = 1000.0
NSA_Q = NSA_HEADS * NSA_HEAD_DIM
NSA_KV = NSA_KV_HEADS * NSA_HEAD_DIM
MLA_HEADS = 16
MLA_LATENT = 256
MLA_ROPE = 32
MLA_QK = MLA_LATENT + MLA_ROPE
MLA_V_DIM = 64
IDX_HEADS = 8
IDX_DIM = 64
DSA_TOP = 256

V7X_VMEM_BYTES = 64 * 2**20
VMEM_LIMIT = V7X_VMEM_BYTES - 8 * 2**20
LANE = 128
TM_FFN = 1024
TQ_PROJ = 256
TQ_NSA = 128
TK_NSA = 512
TQ_DSA = 256
TA_DSA = 512
S5_CHUNK = 64


def _cparams(sem):
    return pltpu.CompilerParams(dimension_semantics=sem, vmem_limit_bytes=VMEM_LIMIT)


def _resident(shape):
    nd = len(shape)
    return pl.BlockSpec(shape, lambda *_: (0,) * nd, pipeline_mode=pl.Buffered(1))


def _dot(a, b):
    return jnp.dot(a, b, preferred_element_type=F32)


def _dot_nt(a, b):
    return lax.dot_general(a, b, (((1,), (1,)), ((), ())), preferred_element_type=F32)


def _rms_rows(x, g):
    return x * lax.rsqrt(jnp.mean(x * x, axis=-1, keepdims=True) + EPS) * g


def _rope_fmaj(y, cos, sin):
    half = y.shape[1] // 2
    t1 = y[:, :half, :]
    t2 = y[:, half:, :]
    return jnp.concatenate([t1 * cos - t2 * sin, t2 * cos + t1 * sin], axis=1)


def _ffn_tile(x, g_ref, g0, win_ref, wout_ref, n_chunk):
    xn = _rms_rows(x, g_ref[g0:g0 + 1, :]).astype(BF16)
    ck = D_FF // n_chunk
    acc = None
    for c in range(n_chunk):
        a = _dot(xn, win_ref[:, c * ck:(c + 1) * ck])
        u = _dot(xn, win_ref[:, D_FF + c * ck:D_FF + (c + 1) * ck])
        act = (jax.nn.silu(a) * u).astype(BF16)
        y = _dot(act, wout_ref[c * ck:(c + 1) * ck, :])
        acc = y if acc is None else acc + y
    return x + 0.5 * _rms_rows(acc, g_ref[g0 + 1:g0 + 2, :])


def _ffn_kernel(h_ref, g_ref, win_ref, wout_ref, o_ref, *, g0, n_chunk):
    o_ref[...] = _ffn_tile(h_ref[...], g_ref, g0, win_ref, wout_ref, n_chunk)


def _ffn_ple_kernel(h_ref, p_ref, g_ref, win_ref, wout_ref, wg_ref, wp_ref, o_ref, *, n_chunk):
    x = _ffn_tile(h_ref[...], g_ref, 4, win_ref, wout_ref, n_chunk)
    gate = jax.nn.sigmoid(_dot(_rms_rows(x, g_ref[6:7, :]).astype(BF16), wg_ref[...]))
    e = _dot(p_ref[...].astype(BF16), wp_ref[...]) * gate
    o_ref[...] = x + _rms_rows(e, g_ref[7:8, :])


def _ffn_ple(h, p, g, w_in, w_out, w_gate, w_proj):
    t, d = h.shape
    tm = min(TM_FFN, t)
    return pl.pallas_call(
        functools.partial(_ffn_ple_kernel, n_chunk=2),
        grid=(t // tm,),
        in_specs=[pl.BlockSpec((tm, d), lambda i: (i, 0)),
                  pl.BlockSpec((tm, p.shape[1]), lambda i: (i, 0)), _resident(g.shape),
                  _resident(w_in.shape), _resident(w_out.shape), _resident(w_gate.shape),
                  _resident(w_proj.shape)],
        out_specs=pl.BlockSpec((tm, d), lambda i: (i, 0)),
        out_shape=jax.ShapeDtypeStruct((t, d), F32),
        compiler_params=_cparams(("parallel",)),
        name="ffn_ple",
    )(h, p, g, w_in, w_out, w_gate, w_proj)


def _ffn(h, g, w_in, w_out, g0):
    t, d = h.shape
    tm = min(TM_FFN, t)
    return pl.pallas_call(
        functools.partial(_ffn_kernel, g0=g0, n_chunk=2),
        grid=(t // tm,),
        in_specs=[pl.BlockSpec((tm, d), lambda i: (i, 0)), _resident(g.shape),
                  _resident(w_in.shape), _resident(w_out.shape)],
        out_specs=pl.BlockSpec((tm, d), lambda i: (i, 0)),
        out_shape=jax.ShapeDtypeStruct((t, d), F32),
        compiler_params=_cparams(("parallel",)),
        name="ffn",
    )(h, g, w_in, w_out)


def _outproj_even_kernel(h_ref, ys_ref, u_ref, b_ref, g_ref, d_ref, wglu_ref, wa_ref, wb_ref, o_ref):
    y = ys_ref[...] + d_ref[...] * u_ref[...]
    z = jax.nn.gelu(y)
    a = z * jax.nn.sigmoid(_dot(z.astype(BF16), wglu_ref[...]))
    mix = _dot(a.astype(BF16), wa_ref[...]) + _dot(b_ref[...].astype(BF16), wb_ref[...])
    o_ref[...] = h_ref[...] + _rms_rows(mix, g_ref[3:4, :])


def _outproj_even(h, ys, u, b_out, g, d_skip, w_glu, w_a, w_b):
    t, d = h.shape
    tm = min(TM_FFN, t)
    tok = lambda w: pl.BlockSpec((tm, w), lambda i: (i, 0))
    return pl.pallas_call(
        _outproj_even_kernel,
        grid=(t // tm,),
        in_specs=[tok(d), tok(S5_WIDTH), tok(S5_WIDTH), tok(NSA_Q), _resident(g.shape),
                  _resident(d_skip.shape), _resident(w_glu.shape), _resident(w_a.shape),
                  _resident(w_b.shape)],
        out_specs=tok(d),
        out_shape=jax.ShapeDtypeStruct((t, d), F32),
        compiler_params=_cparams(("parallel",)),
        name="outproj_even",
    )(h, ys, u, b_out, g, d_skip, w_glu, w_a, w_b)


def _proj_even_kernel(h_ref, g_ref, wu_ref, wt_ref, cos_ref, sin_ref,
                      u_ref, q_ref, k_ref, v_ref, gt_ref):
    xn = _rms_rows(h_ref[...], g_ref[2:3, :]).astype(BF16)
    u_ref[...] = _dot(xn, wu_ref[...])
    y = _dot_nt(wt_ref[...], xn)
    tq = y.shape[1]
    cos = cos_ref[...]
    sin = sin_ref[...]
    hd = NSA_HEAD_DIM
    q = _rope_fmaj(y[0:NSA_Q].reshape(NSA_HEADS, hd, tq), cos, sin)
    q_ref[...] = (q * (hd ** -0.5 * LOG2E)).reshape(NSA_Q, tq)
    k0 = NSA_Q
    nk = 3 * NSA_KV
    k = _rope_fmaj(y[k0:k0 + nk].reshape(3 * NSA_KV_HEADS, hd, tq), cos, sin)
    k_ref[...] = k.reshape(nk, tq).astype(k_ref.dtype)
    v0 = k0 + nk
    v_ref[...] = y[v0:v0 + nk].astype(v_ref.dtype)
    g0 = v0 + nk
    gt_ref[...] = jax.nn.sigmoid(y[g0:g0 + 3 * NSA_HEADS])


def _proj_even(h, g, wu, wt, cos, sin, bsz, seq):
    d = h.shape[1]
    tq = min(TQ_PROJ, seq)
    nq = seq // tq
    nk = 3 * NSA_KV
    fm = lambda rows: pl.BlockSpec((None, rows, tq), lambda b, i: (b, 0, i))
    return pl.pallas_call(
        _proj_even_kernel,
        grid=(bsz, nq),
        in_specs=[pl.BlockSpec((tq, d), lambda b, i: (b * nq + i, 0)), _resident(g.shape),
                  _resident(wu.shape), _resident(wt.shape),
                  pl.BlockSpec((cos.shape[0], tq), lambda b, i: (0, i)),
                  pl.BlockSpec((sin.shape[0], tq), lambda b, i: (0, i))],
        out_specs=[pl.BlockSpec((tq, S5_WIDTH), lambda b, i: (b * nq + i, 0)),
                   fm(NSA_Q), fm(nk), fm(nk), fm(3 * NSA_HEADS)],
        out_shape=[jax.ShapeDtypeStruct((bsz * seq, S5_WIDTH), F32),
                   jax.ShapeDtypeStruct((bsz, NSA_Q, seq), F32),
                   jax.ShapeDtypeStruct((bsz, nk, seq), BF16),
                   jax.ShapeDtypeStruct((bsz, nk, seq), BF16),
                   jax.ShapeDtypeStruct((bsz, 3 * NSA_HEADS, seq), F32)],
        compiler_params=_cparams(("parallel", "parallel")),
        name="proj_even",
    )(h, g, wu, wt, cos, sin)


def _proj_odd_kv_kernel(h_ref, g_ref, wt_ref, kvn_ref, cos32_ref, sin32_ref, cos16_ref, sin16_ref,
                        ckv_ref, kr_ref, qi_ref, ki_ref, wi_ref):
    xn = _rms_rows(h_ref[...], g_ref[2:3, :]).astype(BF16)
    y = _dot_nt(wt_ref[...], xn)
    tq = y.shape[1]
    c = y[0:MLA_LATENT]
    c = c * lax.rsqrt(jnp.mean(c * c, axis=0, keepdims=True) + EPS) * kvn_ref[...]
    ckv_ref[...] = c.astype(ckv_ref.dtype)
    r0 = MLA_LATENT
    kr = _rope_fmaj(y[r0:r0 + MLA_ROPE].reshape(1, MLA_ROPE, tq), cos16_ref[...], sin16_ref[...])
    kr_ref[...] = kr.reshape(MLA_ROPE, tq).astype(kr_ref.dtype)
    q0 = r0 + MLA_ROPE
    nqi = IDX_HEADS * IDX_DIM
    qi = _rope_fmaj(y[q0:q0 + nqi].reshape(IDX_HEADS, IDX_DIM, tq), cos32_ref[...], sin32_ref[...])
    qi_ref[...] = qi.reshape(nqi, tq).astype(qi_ref.dtype)
    k0 = q0 + nqi
    ki = _rope_fmaj(y[k0:k0 + IDX_DIM].reshape(1, IDX_DIM, tq), cos32_ref[...], sin32_ref[...])
    ki_ref[...] = ki.reshape(IDX_DIM, tq).astype(ki_ref.dtype)
    w0 = k0 + IDX_DIM
    wi_ref[...] = y[w0:w0 + IDX_HEADS]


def _proj_odd_kv(h, g, wt, kvn, cos32, sin32, cos16, sin16, bsz, seq):
    d = h.shape[1]
    tq = min(TQ_PROJ, seq)
    nq = seq // tq
    fm = lambda rows: pl.BlockSpec((None, rows, tq), lambda b, i: (b, 0, i))
    tab = lambda a: pl.BlockSpec((a.shape[0], tq), lambda b, i: (0, i))
    nqi = IDX_HEADS * IDX_DIM
    return pl.pallas_call(
        _proj_odd_kv_kernel,
        grid=(bsz, nq),
        in_specs=[pl.BlockSpec((tq, d), lambda b, i: (b * nq + i, 0)), _resident(g.shape),
                  _resident(wt.shape), _resident(kvn.shape),
                  tab(cos32), tab(sin32), tab(cos16), tab(sin16)],
        out_specs=[fm(MLA_LATENT), fm(MLA_ROPE), fm(nqi), fm(IDX_DIM), fm(IDX_HEADS)],
        out_shape=[jax.ShapeDtypeStruct((bsz, MLA_LATENT, seq), BF16),
                   jax.ShapeDtypeStruct((bsz, MLA_ROPE, seq), BF16),
                   jax.ShapeDtypeStruct((bsz, nqi, seq), BF16),
                   jax.ShapeDtypeStruct((bsz, IDX_DIM, seq), BF16),
                   jax.ShapeDtypeStruct((bsz, IDX_HEADS, seq), F32)],
        compiler_params=_cparams(("parallel", "parallel")),
        name="proj_odd_kv",
    )(h, g, wt, kvn, cos32, sin32, cos16, sin16)


def _proj_odd_q_kernel(h_ref, g_ref, wt_ref, cos16_ref, sin16_ref, q_ref):
    xn = _rms_rows(h_ref[...], g_ref[2:3, :]).astype(BF16)
    tq = xn.shape[0]
    cos = cos16_ref[...]
    sin = sin16_ref[...]
    for hh in range(MLA_HEADS):
        y = _dot_nt(wt_ref[hh * MLA_QK:(hh + 1) * MLA_QK, :], xn)
        r = _rope_fmaj(y[MLA_LATENT:].reshape(1, MLA_ROPE, tq), cos, sin)
        q = jnp.concatenate([y[:MLA_LATENT], r.reshape(MLA_ROPE, tq)], axis=0)
        q_ref[:, hh * tq:(hh + 1) * tq] = (q * (MLA_QK ** -0.5 * LOG2E)).astype(q_ref.dtype)


def _proj_odd_q(h, g, wt, cos16, sin16, bsz, seq):
    d = h.shape[1]
    tq = min(TQ_DSA, seq)
    nq = seq // tq
    return pl.pallas_call(
        _proj_odd_q_kernel,
        grid=(bsz, nq),
        in_specs=[pl.BlockSpec((tq, d), lambda b, i: (b * nq + i, 0)), _resident(g.shape),
                  _resident(wt.shape),
                  pl.BlockSpec((cos16.shape[0], tq), lambda b, i: (0, i)),
                  pl.BlockSpec((sin16.shape[0], tq), lambda b, i: (0, i))],
        out_specs=pl.BlockSpec((None, None, MLA_QK, MLA_HEADS * tq), lambda b, i: (b, i, 0, 0)),
        out_shape=jax.ShapeDtypeStruct((bsz, nq, MLA_QK, MLA_HEADS * tq), BF16),
        compiler_params=_cparams(("parallel", "parallel")),
        name="proj_odd_q",
    )(h, g, wt, cos16, sin16)


def _s5_state_kernel(u_ref, bc_ref, s_ref):
    s_ref[...] = _dot(u_ref[...], bc_ref[...])


def _s5_scan_kernel(sr_ref, si_ref, lr_ref, li_ref, xr_ref, xi_ref):
    n_chunk = sr_ref.shape[0]
    lr = lr_ref[...]
    li = li_ref[...]

    def body(c, carry):
        xr, xi = carry
        xr_ref[c] = xr
        xi_ref[c] = xi
        return (lr * xr - li * xi + sr_ref[c], lr * xi + li * xr + si_ref[c])

    zero = jnp.zeros(sr_ref.shape[1:], F32)
    lax.fori_loop(0, n_chunk, body, (zero, zero))


def _s5_out_kernel(u_ref, x_ref, m_ref, cc_ref, y_ref):
    x = x_ref[...]
    hi = x.astype(BF16)
    lo = (x - hi.astype(F32)).astype(BF16)
    cc = cc_ref[...]
    y_ref[...] = _dot(u_ref[...], m_ref[...]) + _dot(hi, cc) + _dot(lo, cc)


def _s5_matrices(a_re, a_im, b_re, b_im, c_re, c_im, log_dt, tc):
    hp = lax.Precision.HIGHEST
    dt = jnp.exp(log_dt)[:, None]
    lam = lax.complex(a_re, a_im)
    lam_dt = lam * dt
    lam_bar = jnp.exp(lam_dt)
    b_bar = ((lam_bar - 1.0) / lam)[..., None] * lax.complex(b_re, b_im)
    k = jnp.arange(tc + 1, dtype=F32)[:, None, None]
    pw = jnp.exp(lam_dt[None] * k)
    pr, pi = jnp.real(pw), jnp.imag(pw)
    bbr, bbi = jnp.real(b_bar), jnp.imag(b_bar)
    cpr = c_re[None] * pr[:, :, None, :] - c_im[None] * pi[:, :, None, :]
    cpi = c_re[None] * pi[:, :, None, :] + c_im[None] * pr[:, :, None, :]
    kk = (jnp.einsum('kgpn,gnq->kgpq', cpr[:tc], bbr, precision=hp)
          - jnp.einsum('kgpn,gnq->kgpq', cpi[:tc], bbi, precision=hp))
    n_g, n_p = a_re.shape[0], b_re.shape[2]
    w = jnp.pad(kk.transpose(1, 3, 0, 2).astype(BF16), ((0, 0), (0, 0), (tc - 1, 0), (0, 0)))
    rows = jnp.tile(w, (1, 1, tc + 1, 1))[:, :, :2 * tc * tc].reshape(n_g, n_p, tc, 2 * tc, n_p)
    m = rows[:, :, ::-1, :tc].reshape(n_g, n_p * tc, tc * n_p)
    rev = pw[tc - 1 - jnp.arange(tc)]
    bc = rev[:, :, :, None] * b_bar[None]
    bc = bc.transpose(1, 3, 0, 2).reshape(n_g, n_p * tc, -1)
    bc = jnp.concatenate([jnp.real(bc), jnp.imag(bc)], axis=-1)
    mr = cpr[1:tc + 1].transpose(1, 3, 0, 2).reshape(n_g, -1, tc * n_p)
    mi = cpi[1:tc + 1].transpose(1, 3, 0, 2).reshape(n_g, -1, tc * n_p)
    cc = jnp.concatenate([mr, -mi], axis=1)
    ltc = pw[tc].reshape(1, -1)
    return m, bc.astype(BF16), cc.astype(BF16), jnp.real(ltc), jnp.imag(ltc)


def _s5_scan(u, mats, bsz, seq):
    m, bc, cc, lr, li = mats
    tc = S5_CHUNK
    n_c = seq // tc
    n_g, n_p, n_s = S5_GROUPS, S5_GROUP, S5_STATE
    rows = bsz * n_c
    kd = tc * n_p
    ug = (u.reshape(bsz, n_c, tc, n_g, n_p).transpose(3, 0, 1, 4, 2)
          .reshape(n_g, rows, kd).astype(BF16))
    grp = lambda a, b: pl.BlockSpec((None, a, b), lambda gi: (gi, 0, 0))
    s = pl.pallas_call(
        _s5_state_kernel,
        grid=(n_g,),
        in_specs=[grp(rows, kd), grp(kd, 2 * n_s)],
        out_specs=grp(rows, 2 * n_s),
        out_shape=jax.ShapeDtypeStruct((n_g, rows, 2 * n_s), F32),
        compiler_params=_cparams(("parallel",)),
        name="s5_state",
    )(ug, bc)
    s5 = s.reshape(n_g, bsz, n_c, 2, n_s).transpose(3, 2, 1, 0, 4).reshape(2, n_c, bsz, n_g * n_s)
    full = lambda shp: pl.BlockSpec(shp, lambda: (0,) * len(shp))
    xr, xi = pl.pallas_call(
        _s5_scan_kernel,
        in_specs=[full(s5.shape[1:]), full(s5.shape[1:]), full(lr.shape), full(li.shape)],
        out_specs=[full(s5.shape[1:]), full(s5.shape[1:])],
        out_shape=[jax.ShapeDtypeStruct(s5.shape[1:], F32)] * 2,
        compiler_params=pltpu.CompilerParams(vmem_limit_bytes=VMEM_LIMIT),
        name="s5_scan",
    )(s5[0], s5[1], lr, li)
    x = jnp.stack([xr, xi]).reshape(2, n_c, bsz, n_g, n_s).transpose(3, 2, 1, 0, 4)
    x = x.reshape(n_g, rows, 2 * n_s)
    y = pl.pallas_call(
        _s5_out_kernel,
        grid=(n_g,),
        in_specs=[grp(rows, kd), grp(rows, 2 * n_s), grp(kd, kd), grp(2 * n_s, kd)],
        out_specs=grp(rows, kd),
        out_shape=jax.ShapeDtypeStruct((n_g, rows, kd), F32),
        compiler_params=_cparams(("parallel",)),
        name="s5_out",
    )(ug, x, m, cc)
    return (y.reshape(n_g, bsz, n_c, tc, n_p).transpose(1, 2, 3, 0, 4)
            .reshape(bsz * seq, n_g * n_p))


def _compress_kernel(x_ref, pe_ref, w1_ref, w2_ref, o_ref):
    x = x_ref[...].astype(F32)
    nb = x.shape[0]
    a = _dot((x + pe_ref[0:1, :]).astype(BF16), w1_ref[0])
    b = _dot((x + pe_ref[1:2, :]).astype(BF16), w1_ref[1])
    pre = a + pltpu.roll(b, nb - 1, 0)
    o_ref[...] = _dot(jax.nn.gelu(pre).astype(BF16), w2_ref[...]).astype(o_ref.dtype)


def _compress(x, pe, w1, w2):
    bsz, hk, nb, kd = x.shape
    hd = w2.shape[1]
    return pl.pallas_call(
        _compress_kernel,
        grid=(bsz, hk),
        in_specs=[pl.BlockSpec((None, None, nb, kd), lambda b, h: (b, h, 0, 0)),
                  _resident(pe.shape), _resident(w1.shape), _resident(w2.shape)],
        out_specs=pl.BlockSpec((None, None, nb, hd), lambda b, h: (b, h, 0, 0)),
        out_shape=jax.ShapeDtypeStruct((bsz, hk, nb, hd), BF16),
        compiler_params=_cparams(("parallel", "parallel")),
        name="nsa_compress",
    )(x, pe, w1, w2)


def _nsa_kernel(q_ref, g_ref, kc_ref, vct_ref, ks_ref, vst_ref, kw_ref, vwt_ref, ov_ref,
                o_ref, sel_ref, zero_ref, sa_ref, sb_ref, *, tq, tk, top_n):
    gq, hd = NSA_GQA, NSA_HEAD_DIM
    n_sb = sel_ref.shape[0]
    nb = kc_ref.shape[0]
    wd = gq * tq
    i = pl.program_id(2)
    s0 = i * tq
    q = q_ref[...]
    qt = jnp.concatenate([q[g * hd:(g + 1) * hd, :] for g in range(gq)], axis=1).astype(BF16)
    t1 = s0 + lax.broadcasted_iota(I32, (1, tq), 1)
    t4 = s0 + (lax.broadcasted_iota(I32, (1, wd), 1) & (tq - 1))

    sc = _dot(kc_ref[...], qt)
    c_last = lax.broadcasted_iota(I32, (nb, 1), 0) * CMP_STRIDE + (CMP_LEN - 1)
    bias_c = jnp.where(c_last <= t1, 0.0, NEG)
    sm = sc + jnp.concatenate([bias_c] * gq, axis=1)
    e = jnp.exp2(sm - jnp.max(sm, axis=0, keepdims=True))
    p_c = e * jnp.where(t4 >= CMP_LEN - 1, 1.0 / jnp.sum(e, axis=0, keepdims=True), 0.0)
    o_c = _dot(vct_ref[...], p_c.astype(BF16))
    psum = p_c[:, 0:tq]
    for g in range(1, gq):
        psum = psum + p_c[:, g * tq:(g + 1) * tq]
    hi = psum.astype(BF16)
    lo = (psum - hi.astype(F32)).astype(BF16)
    imp = _dot(ov_ref[...], hi) + _dot(ov_ref[...], lo)
    blk = lax.broadcasted_iota(I32, (n_sb, 1), 0)
    cur = t1 >> int(math.log2(SEL_BLOCK))
    forced = (blk == 0) | (blk == cur) | (blk == cur - 1)
    imp = imp + jnp.where(forced, FORCE_BONUS, 0.0)
    imp = jnp.where(blk * SEL_BLOCK <= t1, imp, NEG)

    rowf = lax.broadcasted_iota(I32, (n_sb, tq), 0).astype(F32)
    sel = jnp.zeros((n_sb, tq), F32)
    x = imp
    for _ in range(top_n):
        mx = jnp.max(x, axis=0, keepdims=True)
        first = jnp.min(jnp.where(x == mx, rowf, float(n_sb)), axis=0, keepdims=True)
        hit = rowf == first
        sel = jnp.where(hit, 1.0, sel)
        x = jnp.where(hit, -jnp.inf, x)
    sel_ref[...] = sel

    zero_ref[...] = jnp.zeros(zero_ref.shape, F32)
    init = (jnp.full((1, wd), NEG, F32), jnp.zeros((1, wd), F32), zero_ref[...])

    bpt = tk // SEL_BLOCK
    key_row = lax.broadcasted_iota(I32, (tk, 1), 0)

    last_tile = ks_ref.shape[0] - 1

    def scores(j):
        return _dot(ks_ref[jnp.minimum(j, last_tile)], qt)

    def consume(j, s_ref, carry):
        m, l, acc, pv = carry
        jc = jnp.minimum(j, last_tile)
        rows = [jnp.broadcast_to(sel_ref[pl.ds(jc * bpt + b, 1), :], (SEL_BLOCK, tq))
                for b in range(bpt)]
        keep = (jnp.concatenate(rows, axis=0) > 0.5) & (j * tk + key_row <= t1)
        bias = jnp.where(keep, 0.0, NEG)
        s = s_ref[...] + jnp.concatenate([bias] * gq, axis=1)
        m_new = jnp.maximum(m, jnp.max(s, axis=0, keepdims=True))
        alpha = jnp.exp2(m - m_new)
        p = jnp.exp2(s - m_new)
        l = alpha * l + jnp.sum(p, axis=0, keepdims=True)
        return m_new, l, alpha * (acc + pv), _dot(vst_ref[jc], p.astype(BF16))

    def tile_pair(k, carry):
        j = 2 * k
        sb_ref[...] = scores(j + 1)
        carry = consume(j, sa_ref, carry)
        sa_ref[...] = scores(j + 2)
        return consume(j + 1, sb_ref, carry)

    sa_ref[...] = scores(0)
    _, l_s, a_s, pv_s = lax.fori_loop(0, (s0 // tk + 2) // 2, tile_pair, init + (zero_ref[...],))
    o_s = (a_s + pv_s) / l_s

    n_wt = (WINDOW + tq) // LANE
    j_lo = jnp.clip(i * (tq // LANE) - WINDOW // LANE, 0, kw_ref.shape[0] - n_wt)
    kwin = jnp.concatenate([kw_ref[j_lo + w] for w in range(n_wt)], axis=0)
    vwin = jnp.concatenate([vwt_ref[j_lo + w] for w in range(n_wt)], axis=1)
    sw = _dot(kwin, qt)
    diff = t1 - (j_lo * LANE + lax.broadcasted_iota(I32, (n_wt * LANE, 1), 0))
    sw = sw + jnp.concatenate([jnp.where((diff >= 0) & (diff < WINDOW), 0.0, NEG)] * gq, axis=1)
    pw = jnp.exp2(sw - jnp.max(sw, axis=0, keepdims=True))
    o_w = _dot(vwin, pw.astype(BF16)) / jnp.sum(pw, axis=0, keepdims=True)

    gt = g_ref[...]
    outs = []
    for g in range(gq):
        cs = slice(g * tq, (g + 1) * tq)
        outs.append(gt[3 * g:3 * g + 1, :] * o_c[:, cs] + gt[3 * g + 1:3 * g + 2, :] * o_s[:, cs]
                    + gt[3 * g + 2:3 * g + 3, :] * o_w[:, cs])
    o_ref[...] = jnp.concatenate(outs, axis=0).T


def _nsa(q, gates, kcmp, vcmp_t, ks, vs_t, kw, vw_t, ov_t, seq):
    bsz = q.shape[0]
    tq = min(TQ_NSA, seq)
    tk = min(TK_NSA, seq)
    n_sb = seq // SEL_BLOCK
    gq, hd = NSA_GQA, NSA_HEAD_DIM
    per_head = lambda a: pl.BlockSpec((None, None) + a.shape[2:],
                                      lambda b, h, i: (b, h) + (0,) * (a.ndim - 2),
                                      pipeline_mode=pl.Buffered(1))
    return pl.pallas_call(
        functools.partial(_nsa_kernel, tq=tq, tk=tk, top_n=min(SEL_TOP, n_sb)),
        grid=(bsz, NSA_KV_HEADS, seq // tq),
        in_specs=[pl.BlockSpec((None, gq * hd, tq), lambda b, h, i: (b, h, i)),
                  pl.BlockSpec((None, None, 3 * gq, tq), lambda b, h, i: (b, h, 0, i)),
                  per_head(kcmp), per_head(vcmp_t), per_head(ks), per_head(vs_t),
                  per_head(kw), per_head(vw_t), _resident(ov_t.shape)],
        out_specs=pl.BlockSpec((tq, gq * hd), lambda b, h, i: (b * (seq // tq) + i, h)),
        out_shape=jax.ShapeDtypeStruct((bsz * seq, NSA_Q), F32),
        scratch_shapes=[pltpu.VMEM((n_sb, tq), F32), pltpu.VMEM((hd, gq * tq), F32),
                        pltpu.VMEM((tk, gq * tq), F32), pltpu.VMEM((tk, gq * tq), F32)],
        compiler_params=_cparams(("parallel", "parallel", "arbitrary")),
        name="nsa",
    )(q, gates, kcmp, vcmp_t, ks, vs_t, kw, vw_t, ov_t)


def _dsa_kernel(h_ref, g_ref, q_ref, qi_ref, wi_ref, kcat_ref, ckt_ref, kidx_ref, wuv_ref, wout_ref,
                o_ref, ihi, ilo, acc_ref, m_ref, l_ref, tie_cap, *, tq, top):
    tk = tq
    n_h = MLA_HEADS
    i16 = jnp.int16
    i = pl.program_id(1)
    row = lax.broadcasted_iota(I32, (tk, 1), 0)
    col = lax.broadcasted_iota(I32, (1, tq), 1)
    causal = row <= col

    def tile(ref, j):
        return ref.at[pl.ds(pl.multiple_of(j * tk, tk), tk), :]

    qi = qi_ref[...]
    wi = wi_ref[...] * (IDX_HEADS ** -0.5) * (IDX_DIM ** -0.5)

    def idx_scores(j):
        kj = kidx_ref[j]
        sc = jnp.zeros((tk, tq), F32)
        for h in range(IDX_HEADS):
            d = _dot(kj, qi[h * IDX_DIM:(h + 1) * IDX_DIM, :])
            sc = sc + wi[h:h + 1, :] * jnp.maximum(d, 0.0)
        return sc

    def store_keys(j, sc):
        b = pltpu.bitcast(sc, I32)
        key = b ^ ((b >> 31) & 0x7FFFFFFF)
        tile(ihi, j)[...] = (key >> 16).astype(i16)
        tile(ilo, j)[...] = ((key & 0xFFFF) - 32768).astype(i16)

    def idx_tile(j, _):
        store_keys(j, idx_scores(j))
        return 0

    lax.fori_loop(0, i, idx_tile, 0)
    store_keys(i, jnp.where(causal, idx_scores(i), NEG))

    @pl.when(((i + 1) * tq) % kcat_ref.shape[1] != 0)
    def _():
        store_keys(i + 1, jnp.full((tk, tq), NEG, F32))

    ts = kcat_ref.shape[1]
    n_st = (i * tq) // ts + 1
    s_row = lax.broadcasted_iota(I32, (ts, 1), 0)

    def wtile(ref, j):
        return ref.at[pl.ds(pl.multiple_of(j * ts, ts), ts), :]

    def count(ref, pred):
        def body(j, c):
            hit = jnp.where(pred(wtile(ref, j)[...]), jnp.ones((), i16), jnp.zeros((), i16))
            parts = [hit[k * 16:(k + 1) * 16] for k in range(ts // 16)]
            while len(parts) > 1:
                parts = [a + b for a, b in zip(parts[0::2], parts[1::2])]
            return c + parts[0]
        c16 = lax.fori_loop(0, n_st, body, jnp.zeros((16, tq), i16))
        return jnp.sum(c16.astype(I32), axis=0, keepdims=True)

    def search(ref, need, n_all):
        def bit_step(b, st):
            thr, c_ge, c_gt = st
            cand = thr + (jnp.int32(1) << (15 - b))
            cand16 = cand.astype(i16)
            cnt = count(ref, lambda t: t >= cand16)
            ok = cnt >= need
            return jnp.where(ok, cand, thr), jnp.where(ok, cnt, c_ge), jnp.where(ok, c_gt, cnt)
        init = (jnp.full((1, tq), -32768, I32), n_all, jnp.zeros((1, tq), I32))
        return lax.fori_loop(0, 16, bit_step, init)

    thr_hi, n_ge_hi, n_gt_hi = search(ihi, top, jnp.full((1, tq), 1, I32) * (n_st * ts))
    thr_hi16 = thr_hi.astype(i16)
    need = top - n_gt_hi
    n_eq_hi = n_ge_hi - n_gt_hi

    def mask_lo(j, _):
        lo = wtile(ilo, j)
        lo[...] = jnp.where(wtile(ihi, j)[...] == thr_hi16, lo[...], jnp.full((), -32768, i16))
        return 0

    lax.fori_loop(0, n_st, mask_lo, 0)
    thr_lo, n_ge_lo, _ = search(ilo, need, n_eq_hi)

    n_sel = n_gt_hi + n_ge_lo
    tie_cap[...] = jnp.full((1, tq), ihi.shape[0], I32)
    excess = n_sel - top
    max_excess = jnp.max(excess)
    n_bits = (ihi.shape[0] - 1).bit_length()

    def eq_tile(j):
        return ((wtile(ihi, j)[...].astype(I32) == thr_hi)
                & (wtile(ilo, j)[...].astype(I32) == thr_lo))

    @pl.when((max_excess > 0) & (max_excess <= n_bits))
    def _():
        def drop(k, cap):
            def body(j, m8):
                idx = j * ts + s_row
                v = jnp.where(eq_tile(j) & (idx <= cap), idx, -1)
                return jnp.maximum(m8, jnp.max(v.reshape(ts // 8, 8, tq), axis=0))
            m8 = lax.fori_loop(0, n_st, body, jnp.full((8, tq), -1, I32))
            return jnp.where(excess > k, jnp.max(m8, axis=0, keepdims=True) - 1, cap)

        tie_cap[...] = lax.fori_loop(0, max_excess, drop, tie_cap[...])

    @pl.when(max_excess > n_bits)
    def _():
        def count32(pred):
            def body(j, c):
                return c + jnp.sum(jnp.where(pred(j), 1, 0).reshape(ts // 8, 8, tq), axis=0)
            c8 = lax.fori_loop(0, n_st, body, jnp.zeros((8, tq), I32))
            return jnp.sum(c8, axis=0, keepdims=True)

        need_eq = count32(eq_tile) - excess

        def bit_step(b, cap):
            cand = cap + (jnp.int32(1) << (n_bits - 1 - b))
            below = count32(lambda j: eq_tile(j) & (j * ts + s_row < cand))
            return jnp.where(below >= need_eq, cap, cand)

        tie_cap[...] = lax.fori_loop(0, n_bits, bit_step, jnp.zeros((1, tq), I32))

    m_ref[...] = jnp.full(m_ref.shape, NEG, F32)
    l_ref[...] = jnp.zeros(l_ref.shape, F32)
    acc_ref[...] = jnp.zeros(acc_ref.shape, F32)
    cap = tie_cap[...]

    ta = kcat_ref.shape[1]
    a_row = lax.broadcasted_iota(I32, (ta, 1), 0)
    q_pos = i * tq + col

    def att_tile(j, _):
        rows = pl.ds(pl.multiple_of(j * ta, ta), ta)
        hi = ihi[rows, :].astype(I32)
        lo = ilo[rows, :].astype(I32)
        k_pos = j * ta + a_row
        keep = (hi > thr_hi) | ((hi == thr_hi) & ((lo > thr_lo) | ((lo == thr_lo) & (k_pos <= cap))))
        bias = jnp.where(keep & (k_pos <= q_pos), 0.0, NEG)
        s_all = _dot(kcat_ref[j], q_ref[...])
        ps, alphas = [], []
        for h in range(n_h):
            cs = slice(h * tq, (h + 1) * tq)
            s = s_all[:, cs] + bias
            m_old = m_ref[:, cs]
            m_new = jnp.maximum(m_old, jnp.max(s, axis=0, keepdims=True))
            alpha = jnp.exp2(m_old - m_new)
            p = jnp.exp2(s - m_new)
            l_ref[:, cs] = alpha * l_ref[:, cs] + jnp.sum(p, axis=0, keepdims=True)
            m_ref[:, cs] = m_new
            ps.append(p.astype(BF16))
            alphas.append(alpha)
        p_all = jnp.concatenate(ps, axis=1)
        acc_ref[...] = acc_ref[...] * jnp.concatenate(alphas, axis=1) + _dot(ckt_ref[j], p_all)
        return 0

    lax.fori_loop(0, (i * tq) // ta + 1, att_tile, 0)

    outs = []
    for h in range(n_h):
        cs = slice(h * tq, (h + 1) * tq)
        o = acc_ref[:, cs] / l_ref[:, cs]
        outs.append(_dot(wuv_ref[h], o.astype(BF16)))
    mix = jnp.concatenate(outs, axis=0).T
    y = _dot(mix.astype(BF16), wout_ref[...])
    o_ref[...] = h_ref[...] + _rms_rows(y, g_ref[3:4, :])


def _dsa(h, g, q, qi, wi, kcat, ckt, kidx, wuv_t, w_out, seq):
    bsz = q.shape[0]
    d = h.shape[1]
    tq = min(TQ_DSA, seq)
    nq = seq // tq
    top = min(DSA_TOP, seq // 4)
    wd = MLA_HEADS * tq
    per_b = lambda a: pl.BlockSpec((None,) + a.shape[1:], lambda b, i: (b,) + (0,) * (a.ndim - 1),
                                   pipeline_mode=pl.Buffered(1))
    return pl.pallas_call(
        functools.partial(_dsa_kernel, tq=tq, top=top),
        grid=(bsz, nq),
        in_specs=[pl.BlockSpec((tq, d), lambda b, i: (b * nq + i, 0)), _resident(g.shape),
                  pl.BlockSpec((None, None, MLA_QK, wd), lambda b, i: (b, i, 0, 0)),
                  pl.BlockSpec((None, IDX_HEADS * IDX_DIM, tq), lambda b, i: (b, 0, i)),
                  pl.BlockSpec((None, IDX_HEADS, tq), lambda b, i: (b, 0, i)),
                  per_b(kcat), per_b(ckt), per_b(kidx), _resident(wuv_t.shape),
                  _resident(w_out.shape)],
        out_specs=pl.BlockSpec((tq, d), lambda b, i: (b * nq + i, 0)),
        out_shape=jax.ShapeDtypeStruct(h.shape, F32),
        scratch_shapes=[pltpu.VMEM((seq, tq), jnp.int16),
                        pltpu.VMEM((seq, tq), jnp.int16),
                        pltpu.VMEM((MLA_LATENT, wd), F32),
                        pltpu.VMEM((1, wd), F32),
                        pltpu.VMEM((1, wd), F32),
                        pltpu.VMEM((1, tq), I32)],
        compiler_params=_cparams(("parallel", "arbitrary")),
        name="dsa",
    )(h, g, q, qi, wi, kcat, ckt, kidx, wuv_t, w_out)


def _rope_tables(seq, dim):
    half = dim // 2
    inv = ROPE_THETA ** (-jnp.arange(half, dtype=F32) / half)
    ang = inv[:, None] * jnp.arange(seq, dtype=F32)[None, :]
    return jnp.cos(ang), jnp.sin(ang)


def _even_mixer(h, g, bsz, seq, w_in, w_out, s5, pe_k, pe_v, wk1, wk2, wv1, wv2, tabs):
    cos32, sin32 = tabs[64]
    hd, hk = NSA_HEAD_DIM, NSA_KV_HEADS
    o = 0
    cols = {}
    for name, size in (("u", S5_WIDTH), ("q", NSA_Q), ("kc", NSA_KV), ("vc", NSA_KV), ("ks", NSA_KV),
                       ("vs", NSA_KV), ("kw", NSA_KV), ("vw", NSA_KV), ("gt", 3 * NSA_HEADS)):
        cols[name] = w_in[:, o:o + size]
        o += size
    wu = cols["u"].astype(BF16)
    wt = jnp.concatenate([cols[n] for n in ("q", "kc", "ks", "kw", "vc", "vs", "vw", "gt")], axis=1).T
    wt = jnp.pad(wt, ((0, (-wt.shape[0]) % 16), (0, 0))).astype(BF16)
    u, q_t, k_t, v_t, gt_t = _proj_even(h, g, wu, wt, cos32, sin32, bsz, seq)

    a_re, a_im, b_re, b_im, c_re, c_im, log_dt, d_skip, w_glu = s5
    mats = _s5_matrices(a_re, a_im, b_re, b_im, c_re, c_im, log_dt, S5_CHUNK)
    ys = _s5_scan(u, mats, bsz, seq)

    k5 = k_t.reshape(bsz, 3, hk, hd, seq)
    v5 = v_t.reshape(bsz, 3, hk, hd, seq)
    nb = seq // CMP_STRIDE
    half_blk = lambda a: (a.reshape(bsz, hk, hd, nb, CMP_STRIDE).transpose(0, 1, 3, 4, 2)
                          .reshape(bsz, hk, nb, CMP_STRIDE * hd))
    pe2 = lambda pe: pe.reshape(2, CMP_STRIDE * hd)
    w1s = lambda w: w.reshape(2, CMP_STRIDE * hd, hd).astype(BF16)
    kcmp = _compress(half_blk(k5[:, 0]), pe2(pe_k), w1s(wk1), wk2.astype(BF16))
    vcmp = _compress(half_blk(v5[:, 0]), pe2(pe_v), w1s(wv1), wv2.astype(BF16))
    vcmp_t = vcmp.transpose(0, 1, 3, 2)
    tk = min(TK_NSA, seq)
    ks = k5[:, 1].reshape(bsz, hk, hd, seq // tk, tk).transpose(0, 1, 3, 4, 2)
    vs_t = v5[:, 1].reshape(bsz, hk, hd, seq // tk, tk).transpose(0, 1, 3, 2, 4)
    kw = k5[:, 2].reshape(bsz, hk, hd, seq // LANE, LANE).transpose(0, 1, 3, 4, 2)
    vw_t = v5[:, 2].reshape(bsz, hk, hd, seq // LANE, LANE).transpose(0, 1, 3, 2, 4)
    n_sb = seq // SEL_BLOCK
    c_start = jnp.arange(nb) * CMP_STRIDE
    b_start = jnp.arange(n_sb) * SEL_BLOCK
    ov_t = ((c_start[None, :] < b_start[:, None] + SEL_BLOCK)
            & (c_start[None, :] + CMP_LEN > b_start[:, None])
            & (jnp.arange(nb)[None, :] < nb - 1)).astype(BF16)
    gates = gt_t.reshape(bsz, hk, 3 * NSA_GQA, seq)
    b_out = _nsa(q_t, gates, kcmp, vcmp_t, ks, vs_t, kw, vw_t, ov_t, seq)
    return _outproj_even(h, ys, u, b_out, g, d_skip.reshape(1, -1), w_glu.astype(BF16),
                         w_out[:S5_WIDTH].astype(BF16), w_out[S5_WIDTH:].astype(BF16))


def _odd_mixer(h, g, bsz, seq, w_in, kv_norm, w_uv, w_out, tabs):
    cos32, sin32 = tabs[64]
    cos16, sin16 = tabs[32]
    d = h.shape[1]
    sizes = (MLA_HEADS * MLA_LATENT, MLA_HEADS * MLA_ROPE, MLA_LATENT, MLA_ROPE,
             IDX_HEADS * IDX_DIM, IDX_DIM, IDX_HEADS)
    parts = []
    o = 0
    for s in sizes:
        parts.append(w_in[:, o:o + s])
        o += s
    w_ql, w_qr, w_c, w_kr, w_qi, w_ki, w_wi = parts
    wq = jnp.concatenate([w_ql.reshape(d, MLA_HEADS, MLA_LATENT), w_qr.reshape(d, MLA_HEADS, MLA_ROPE)],
                         axis=2).reshape(d, MLA_HEADS * MLA_QK).T.astype(BF16)
    wkv = jnp.concatenate([w_c, w_kr, w_qi, w_ki, w_wi], axis=1).T
    wkv = jnp.pad(wkv, ((0, (-wkv.shape[0]) % 16), (0, 0))).astype(BF16)
    ckv_t, kr_t, qi_t, ki_t, wi_t = _proj_odd_kv(h, g, wkv, kv_norm.reshape(-1, 1), cos32, sin32,
                                                 cos16, sin16, bsz, seq)
    q = _proj_odd_q(h, g, wq, cos16, sin16, bsz, seq)
    tk = min(TQ_DSA, seq)
    ta = min(TA_DSA, seq)
    kcat = jnp.concatenate([ckv_t, kr_t], axis=1).transpose(0, 2, 1).reshape(bsz, seq // ta, ta, MLA_QK)
    ckt = ckv_t.reshape(bsz, MLA_LATENT, seq // ta, ta).transpose(0, 2, 1, 3)
    kidx = ki_t.transpose(0, 2, 1).reshape(bsz, seq // tk, tk, IDX_DIM)
    wuv_t = w_uv.transpose(0, 2, 1).astype(BF16)
    return _dsa(h, g, q, qi_t, wi_t, kcat, ckt, kidx, wuv_t, w_out.astype(BF16), seq)


def kernel(x, p, norm_g, ffn1_w_in, ffn1_w_out, ffn2_w_in, ffn2_w_out, ple_w_gate, ple_w_proj,
           ev_w_in, ev_w_out, s5_a_re, s5_a_im, s5_b_re, s5_b_im, s5_c_re, s5_c_im, s5_log_dt, s5_d,
           s5_w_glu, nsa_pe_k, nsa_pe_v, nsa_wk1, nsa_wk2, nsa_wv1, nsa_wv2,
           od_w_in, od_kv_norm, od_w_uv, od_w_out):
    bsz, seq, d = x.shape
    depth = norm_g.shape[0]
    tabs = {64: _rope_tables(seq, 64), 32: _rope_tables(seq, 32)}
    h = x.reshape(bsz * seq, d)
    for i in range(depth):
        g = norm_g[i]
        j = i // 2
        h = _ffn(h, g, ffn1_w_in[i].astype(BF16), ffn1_w_out[i].astype(BF16), 0)
        if i % 2 == 0:
            s5 = (s5_a_re[j], s5_a_im[j], s5_b_re[j], s5_b_im[j], s5_c_re[j], s5_c_im[j],
                  s5_log_dt[j], s5_d[j], s5_w_glu[j])
            h = _even_mixer(h, g, bsz, seq, ev_w_in[j], ev_w_out[j], s5, nsa_pe_k[j], nsa_pe_v[j],
                            nsa_wk1[j], nsa_wk2[j], nsa_wv1[j], nsa_wv2[j], tabs)
        else:
            h = _odd_mixer(h, g, bsz, seq, od_w_in[j], od_kv_norm[j], od_w_uv[j], od_w_out[j], tabs)
        h = _ffn_ple(h, p[i].reshape(bsz * seq, -1), g, ffn2_w_in[i].astype(BF16),
                     ffn2_w_out[i].astype(BF16), ple_w_gate[i].astype(BF16),
                     ple_w_proj[i].astype(BF16))
    return h.reshape(bsz, seq, d)
```

```python
import functools
import math

import jax
import jax.numpy as jnp
from jax import lax
from jax.experimental import pallas as pl
from jax.experimental.pallas import tpu as pltpu

F32 = jnp.float32
BF16 = jnp.bfloat16
I32 = jnp.int32

ROPE_THETA = 10000.0
EPS = 1e-6
NEG = -1e30
LOG2E = math.log2(math.e)
D_FF = 2816
S5_WIDTH = 512
S5_GROUP = 16
S5_GROUPS = S5_WIDTH // S5_GROUP
S5_STATE = 64
NSA_HEADS = 8
NSA_KV_HEADS = 2
NSA_GQA = NSA_HEADS // NSA_KV_HEADS
NSA_HEAD_DIM = 64
CMP_LEN = 32
CMP_STRIDE = 16
SEL_BLOCK = 64
SEL_TOP = 16
WINDOW = 512
FORCE_BONUS = 1000.0
NSA_Q = NSA_HEADS * NSA_HEAD_DIM
NSA_KV = NSA_KV_HEADS * NSA_HEAD_DIM
MLA_HEADS = 16
MLA_LATENT = 256
MLA_ROPE = 32
MLA_QK = MLA_LATENT + MLA_ROPE
MLA_V_DIM = 64
IDX_HEADS = 8
IDX_DIM = 64
DSA_TOP = 256

V7X_VMEM_BYTES = 64 * 2**20
VMEM_LIMIT = V7X_VMEM_BYTES - 8 * 2**20
LANE = 128
TM_FFN = 1024
TQ_PROJ = 256
TQ_NSA = 128
TK_NSA = 512
TQ_DSA = 256
TA_DSA = 512
S5_CHUNK = 64


def _cparams(sem):
    return pltpu.CompilerParams(dimension_semantics=sem, vmem_limit_bytes=VMEM_LIMIT)


def _resident(shape):
    nd = len(shape)
    return pl.BlockSpec(shape, lambda *_: (0,) * nd, pipeline_mode=pl.Buffered(1))


def _dot(a, b):
    return jnp.dot(a, b, preferred_element_type=F32)


def _dot_nt(a, b):
    return lax.dot_general(a, b, (((1,), (1,)), ((), ())), preferred_element_type=F32)


def _rms_rows(x, g):
    return x * lax.rsqrt(jnp.mean(x * x, axis=-1, keepdims=True) + EPS) * g


def _rope_fmaj(y, cos, sin):
    half = y.shape[1] // 2
    t1 = y[:, :half, :]
    t2 = y[:, half:, :]
    return jnp.concatenate([t1 * cos - t2 * sin, t2 * cos + t1 * sin], axis=1)


def _ffn_tile(x, g_ref, g0, win_ref, wout_ref, n_chunk):
    xn = _rms_rows(x, g_ref[g0:g0 + 1, :]).astype(BF16)
    ck = D_FF // n_chunk
    acc = None
    for c in range(n_chunk):
        a = _dot(xn, win_ref[:, c * ck:(c + 1) * ck])
        u = _dot(xn, win_ref[:, D_FF + c * ck:D_FF + (c + 1) * ck])
        act = (jax.nn.silu(a) * u).astype(BF16)
        y = _dot(act, wout_ref[c * ck:(c + 1) * ck, :])
        acc = y if acc is None else acc + y
    return x + 0.5 * _rms_rows(acc, g_ref[g0 + 1:g0 + 2, :])


def _ffn_kernel(h_ref, g_ref, win_ref, wout_ref, o_ref, *, g0, n_chunk):
    o_ref[...] = _ffn_tile(h_ref[...], g_ref, g0, win_ref, wout_ref, n_chunk)


def _ffn_ple_kernel(h_ref, p_ref, g_ref, win_ref, wout_ref, wg_ref, wp_ref, o_ref, *, n_chunk):
    x = _ffn_tile(h_ref[...], g_ref, 4, win_ref, wout_ref, n_chunk)
    gate = jax.nn.sigmoid(_dot(_rms_rows(x, g_ref[6:7, :]).astype(BF16), wg_ref[...]))
    e = _dot(p_ref[...].astype(BF16), wp_ref[...]) * gate
    o_ref[...] = x + _rms_rows(e, g_ref[7:8, :])


def _ffn_ple(h, p, g, w_in, w_out, w_gate, w_proj):
    t, d = h.shape
    tm = min(TM_FFN, t)
    return pl.pallas_call(
        functools.partial(_ffn_ple_kernel, n_chunk=2),
        grid=(t // tm,),
        in_specs=[pl.BlockSpec((tm, d), lambda i: (i, 0)),
                  pl.BlockSpec((tm, p.shape[1]), lambda i: (i, 0)), _resident(g.shape),
                  _resident(w_in.shape), _resident(w_out.shape), _resident(w_gate.shape),
                  _resident(w_proj.shape)],
        out_specs=pl.BlockSpec((tm, d), lambda i: (i, 0)),
        out_shape=jax.ShapeDtypeStruct((t, d), F32),
        compiler_params=_cparams(("parallel",)),
        name="ffn_ple",
    )(h, p, g, w_in, w_out, w_gate, w_proj)


def _ffn(h, g, w_in, w_out, g0):
    t, d = h.shape
    tm = min(TM_FFN, t)
    return pl.pallas_call(
        functools.partial(_ffn_kernel, g0=g0, n_chunk=2),
        grid=(t // tm,),
        in_specs=[pl.BlockSpec((tm, d), lambda i: (i, 0)), _resident(g.shape),
                  _resident(w_in.shape), _resident(w_out.shape)],
        out_specs=pl.BlockSpec((tm, d), lambda i: (i, 0)),
        out_shape=jax.ShapeDtypeStruct((t, d), F32),
        compiler_params=_cparams(("parallel",)),
        name="ffn",
    )(h, g, w_in, w_out)


def _outproj_even_kernel(h_ref, ys_ref, u_ref, b_ref, g_ref, d_ref, wglu_ref, wa_ref, wb_ref, o_ref):
    y = ys_ref[...] + d_ref[...] * u_ref[...]
    z = jax.nn.gelu(y)
    a = z * jax.nn.sigmoid(_dot(z.astype(BF16), wglu_ref[...]))
    mix = _dot(a.astype(BF16), wa_ref[...]) + _dot(b_ref[...].astype(BF16), wb_ref[...])
    o_ref[...] = h_ref[...] + _rms_rows(mix, g_ref[3:4, :])


def _outproj_even(h, ys, u, b_out, g, d_skip, w_glu, w_a, w_b):
    t, d = h.shape
    tm = min(TM_FFN, t)
    tok = lambda w: pl.BlockSpec((tm, w), lambda i: (i, 0))
    return pl.pallas_call(
        _outproj_even_kernel,
        grid=(t // tm,),
        in_specs=[tok(d), tok(S5_WIDTH), tok(S5_WIDTH), tok(NSA_Q), _resident(g.shape),
                  _resident(d_skip.shape), _resident(w_glu.shape), _resident(w_a.shape),
                  _resident(w_b.shape)],
        out_specs=tok(d),
        out_shape=jax.ShapeDtypeStruct((t, d), F32),
        compiler_params=_cparams(("parallel",)),
        name="outproj_even",
    )(h, ys, u, b_out, g, d_skip, w_glu, w_a, w_b)


def _proj_even_kernel(h_ref, g_ref, wu_ref, wt_ref, cos_ref, sin_ref,
                      u_ref, q_ref, k_ref, v_ref, gt_ref):
    xn = _rms_rows(h_ref[...], g_ref[2:3, :]).astype(BF16)
    u_ref[...] = _dot(xn, wu_ref[...])
    y = _dot_nt(wt_ref[...], xn)
    tq = y.shape[1]
    cos = cos_ref[...]
    sin = sin_ref[...]
    hd = NSA_HEAD_DIM
    q = _rope_fmaj(y[0:NSA_Q].reshape(NSA_HEADS, hd, tq), cos, sin)
    q_ref[...] = (q * (hd ** -0.5 * LOG2E)).reshape(NSA_Q, tq)
    k0 = NSA_Q
    nk = 3 * NSA_KV
    k = _rope_fmaj(y[k0:k0 + nk].reshape(3 * NSA_KV_HEADS, hd, tq), cos, sin)
    k_ref[...] = k.reshape(nk, tq).astype(k_ref.dtype)
    v0 = k0 + nk
    v_ref[...] = y[v0:v0 + nk].astype(v_ref.dtype)
    g0 = v0 + nk
    gt_ref[...] = jax.nn.sigmoid(y[g0:g0 + 3 * NSA_HEADS])


def _proj_even(h, g, wu, wt, cos, sin, bsz, seq):
    d = h.shape[1]
    tq = min(TQ_PROJ, seq)
    nq = seq // tq
    nk = 3 * NSA_KV
    fm = lambda rows: pl.BlockSpec((None, rows, tq), lambda b, i: (b, 0, i))
    return pl.pallas_call(
        _proj_even_kernel,
        grid=(bsz, nq),
        in_specs=[pl.BlockSpec((tq, d), lambda b, i: (b * nq + i, 0)), _resident(g.shape),
                  _resident(wu.shape), _resident(wt.shape),
                  pl.BlockSpec((cos.shape[0], tq), lambda b, i: (0, i)),
                  pl.BlockSpec((sin.shape[0], tq), lambda b, i: (0, i))],
        out_specs=[pl.BlockSpec((tq, S5_WIDTH), lambda b, i: (b * nq + i, 0)),
                   fm(NSA_Q), fm(nk), fm(nk), fm(3 * NSA_HEADS)],
        out_shape=[jax.ShapeDtypeStruct((bsz * seq, S5_WIDTH), F32),
                   jax.ShapeDtypeStruct((bsz, NSA_Q, seq), F32),
                   jax.ShapeDtypeStruct((bsz, nk, seq), BF16),
                   jax.ShapeDtypeStruct((bsz, nk, seq), BF16),
                   jax.ShapeDtypeStruct((bsz, 3 * NSA_HEADS, seq), F32)],
        compiler_params=_cparams(("parallel", "parallel")),
        name="proj_even",
    )(h, g, wu, wt, cos, sin)


def _proj_odd_kv_kernel(h_ref, g_ref, wt_ref, kvn_ref, cos32_ref, sin32_ref, cos16_ref, sin16_ref,
                        ckv_ref, kr_ref, qi_ref, ki_ref, wi_ref):
    xn = _rms_rows(h_ref[...], g_ref[2:3, :]).astype(BF16)
    y = _dot_nt(wt_ref[...], xn)
    tq = y.shape[1]
    c = y[0:MLA_LATENT]
    c = c * lax.rsqrt(jnp.mean(c * c, axis=0, keepdims=True) + EPS) * kvn_ref[...]
    ckv_ref[...] = c.astype(ckv_ref.dtype)
    r0 = MLA_LATENT
    kr = _rope_fmaj(y[r0:r0 + MLA_ROPE].reshape(1, MLA_ROPE, tq), cos16_ref[...], sin16_ref[...])
    kr_ref[...] = kr.reshape(MLA_ROPE, tq).astype(kr_ref.dtype)
    q0 = r0 + MLA_ROPE
    nqi = IDX_HEADS * IDX_DIM
    qi = _rope_fmaj(y[q0:q0 + nqi].reshape(IDX_HEADS, IDX_DIM, tq), cos32_ref[...], sin32_ref[...])
    qi_ref[...] = qi.reshape(nqi, tq).astype(qi_ref.dtype)
    k0 = q0 + nqi
    ki = _rope_fmaj(y[k0:k0 + IDX_DIM].reshape(1, IDX_DIM, tq), cos32_ref[...], sin32_ref[...])
    ki_ref[...] = ki.reshape(IDX_DIM, tq).astype(ki_ref.dtype)
    w0 = k0 + IDX_DIM
    wi_ref[...] = y[w0:w0 + IDX_HEADS]


def _proj_odd_kv(h, g, wt, kvn, cos32, sin32, cos16, sin16, bsz, seq):
    d = h.shape[1]
    tq = min(TQ_PROJ, seq)
    nq = seq // tq
    fm = lambda rows: pl.BlockSpec((None, rows, tq), lambda b, i: (b, 0, i))
    tab = lambda a: pl.BlockSpec((a.shape[0], tq), lambda b, i: (0, i))
    nqi = IDX_HEADS * IDX_DIM
    return pl.pallas_call(
        _proj_odd_kv_kernel,
        grid=(bsz, nq),
        in_specs=[pl.BlockSpec((tq, d), lambda b, i: (b * nq + i, 0)), _resident(g.shape),
                  _resident(wt.shape), _resident(kvn.shape),
                  tab(cos32), tab(sin32), tab(cos16), tab(sin16)],
        out_specs=[fm(MLA_LATENT), fm(MLA_ROPE), fm(nqi), fm(IDX_DIM), fm(IDX_HEADS)],
        out_shape=[jax.ShapeDtypeStruct((bsz, MLA_LATENT, seq), BF16),
                   jax.ShapeDtypeStruct((bsz, MLA_ROPE, seq), BF16),
                   jax.ShapeDtypeStruct((bsz, nqi, seq), BF16),
                   jax.ShapeDtypeStruct((bsz, IDX_DIM, seq), BF16),
                   jax.ShapeDtypeStruct((bsz, IDX_HEADS, seq), F32)],
        compiler_params=_cparams(("parallel", "parallel")),
        name="proj_odd_kv",
    )(h, g, wt, kvn, cos32, sin32, cos16, sin16)


def _proj_odd_q_kernel(h_ref, g_ref, wt_ref, cos16_ref, sin16_ref, q_ref):
    xn = _rms_rows(h_ref[...], g_ref[2:3, :]).astype(BF16)
    tq = xn.shape[0]
    cos = cos16_ref[...]
    sin = sin16_ref[...]
    for hh in range(MLA_HEADS):
        y = _dot_nt(wt_ref[hh * MLA_QK:(hh + 1) * MLA_QK, :], xn)
        r = _rope_fmaj(y[MLA_LATENT:].reshape(1, MLA_ROPE, tq), cos, sin)
        q = jnp.concatenate([y[:MLA_LATENT], r.reshape(MLA_ROPE, tq)], axis=0)
        q_ref[:, hh * tq:(hh + 1) * tq] = (q * (MLA_QK ** -0.5 * LOG2E)).astype(q_ref.dtype)


def _proj_odd_q(h, g, wt, cos16, sin16, bsz, seq):
    d = h.shape[1]
    tq = min(TQ_DSA, seq)
    nq = seq // tq
    return pl.pallas_call(
        _proj_odd_q_kernel,
        grid=(bsz, nq),
        in_specs=[pl.BlockSpec((tq, d), lambda b, i: (b * nq + i, 0)), _resident(g.shape),
                  _resident(wt.shape),
                  pl.BlockSpec((cos16.shape[0], tq), lambda b, i: (0, i)),
                  pl.BlockSpec((sin16.shape[0], tq), lambda b, i: (0, i))],
        out_specs=pl.BlockSpec((None, None, MLA_QK, MLA_HEADS * tq), lambda b, i: (b, i, 0, 0)),
        out_shape=jax.ShapeDtypeStruct((bsz, nq, MLA_QK, MLA_HEADS * tq), BF16),
        compiler_params=_cparams(("parallel", "parallel")),
        name="proj_odd_q",
    )(h, g, wt, cos16, sin16)


def _s5_state_kernel(u_ref, bc_ref, s_ref):
    s_ref[...] = _dot(u_ref[...], bc_ref[...])


def _s5_scan_kernel(sr_ref, si_ref, lr_ref, li_ref, xr_ref, xi_ref):
    n_chunk = sr_ref.shape[0]
    lr = lr_ref[...]
    li = li_ref[...]

    def body(c, carry):
        xr, xi = carry
        xr_ref[c] = xr
        xi_ref[c] = xi
        return (lr * xr - li * xi + sr_ref[c], lr * xi + li * xr + si_ref[c])

    zero = jnp.zeros(sr_ref.shape[1:], F32)
    lax.fori_loop(0, n_chunk, body, (zero, zero))


def _s5_out_kernel(u_ref, x_ref, m_ref, cc_ref, y_ref):
    x = x_ref[...]
    hi = x.astype(BF16)
    lo = (x - hi.astype(F32)).astype(BF16)
    cc = cc_ref[...]
    y_ref[...] = _dot(u_ref[...], m_ref[...]) + _dot(hi, cc) + _dot(lo, cc)


def _s5_matrices(a_re, a_im, b_re, b_im, c_re, c_im, log_dt, tc):
    hp = lax.Precision.HIGHEST
    dt = jnp.exp(log_dt)[:, None]
    lam = lax.complex(a_re, a_im)
    lam_dt = lam * dt
    lam_bar = jnp.exp(lam_dt)
    b_bar = ((lam_bar - 1.0) / lam)[..., None] * lax.complex(b_re, b_im)
    k = jnp.arange(tc + 1, dtype=F32)[:, None, None]
    pw = jnp.exp(lam_dt[None] * k)
    pr, pi = jnp.real(pw), jnp.imag(pw)
    bbr, bbi = jnp.real(b_bar), jnp.imag(b_bar)
    cpr = c_re[None] * pr[:, :, None, :] - c_im[None] * pi[:, :, None, :]
    cpi = c_re[None] * pi[:, :, None, :] + c_im[None] * pr[:, :, None, :]
    kk = (jnp.einsum('kgpn,gnq->kgpq', cpr[:tc], bbr, precision=hp)
          - jnp.einsum('kgpn,gnq->kgpq', cpi[:tc], bbi, precision=hp))
    n_g, n_p = a_re.shape[0], b_re.shape[2]
    kq = jnp.pad(kk.transpose(1, 3, 0, 2).astype(BF16), ((0, 0), (0, 0), (tc, 0), (0, 0)))
    lag = tc + jnp.arange(tc)[None, :] - jnp.arange(tc)[:, None]
    m = kq[:, :, lag, :].reshape(n_g, n_p * tc, tc * n_p)
    rev = pw[tc - 1 - jnp.arange(tc)]
    bc = rev[:, :, :, None] * b_bar[None]
    bc = bc.transpose(1, 3, 0, 2).reshape(n_g, n_p * tc, -1)
    bc = jnp.concatenate([jnp.real(bc), jnp.imag(bc)], axis=-1)
    mr = cpr[1:tc + 1].transpose(1, 3, 0, 2).reshape(n_g, -1, tc * n_p)
    mi = cpi[1:tc + 1].transpose(1, 3, 0, 2).reshape(n_g, -1, tc * n_p)
    cc = jnp.concatenate([mr, -mi], axis=1)
    ltc = pw[tc].reshape(1, -1)
    return m, bc.astype(BF16), cc.astype(BF16), jnp.real(ltc), jnp.imag(ltc)


def _s5_scan(u, mats, bsz, seq):
    m, bc, cc, lr, li = mats
    tc = S5_CHUNK
    n_c = seq // tc
    n_g, n_p, n_s = S5_GROUPS, S5_GROUP, S5_STATE
    rows = bsz * n_c
    kd = tc * n_p
    ug = (u.reshape(bsz, n_c, tc, n_g, n_p).transpose(3, 0, 1, 4, 2)
          .reshape(n_g, rows, kd).astype(BF16))
    grp = lambda a, b: pl.BlockSpec((None, a, b), lambda gi: (gi, 0, 0))
    s = pl.pallas_call(
        _s5_state_kernel,
        grid=(n_g,),
        in_specs=[grp(rows, kd), grp(kd, 2 * n_s)],
        out_specs=grp(rows, 2 * n_s),
        out_shape=jax.ShapeDtypeStruct((n_g, rows, 2 * n_s), F32),
        compiler_params=_cparams(("parallel",)),
        name="s5_state",
    )(ug, bc)
    s5 = s.reshape(n_g, bsz, n_c, 2, n_s).transpose(3, 2, 1, 0, 4).reshape(2, n_c, bsz, n_g * n_s)
    full = lambda shp: pl.BlockSpec(shp, lambda: (0,) * len(shp))
    xr, xi = pl.pallas_call(
        _s5_scan_kernel,
        in_specs=[full(s5.shape[1:]), full(s5.shape[1:]), full(lr.shape), full(li.shape)],
        out_specs=[full(s5.shape[1:]), full(s5.shape[1:])],
        out_shape=[jax.ShapeDtypeStruct(s5.shape[1:], F32)] * 2,
        compiler_params=pltpu.CompilerParams(vmem_limit_bytes=VMEM_LIMIT),
        name="s5_scan",
    )(s5[0], s5[1], lr, li)
    x = jnp.stack([xr, xi]).reshape(2, n_c, bsz, n_g, n_s).transpose(3, 2, 1, 0, 4)
    x = x.reshape(n_g, rows, 2 * n_s)
    y = pl.pallas_call(
        _s5_out_kernel,
        grid=(n_g,),
        in_specs=[grp(rows, kd), grp(rows, 2 * n_s), grp(kd, kd), grp(2 * n_s, kd)],
        out_specs=grp(rows, kd),
        out_shape=jax.ShapeDtypeStruct((n_g, rows, kd), F32),
        compiler_params=_cparams(("parallel",)),
        name="s5_out",
    )(ug, x, m, cc)
    return (y.reshape(n_g, bsz, n_c, tc, n_p).transpose(1, 2, 3, 0, 4)
            .reshape(bsz * seq, n_g * n_p))


def _compress_kernel(x_ref, pe_ref, w1_ref, w2_ref, o_ref):
    x = x_ref[...].astype(F32)
    nb = x.shape[0]
    a = _dot((x + pe_ref[0:1, :]).astype(BF16), w1_ref[0])
    b = _dot((x + pe_ref[1:2, :]).astype(BF16), w1_ref[1])
    pre = a + pltpu.roll(b, nb - 1, 0)
    o_ref[...] = _dot(jax.nn.gelu(pre).astype(BF16), w2_ref[...]).astype(o_ref.dtype)


def _compress(x, pe, w1, w2):
    bsz, hk, nb, kd = x.shape
    hd = w2.shape[1]
    return pl.pallas_call(
        _compress_kernel,
        grid=(bsz, hk),
        in_specs=[pl.BlockSpec((None, None, nb, kd), lambda b, h: (b, h, 0, 0)),
                  _resident(pe.shape), _resident(w1.shape), _resident(w2.shape)],
        out_specs=pl.BlockSpec((None, None, nb, hd), lambda b, h: (b, h, 0, 0)),
        out_shape=jax.ShapeDtypeStruct((bsz, hk, nb, hd), BF16),
        compiler_params=_cparams(("parallel", "parallel")),
        name="nsa_compress",
    )(x, pe, w1, w2)


def _nsa_kernel(q_ref, g_ref, kc_ref, vct_ref, ks_ref, vst_ref, kw_ref, vwt_ref, ov_ref,
                o_ref, sel_ref, zero_ref, sa_ref, sb_ref, *, tq, tk, top_n):
    gq, hd = NSA_GQA, NSA_HEAD_DIM
    n_sb = sel_ref.shape[0]
    nb = kc_ref.shape[0]
    wd = gq * tq
    i = pl.program_id(2)
    s0 = i * tq
    q = q_ref[...]
    qt = jnp.concatenate([q[g * hd:(g + 1) * hd, :] for g in range(gq)], axis=1).astype(BF16)
    t1 = s0 + lax.broadcasted_iota(I32, (1, tq), 1)
    t4 = s0 + (lax.broadcasted_iota(I32, (1, wd), 1) & (tq - 1))

    sc = _dot(kc_ref[...], qt)
    c_last = lax.broadcasted_iota(I32, (nb, 1), 0) * CMP_STRIDE + (CMP_LEN - 1)
    bias_c = jnp.where(c_last <= t1, 0.0, NEG)
    sm = sc + jnp.concatenate([bias_c] * gq, axis=1)
    e = jnp.exp2(sm - jnp.max(sm, axis=0, keepdims=True))
    p_c = e * jnp.where(t4 >= CMP_LEN - 1, 1.0 / jnp.sum(e, axis=0, keepdims=True), 0.0)
    o_c = _dot(vct_ref[...], p_c.astype(BF16))
    psum = p_c[:, 0:tq]
    for g in range(1, gq):
        psum = psum + p_c[:, g * tq:(g + 1) * tq]
    hi = psum.astype(BF16)
    lo = (psum - hi.astype(F32)).astype(BF16)
    imp = _dot(ov_ref[...], hi) + _dot(ov_ref[...], lo)
    blk = lax.broadcasted_iota(I32, (n_sb, 1), 0)
    cur = t1 >> int(math.log2(SEL_BLOCK))
    forced = (blk == 0) | (blk == cur) | (blk == cur - 1)
    imp = imp + jnp.where(forced, FORCE_BONUS, 0.0)
    imp = jnp.where(blk * SEL_BLOCK <= t1, imp, NEG)

    rowf = lax.broadcasted_iota(I32, (n_sb, tq), 0).astype(F32)
    sel = jnp.zeros((n_sb, tq), F32)
    x = imp
    for _ in range(top_n):
        mx = jnp.max(x, axis=0, keepdims=True)
        first = jnp.min(jnp.where(x == mx, rowf, float(n_sb)), axis=0, keepdims=True)
        hit = rowf == first
        sel = jnp.where(hit, 1.0, sel)
        x = jnp.where(hit, -jnp.inf, x)
    sel_ref[...] = sel

    zero_ref[...] = jnp.zeros(zero_ref.shape, F32)
    init = (jnp.full((1, wd), NEG, F32), jnp.zeros((1, wd), F32), zero_ref[...])

    bpt = tk // SEL_BLOCK
    key_row = lax.broadcasted_iota(I32, (tk, 1), 0)

    last_tile = ks_ref.shape[0] - 1

    def scores(j):
        return _dot(ks_ref[jnp.minimum(j, last_tile)], qt)

    def consume(j, s_ref, carry):
        m, l, acc, pv = carry
        jc = jnp.minimum(j, last_tile)
        rows = [jnp.broadcast_to(sel_ref[pl.ds(jc * bpt + b, 1), :], (SEL_BLOCK, tq))
                for b in range(bpt)]
        keep = (jnp.concatenate(rows, axis=0) > 0.5) & (j * tk + key_row <= t1)
        bias = jnp.where(keep, 0.0, NEG)
        s = s_ref[...] + jnp.concatenate([bias] * gq, axis=1)
        m_new = jnp.maximum(m, jnp.max(s, axis=0, keepdims=True))
        alpha = jnp.exp2(m - m_new)
        p = jnp.exp2(s - m_new)
        l = alpha * l + jnp.sum(p, axis=0, keepdims=True)
        return m_new, l, alpha * (acc + pv), _dot(vst_ref[jc], p.astype(BF16))

    def tile_pair(k, carry):
        j = 2 * k
        sb_ref[...] = scores(j + 1)
        carry = consume(j, sa_ref, carry)
        sa_ref[...] = scores(j + 2)
        return consume(j + 1, sb_ref, carry)

    sa_ref[...] = scores(0)
    _, l_s, a_s, pv_s = lax.fori_loop(0, (s0 // tk + 2) // 2, tile_pair, init + (zero_ref[...],))
    o_s = (a_s + pv_s) / l_s

    n_wt = (WINDOW + tq) // LANE
    j_lo = jnp.clip(i * (tq // LANE) - WINDOW // LANE, 0, kw_ref.shape[0] - n_wt)
    kwin = jnp.concatenate([kw_ref[j_lo + w] for w in range(n_wt)], axis=0)
    vwin = jnp.concatenate([vwt_ref[j_lo + w] for w in range(n_wt)], axis=1)
    sw = _dot(kwin, qt)
    diff = t1 - (j_lo * LANE + lax.broadcasted_iota(I32, (n_wt * LANE, 1), 0))
    sw = sw + jnp.concatenate([jnp.where((diff >= 0) & (diff < WINDOW), 0.0, NEG)] * gq, axis=1)
    pw = jnp.exp2(sw - jnp.max(sw, axis=0, keepdims=True))
    o_w = _dot(vwin, pw.astype(BF16)) / jnp.sum(pw, axis=0, keepdims=True)

    gt = g_ref[...]
    outs = []
    for g in range(gq):
        cs = slice(g * tq, (g + 1) * tq)
        outs.append(gt[3 * g:3 * g + 1, :] * o_c[:, cs] + gt[3 * g + 1:3 * g + 2, :] * o_s[:, cs]
                    + gt[3 * g + 2:3 * g + 3, :] * o_w[:, cs])
    o_ref[...] = jnp.concatenate(outs, axis=0).T


def _nsa(q, gates, kcmp, vcmp_t, ks, vs_t, kw, vw_t, ov_t, seq):
    bsz = q.shape[0]
    tq = min(TQ_NSA, seq)
    tk = min(TK_NSA, seq)
    n_sb = seq // SEL_BLOCK
    gq, hd = NSA_GQA, NSA_HEAD_DIM
    per_head = lambda a: pl.BlockSpec((None, None) + a.shape[2:],
                                      lambda b, h, i: (b, h) + (0,) * (a.ndim - 2),
                                      pipeline_mode=pl.Buffered(1))
    return pl.pallas_call(
        functools.partial(_nsa_kernel, tq=tq, tk=tk, top_n=min(SEL_TOP, n_sb)),
        grid=(bsz, NSA_KV_HEADS, seq // tq),
        in_specs=[pl.BlockSpec((None, gq * hd, tq), lambda b, h, i: (b, h, i)),
                  pl.BlockSpec((None, None, 3 * gq, tq), lambda b, h, i: (b, h, 0, i)),
                  per_head(kcmp), per_head(vcmp_t), per_head(ks), per_head(vs_t),
                  per_head(kw), per_head(vw_t), _resident(ov_t.shape)],
        out_specs=pl.BlockSpec((tq, gq * hd), lambda b, h, i: (b * (seq // tq) + i, h)),
        out_shape=jax.ShapeDtypeStruct((bsz * seq, NSA_Q), F32),
        scratch_shapes=[pltpu.VMEM((n_sb, tq), F32), pltpu.VMEM((hd, gq * tq), F32),
                        pltpu.VMEM((tk, gq * tq), F32), pltpu.VMEM((tk, gq * tq), F32)],
        compiler_params=_cparams(("parallel", "parallel", "arbitrary")),
        name="nsa",
    )(q, gates, kcmp, vcmp_t, ks, vs_t, kw, vw_t, ov_t)


def _dsa_kernel(h_ref, g_ref, q_ref, qi_ref, wi_ref, kcat_ref, ckt_ref, kidx_ref, wuv_ref, wout_ref,
                o_ref, ihi, ilo, acc_ref, m_ref, l_ref, tie_cap, *, tq, top):
    tk = tq
    n_h = MLA_HEADS
    i16 = jnp.int16
    i = pl.program_id(1)
    row = lax.broadcasted_iota(I32, (tk, 1), 0)
    col = lax.broadcasted_iota(I32, (1, tq), 1)
    causal = row <= col

    def tile(ref, j):
        return ref.at[pl.ds(pl.multiple_of(j * tk, tk), tk), :]

    qi = qi_ref[...]
    wi = wi_ref[...] * (IDX_HEADS ** -0.5) * (IDX_DIM ** -0.5)

    def idx_scores(j):
        kj = kidx_ref[j]
        sc = jnp.zeros((tk, tq), F32)
        for h in range(IDX_HEADS):
            d = _dot(kj, qi[h * IDX_DIM:(h + 1) * IDX_DIM, :])
            sc = sc + wi[h:h + 1, :] * jnp.maximum(d, 0.0)
        return sc

    def store_keys(j, sc):
        b = pltpu.bitcast(sc, I32)
        key = b ^ ((b >> 31) & 0x7FFFFFFF)
        tile(ihi, j)[...] = (key >> 16).astype(i16)
        tile(ilo, j)[...] = ((key & 0xFFFF) - 32768).astype(i16)

    def idx_tile(j, _):
        store_keys(j, idx_scores(j))
        return 0

    lax.fori_loop(0, i, idx_tile, 0)
    store_keys(i, jnp.where(causal, idx_scores(i), NEG))

    @pl.when(((i + 1) * tq) % kcat_ref.shape[1] != 0)
    def _():
        store_keys(i + 1, jnp.full((tk, tq), NEG, F32))

    def count(ref, pred):
        def body(j, c):
            hit = jnp.where(pred(tile(ref, j)[...]), jnp.ones((), i16), jnp.zeros((), i16))
            parts = [hit[k * 16:(k + 1) * 16] for k in range(tk // 16)]
            while len(parts) > 1:
                parts = [a + b for a, b in zip(parts[0::2], parts[1::2])]
            return c + parts[0]
        c16 = lax.fori_loop(0, i + 1, body, jnp.zeros((16, tq), i16))
        return jnp.sum(c16.astype(I32), axis=0, keepdims=True)

    def search(ref, need, n_all):
        def bit_step(b, st):
            thr, c_ge, c_gt = st
            cand = thr + (jnp.int32(1) << (15 - b))
            cand16 = cand.astype(i16)
            cnt = count(ref, lambda t: t >= cand16)
            ok = cnt >= need
            return jnp.where(ok, cand, thr), jnp.where(ok, cnt, c_ge), jnp.where(ok, c_gt, cnt)
        init = (jnp.full((1, tq), -32768, I32), n_all, jnp.zeros((1, tq), I32))
        return lax.fori_loop(0, 16, bit_step, init)

    thr_hi, n_ge_hi, n_gt_hi = search(ihi, top, jnp.full((1, tq), 1, I32) * ((i + 1) * tk))
    thr_hi16 = thr_hi.astype(i16)
    need = top - n_gt_hi
    n_eq_hi = n_ge_hi - n_gt_hi

    def mask_lo(j, _):
        lo = tile(ilo, j)
        lo[...] = jnp.where(tile(ihi, j)[...] == thr_hi16, lo[...], jnp.full((), -32768, i16))
        return 0

    lax.fori_loop(0, i + 1, mask_lo, 0)
    thr_lo, n_ge_lo, _ = search(ilo, need, n_eq_hi)

    n_sel = n_gt_hi + n_ge_lo
    tie_cap[...] = jnp.full((1, tq), ihi.shape[0], I32)
    excess = n_sel - top
    max_excess = jnp.max(excess)
    n_bits = (ihi.shape[0] - 1).bit_length()

    def eq_tile(j):
        return ((tile(ihi, j)[...].astype(I32) == thr_hi)
                & (tile(ilo, j)[...].astype(I32) == thr_lo))

    n_top = 3
    max_scans = 2 * n_bits // n_top

    @pl.when((max_excess > 0) & (max_excess <= n_top * max_scans))
    def _():
        def drop(_, st):
            cap, left = st

            def body(j, tops):
                idx = j * tk + row
                v = jnp.where(eq_tile(j) & (idx <= cap), idx, -1).reshape(tk // 8, 8, tq)
                for k in range(tk // 8):
                    t = v[k]
                    new = []
                    for m in tops:
                        new.append(jnp.maximum(m, t))
                        t = jnp.minimum(m, t)
                    tops = tuple(new)
                return tops

            tops = list(lax.fori_loop(0, i + 1, body, (jnp.full((8, tq), -1, I32),) * n_top))
            take = jnp.minimum(left, n_top)
            for r in range(n_top):
                g = jnp.max(functools.reduce(jnp.maximum, tops), axis=0, keepdims=True)
                cap = jnp.where(take == r + 1, g - 1, cap)
                tops = [jnp.where(m == g, -1, m) for m in tops]
            return cap, left - take

        n_scan = (max_excess + n_top - 1) // n_top
        tie_cap[...] = lax.fori_loop(0, n_scan, drop, (tie_cap[...], excess))[0]

    @pl.when(max_excess > n_top * max_scans)
    def _():
        def count32(pred):
            def body(j, c):
                return c + jnp.sum(jnp.where(pred(j), 1, 0).reshape(tk // 8, 8, tq), axis=0)
            c8 = lax.fori_loop(0, i + 1, body, jnp.zeros((8, tq), I32))
            return jnp.sum(c8, axis=0, keepdims=True)

        need_eq = count32(eq_tile) - excess

        def bit_step(b, cap):
            cand = cap + (jnp.int32(1) << (n_bits - 1 - b))
            below = count32(lambda j: eq_tile(j) & (j * tk + row < cand))
            return jnp.where(below >= need_eq, cap, cand)

        tie_cap[...] = lax.fori_loop(0, n_bits, bit_step, jnp.zeros((1, tq), I32))

    m_ref[...] = jnp.full(m_ref.shape, NEG, F32)
    l_ref[...] = jnp.zeros(l_ref.shape, F32)
    acc_ref[...] = jnp.zeros(acc_ref.shape, F32)
    cap = tie_cap[...]

    ta = kcat_ref.shape[1]
    a_row = lax.broadcasted_iota(I32, (ta, 1), 0)
    q_pos = i * tq + col

    def att_tile(j, _):
        rows = pl.ds(pl.multiple_of(j * ta, ta), ta)
        hi = ihi[rows, :].astype(I32)
        lo = ilo[rows, :].astype(I32)
        k_pos = j * ta + a_row
        keep = (hi > thr_hi) | ((hi == thr_hi) & ((lo > thr_lo) | ((lo == thr_lo) & (k_pos <= cap))))
        bias = jnp.where(keep & (k_pos <= q_pos), 0.0, NEG)
        s_all = _dot(kcat_ref[j], q_ref[...])
        ps, alphas = [], []
        for h in range(n_h):
            cs = slice(h * tq, (h + 1) * tq)
            s = s_all[:, cs] + bias
            m_old = m_ref[:, cs]
            m_new = jnp.maximum(m_old, jnp.max(s, axis=0, keepdims=True))
            alpha = jnp.exp2(m_old - m_new)
            p = jnp.exp2(s - m_new)
            l_ref[:, cs] = alpha * l_ref[:, cs] + jnp.sum(p, axis=0, keepdims=True)
            m_ref[:, cs] = m_new
            ps.append(p.astype(BF16))
            alphas.append(alpha)
        p_all = jnp.concatenate(ps, axis=1)
        acc_ref[...] = acc_ref[...] * jnp.concatenate(alphas, axis=1) + _dot(ckt_ref[j], p_all)
        return 0

    lax.fori_loop(0, (i * tq) // ta + 1, att_tile, 0)

    outs = []
    for h in range(n_h):
        cs = slice(h * tq, (h + 1) * tq)
        o = acc_ref[:, cs] / l_ref[:, cs]
        outs.append(_dot(wuv_ref[h], o.astype(BF16)))
    mix = jnp.concatenate(outs, axis=0).T
    y = _dot(mix.astype(BF16), wout_ref[...])
    o_ref[...] = h_ref[...] + _rms_rows(y, g_ref[3:4, :])


def _dsa(h, g, q, qi, wi, kcat, ckt, kidx, wuv_t, w_out, seq):
    bsz = q.shape[0]
    d = h.shape[1]
    tq = min(TQ_DSA, seq)
    nq = seq // tq
    top = min(DSA_TOP, seq // 4)
    wd = MLA_HEADS * tq
    per_b = lambda a: pl.BlockSpec((None,) + a.shape[1:], lambda b, i: (b,) + (0,) * (a.ndim - 1),
                                   pipeline_mode=pl.Buffered(1))
    return pl.pallas_call(
        functools.partial(_dsa_kernel, tq=tq, top=top),
        grid=(bsz, nq),
        in_specs=[pl.BlockSpec((tq, d), lambda b, i: (b * nq + i, 0)), _resident(g.shape),
                  pl.BlockSpec((None, None, MLA_QK, wd), lambda b, i: (b, i, 0, 0)),
                  pl.BlockSpec((None, IDX_HEADS * IDX_DIM, tq), lambda b, i: (b, 0, i)),
                  pl.BlockSpec((None, IDX_HEADS, tq), lambda b, i: (b, 0, i)),
                  per_b(kcat), per_b(ckt), per_b(kidx), _resident(wuv_t.shape),
                  _resident(w_out.shape)],
        out_specs=pl.BlockSpec((tq, d), lambda b, i: (b * nq + i, 0)),
        out_shape=jax.ShapeDtypeStruct(h.shape, F32),
        scratch_shapes=[pltpu.VMEM((seq, tq), jnp.int16),
                        pltpu.VMEM((seq, tq), jnp.int16),
                        pltpu.VMEM((MLA_LATENT, wd), F32),
                        pltpu.VMEM((1, wd), F32),
                        pltpu.VMEM((1, wd), F32),
                        pltpu.VMEM((1, tq), I32)],
        compiler_params=_cparams(("parallel", "arbitrary")),
        name="dsa",
    )(h, g, q, qi, wi, kcat, ckt, kidx, wuv_t, w_out)


def _rope_tables(seq, dim):
    half = dim // 2
    inv = ROPE_THETA ** (-jnp.arange(half, dtype=F32) / half)
    ang = inv[:, None] * jnp.arange(seq, dtype=F32)[None, :]
    return jnp.cos(ang), jnp.sin(ang)


def _even_mixer(h, g, bsz, seq, w_in, w_out, s5, pe_k, pe_v, wk1, wk2, wv1, wv2, tabs):
    cos32, sin32 = tabs[64]
    hd, hk = NSA_HEAD_DIM, NSA_KV_HEADS
    o = 0
    cols = {}
    for name, size in (("u", S5_WIDTH), ("q", NSA_Q), ("kc", NSA_KV), ("vc", NSA_KV), ("ks", NSA_KV),
                       ("vs", NSA_KV), ("kw", NSA_KV), ("vw", NSA_KV), ("gt", 3 * NSA_HEADS)):
        cols[name] = w_in[:, o:o + size]
        o += size
    wu = cols["u"].astype(BF16)
    wt = jnp.concatenate([cols[n] for n in ("q", "kc", "ks", "kw", "vc", "vs", "vw", "gt")], axis=1).T
    wt = jnp.pad(wt, ((0, (-wt.shape[0]) % 16), (0, 0))).astype(BF16)
    u, q_t, k_t, v_t, gt_t = _proj_even(h, g, wu, wt, cos32, sin32, bsz, seq)

    a_re, a_im, b_re, b_im, c_re, c_im, log_dt, d_skip, w_glu = s5
    mats = _s5_matrices(a_re, a_im, b_re, b_im, c_re, c_im, log_dt, S5_CHUNK)
    ys = _s5_scan(u, mats, bsz, seq)

    k5 = k_t.reshape(bsz, 3, hk, hd, seq)
    v5 = v_t.reshape(bsz, 3, hk, hd, seq)
    nb = seq // CMP_STRIDE
    half_blk = lambda a: (a.reshape(bsz, hk, hd, nb, CMP_STRIDE).transpose(0, 1, 3, 4, 2)
                          .reshape(bsz, hk, nb, CMP_STRIDE * hd))
    pe2 = lambda pe: pe.reshape(2, CMP_STRIDE * hd)
    w1s = lambda w: w.reshape(2, CMP_STRIDE * hd, hd).astype(BF16)
    kcmp = _compress(half_blk(k5[:, 0]), pe2(pe_k), w1s(wk1), wk2.astype(BF16))
    vcmp = _compress(half_blk(v5[:, 0]), pe2(pe_v), w1s(wv1), wv2.astype(BF16))
    vcmp_t = vcmp.transpose(0, 1, 3, 2)
    tk = min(TK_NSA, seq)
    ks = k5[:, 1].reshape(bsz, hk, hd, seq // tk, tk).transpose(0, 1, 3, 4, 2)
    vs_t = v5[:, 1].reshape(bsz, hk, hd, seq // tk, tk).transpose(0, 1, 3, 2, 4)
    kw = k5[:, 2].reshape(bsz, hk, hd, seq // LANE, LANE).transpose(0, 1, 3, 4, 2)
    vw_t = v5[:, 2].reshape(bsz, hk, hd, seq // LANE, LANE).transpose(0, 1, 3, 2, 4)
    n_sb = seq // SEL_BLOCK
    c_start = jnp.arange(nb) * CMP_STRIDE
    b_start = jnp.arange(n_sb) * SEL_BLOCK
    ov_t = ((c_start[None, :] < b_start[:, None] + SEL_BLOCK)
            & (c_start[None, :] + CMP_LEN > b_start[:, None])
            & (jnp.arange(nb)[None, :] < nb - 1)).astype(BF16)
    gates = gt_t.reshape(bsz, hk, 3 * NSA_GQA, seq)
    b_out = _nsa(q_t, gates, kcmp, vcmp_t, ks, vs_t, kw, vw_t, ov_t, seq)
    return _outproj_even(h, ys, u, b_out, g, d_skip.reshape(1, -1), w_glu.astype(BF16),
                         w_out[:S5_WIDTH].astype(BF16), w_out[S5_WIDTH:].astype(BF16))


def _odd_mixer(h, g, bsz, seq, w_in, kv_norm, w_uv, w_out, tabs):
    cos32, sin32 = tabs[64]
    cos16, sin16 = tabs[32]
    d = h.shape[1]
    sizes = (MLA_HEADS * MLA_LATENT, MLA_HEADS * MLA_ROPE, MLA_LATENT, MLA_ROPE,
             IDX_HEADS * IDX_DIM, IDX_DIM, IDX_HEADS)
    parts = []
    o = 0
    for s in sizes:
        parts.append(w_in[:, o:o + s])
        o += s
    w_ql, w_qr, w_c, w_kr, w_qi, w_ki, w_wi = parts
    wq = jnp.concatenate([w_ql.reshape(d, MLA_HEADS, MLA_LATENT), w_qr.reshape(d, MLA_HEADS, MLA_ROPE)],
                         axis=2).reshape(d, MLA_HEADS * MLA_QK).T.astype(BF16)
    wkv = jnp.concatenate([w_c, w_kr, w_qi, w_ki, w_wi], axis=1).T
    wkv = jnp.pad(wkv, ((0, (-wkv.shape[0]) % 16), (0, 0))).astype(BF16)
    ckv_t, kr_t, qi_t, ki_t, wi_t = _proj_odd_kv(h, g, wkv, kv_norm.reshape(-1, 1), cos32, sin32,
                                                 cos16, sin16, bsz, seq)
    q = _proj_odd_q(h, g, wq, cos16, sin16, bsz, seq)
    tk = min(TQ_DSA, seq)
    ta = min(TA_DSA, seq)
    kcat = jnp.concatenate([ckv_t, kr_t], axis=1).transpose(0, 2, 1).reshape(bsz, seq // ta, ta, MLA_QK)
    ckt = ckv_t.reshape(bsz, MLA_LATENT, seq // ta, ta).transpose(0, 2, 1, 3)
    kidx = ki_t.transpose(0, 2, 1).reshape(bsz, seq // tk, tk, IDX_DIM)
    wuv_t = w_uv.transpose(0, 2, 1).astype(BF16)
    return _dsa(h, g, q, qi_t, wi_t, kcat, ckt, kidx, wuv_t, w_out.astype(BF16), seq)


def kernel(x, p, norm_g, ffn1_w_in, ffn1_w_out, ffn2_w_in, ffn2_w_out, ple_w_gate, ple_w_proj,
           ev_w_in, ev_w_out, s5_a_re, s5_a_im, s5_b_re, s5_b_im, s5_c_re, s5_c_im, s5_log_dt, s5_d,
           s5_w_glu, nsa_pe_k, nsa_pe_v, nsa_wk1, nsa_wk2, nsa_wv1, nsa_wv2,
           od_w_in, od_kv_norm, od_w_uv, od_w_out):
    bsz, seq, d = x.shape
    depth = norm_g.shape[0]
    tabs = {64: _rope_tables(seq, 64), 32: _rope_tables(seq, 32)}
    h = x.reshape(bsz * seq, d)
    for i in range(depth):
        g = norm_g[i]
        j = i // 2
        h = _ffn(h, g, ffn1_w_in[i].astype(BF16), ffn1_w_out[i].astype(BF16), 0)
        if i % 2 == 0:
            s5 = (s5_a_re[j], s5_a_im[j], s5_b_re[j], s5_b_im[j], s5_c_re[j], s5_c_im[j],
                  s5_log_dt[j], s5_d[j], s5_w_glu[j])
            h = _even_mixer(h, g, bsz, seq, ev_w_in[j], ev_w_out[j], s5, nsa_pe_k[j], nsa_pe_v[j],
                            nsa_wk1[j], nsa_wk2[j], nsa_wv1[j], nsa_wv2[j], tabs)
        else:
            h = _odd_mixer(h, g, bsz, seq, od_w_in[j], od_kv_norm[j], od_w_uv[j], od_w_out[j], tabs)
        h = _ffn_ple(h, p[i].reshape(bsz * seq, -1), g, ffn2_w_in[i].astype(BF16),
                     ffn2_w_out[i].astype(BF16), ple_w_gate[i].astype(BF16),
                     ple_w_proj[i].astype(BF16))
    return h.reshape(bsz, seq, d)
```

```python
import functools
import math

import jax
import jax.numpy as jnp
from jax import lax
from jax.experimental import pallas as pl
from jax.experimental.pallas import tpu as pltpu

F32 = jnp.float32
BF16 = jnp.bfloat16
I32 = jnp.int32

ROPE_THETA = 10000.0
EPS = 1e-6
NEG = -1e30
LOG2E = math.log2(math.e)
D_FF = 2816
S5_WIDTH = 512
S5_GROUP = 16
S5_GROUPS = S5_WIDTH // S5_GROUP
S5_STATE = 64
NSA_HEADS = 8
NSA_KV_HEADS = 2
NSA_GQA = NSA_HEADS // NSA_KV_HEADS
NSA_HEAD_DIM = 64
CMP_LEN = 32
CMP_STRIDE = 16
SEL_BLOCK = 64
SEL_TOP = 16
WINDOW = 512
FORCE_BONUS = 1000.0
NSA_Q = NSA_HEADS * NSA_HEAD_DIM
NSA_KV = NSA_KV_HEADS * NSA_HEAD_DIM
MLA_HEADS = 16
MLA_LATENT = 256
MLA_ROPE = 32
MLA_QK = MLA_LATENT + MLA_ROPE
MLA_V_DIM = 64
IDX_HEADS = 8
IDX_DIM = 64
DSA_TOP = 256

V7X_VMEM_BYTES = 64 * 2**20
VMEM_LIMIT = V7X_VMEM_BYTES - 8 * 2**20
LANE = 128
TM_FFN = 1024
TQ_PROJ = 256
TQ_NSA = 128
TK_NSA = 512
TQ_DSA = 256
TA_DSA = 512
S5_CHUNK = 64


def _cparams(sem):
    return pltpu.CompilerParams(dimension_semantics=sem, vmem_limit_bytes=VMEM_LIMIT)


def _resident(shape):
    nd = len(shape)
    return pl.BlockSpec(shape, lambda *_: (0,) * nd, pipeline_mode=pl.Buffered(1))


def _dot(a, b):
    return jnp.dot(a, b, preferred_element_type=F32)


def _dot_nt(a, b):
    return lax.dot_general(a, b, (((1,), (1,)), ((), ())), preferred_element_type=F32)


def _rms_rows(x, g):
    return x * lax.rsqrt(jnp.mean(x * x, axis=-1, keepdims=True) + EPS) * g


def _rope_fmaj(y, cos, sin):
    half = y.shape[1] // 2
    t1 = y[:, :half, :]
    t2 = y[:, half:, :]
    return jnp.concatenate([t1 * cos - t2 * sin, t2 * cos + t1 * sin], axis=1)


def _ffn_tile(x, g_ref, g0, win_ref, wout_ref, n_chunk):
    xn = _rms_rows(x, g_ref[g0:g0 + 1, :]).astype(BF16)
    ck = D_FF // n_chunk
    acc = None
    for c in range(n_chunk):
        a = _dot(xn, win_ref[:, c * ck:(c + 1) * ck])
        u = _dot(xn, win_ref[:, D_FF + c * ck:D_FF + (c + 1) * ck])
        act = (jax.nn.silu(a) * u).astype(BF16)
        y = _dot(act, wout_ref[c * ck:(c + 1) * ck, :])
        acc = y if acc is None else acc + y
    return x + 0.5 * _rms_rows(acc, g_ref[g0 + 1:g0 + 2, :])


def _ffn_kernel(h_ref, g_ref, win_ref, wout_ref, o_ref, *, g0, n_chunk):
    o_ref[...] = _ffn_tile(h_ref[...], g_ref, g0, win_ref, wout_ref, n_chunk)


def _ffn_ple_kernel(h_ref, p_ref, g_ref, win_ref, wout_ref, wg_ref, wp_ref, o_ref, *, n_chunk):
    x = _ffn_tile(h_ref[...], g_ref, 4, win_ref, wout_ref, n_chunk)
    gate = jax.nn.sigmoid(_dot(_rms_rows(x, g_ref[6:7, :]).astype(BF16), wg_ref[...]))
    e = _dot(p_ref[...].astype(BF16), wp_ref[...]) * gate
    o_ref[...] = x + _rms_rows(e, g_ref[7:8, :])


def _ffn_ple(h, p, g, w_in, w_out, w_gate, w_proj):
    t, d = h.shape
    tm = min(TM_FFN, t)
    return pl.pallas_call(
        functools.partial(_ffn_ple_kernel, n_chunk=2),
        grid=(t // tm,),
        in_specs=[pl.BlockSpec((tm, d), lambda i: (i, 0)),
                  pl.BlockSpec((tm, p.shape[1]), lambda i: (i, 0)), _resident(g.shape),
                  _resident(w_in.shape), _resident(w_out.shape), _resident(w_gate.shape),
                  _resident(w_proj.shape)],
        out_specs=pl.BlockSpec((tm, d), lambda i: (i, 0)),
        out_shape=jax.ShapeDtypeStruct((t, d), F32),
        compiler_params=_cparams(("parallel",)),
        name="ffn_ple",
    )(h, p, g, w_in, w_out, w_gate, w_proj)


def _ffn(h, g, w_in, w_out, g0):
    t, d = h.shape
    tm = min(TM_FFN, t)
    return pl.pallas_call(
        functools.partial(_ffn_kernel, g0=g0, n_chunk=2),
        grid=(t // tm,),
        in_specs=[pl.BlockSpec((tm, d), lambda i: (i, 0)), _resident(g.shape),
                  _resident(w_in.shape), _resident(w_out.shape)],
        out_specs=pl.BlockSpec((tm, d), lambda i: (i, 0)),
        out_shape=jax.ShapeDtypeStruct((t, d), F32),
        compiler_params=_cparams(("parallel",)),
        name="ffn",
    )(h, g, w_in, w_out)


def _outproj_even_kernel(h_ref, ys_ref, u_ref, b_ref, g_ref, d_ref, wglu_ref, wa_ref, wb_ref, o_ref):
    y = ys_ref[...] + d_ref[...] * u_ref[...]
    z = jax.nn.gelu(y)
    a = z * jax.nn.sigmoid(_dot(z.astype(BF16), wglu_ref[...]))
    mix = _dot(a.astype(BF16), wa_ref[...]) + _dot(b_ref[...].astype(BF16), wb_ref[...])
    o_ref[...] = h_ref[...] + _rms_rows(mix, g_ref[3:4, :])


def _outproj_even(h, ys, u, b_out, g, d_skip, w_glu, w_a, w_b):
    t, d = h.shape
    tm = min(TM_FFN, t)
    tok = lambda w: pl.BlockSpec((tm, w), lambda i: (i, 0))
    return pl.pallas_call(
        _outproj_even_kernel,
        grid=(t // tm,),
        in_specs=[tok(d), tok(S5_WIDTH), tok(S5_WIDTH), tok(NSA_Q), _resident(g.shape),
                  _resident(d_skip.shape), _resident(w_glu.shape), _resident(w_a.shape),
                  _resident(w_b.shape)],
        out_specs=tok(d),
        out_shape=jax.ShapeDtypeStruct((t, d), F32),
        compiler_params=_cparams(("parallel",)),
        name="outproj_even",
    )(h, ys, u, b_out, g, d_skip, w_glu, w_a, w_b)


def _proj_even_kernel(h_ref, g_ref, wu_ref, wt_ref, cos_ref, sin_ref,
                      u_ref, q_ref, k_ref, v_ref, gt_ref):
    xn = _rms_rows(h_ref[...], g_ref[2:3, :]).astype(BF16)
    u_ref[...] = _dot(xn, wu_ref[...])
    y = _dot_nt(wt_ref[...], xn)
    tq = y.shape[1]
    cos = cos_ref[...]
    sin = sin_ref[...]
    hd = NSA_HEAD_DIM
    q = _rope_fmaj(y[0:NSA_Q].reshape(NSA_HEADS, hd, tq), cos, sin)
    q_ref[...] = (q * (hd ** -0.5 * LOG2E)).reshape(NSA_Q, tq)
    k0 = NSA_Q
    nk = 3 * NSA_KV
    k = _rope_fmaj(y[k0:k0 + nk].reshape(3 * NSA_KV_HEADS, hd, tq), cos, sin)
    k_ref[...] = k.reshape(nk, tq).astype(k_ref.dtype)
    v0 = k0 + nk
    v_ref[...] = y[v0:v0 + nk].astype(v_ref.dtype)
    g0 = v0 + nk
    gt_ref[...] = jax.nn.sigmoid(y[g0:g0 + 3 * NSA_HEADS])


def _proj_even(h, g, wu, wt, cos, sin, bsz, seq):
    d = h.shape[1]
    tq = min(TQ_PROJ, seq)
    nq = seq // tq
    nk = 3 * NSA_KV
    fm = lambda rows: pl.BlockSpec((None, rows, tq), lambda b, i: (b, 0, i))
    return pl.pallas_call(
        _proj_even_kernel,
        grid=(bsz, nq),
        in_specs=[pl.BlockSpec((tq, d), lambda b, i: (b * nq + i, 0)), _resident(g.shape),
                  _resident(wu.shape), _resident(wt.shape),
                  pl.BlockSpec((cos.shape[0], tq), lambda b, i: (0, i)),
                  pl.BlockSpec((sin.shape[0], tq), lambda b, i: (0, i))],
        out_specs=[pl.BlockSpec((tq, S5_WIDTH), lambda b, i: (b * nq + i, 0)),
                   fm(NSA_Q), fm(nk), fm(nk), fm(3 * NSA_HEADS)],
        out_shape=[jax.ShapeDtypeStruct((bsz * seq, S5_WIDTH), F32),
                   jax.ShapeDtypeStruct((bsz, NSA_Q, seq), F32),
                   jax.ShapeDtypeStruct((bsz, nk, seq), BF16),
                   jax.ShapeDtypeStruct((bsz, nk, seq), BF16),
                   jax.ShapeDtypeStruct((bsz, 3 * NSA_HEADS, seq), F32)],
        compiler_params=_cparams(("parallel", "parallel")),
        name="proj_even",
    )(h, g, wu, wt, cos, sin)


def _proj_odd_kv_kernel(h_ref, g_ref, wt_ref, kvn_ref, cos32_ref, sin32_ref, cos16_ref, sin16_ref,
                        ckv_ref, kr_ref, qi_ref, ki_ref, wi_ref):
    xn = _rms_rows(h_ref[...], g_ref[2:3, :]).astype(BF16)
    y = _dot_nt(wt_ref[...], xn)
    tq = y.shape[1]
    c = y[0:MLA_LATENT]
    c = c * lax.rsqrt(jnp.mean(c * c, axis=0, keepdims=True) + EPS) * kvn_ref[...]
    ckv_ref[...] = c.astype(ckv_ref.dtype)
    r0 = MLA_LATENT
    kr = _rope_fmaj(y[r0:r0 + MLA_ROPE].reshape(1, MLA_ROPE, tq), cos16_ref[...], sin16_ref[...])
    kr_ref[...] = kr.reshape(MLA_ROPE, tq).astype(kr_ref.dtype)
    q0 = r0 + MLA_ROPE
    nqi = IDX_HEADS * IDX_DIM
    qi = _rope_fmaj(y[q0:q0 + nqi].reshape(IDX_HEADS, IDX_DIM, tq), cos32_ref[...], sin32_ref[...])
    qi_ref[...] = qi.reshape(nqi, tq).astype(qi_ref.dtype)
    k0 = q0 + nqi
    ki = _rope_fmaj(y[k0:k0 + IDX_DIM].reshape(1, IDX_DIM, tq), cos32_ref[...], sin32_ref[...])
    ki_ref[...] = ki.reshape(IDX_DIM, tq).astype(ki_ref.dtype)
    w0 = k0 + IDX_DIM
    wi_ref[...] = y[w0:w0 + IDX_HEADS]


def _proj_odd_kv(h, g, wt, kvn, cos32, sin32, cos16, sin16, bsz, seq):
    d = h.shape[1]
    tq = min(TQ_PROJ, seq)
    nq = seq // tq
    fm = lambda rows: pl.BlockSpec((None, rows, tq), lambda b, i: (b, 0, i))
    tab = lambda a: pl.BlockSpec((a.shape[0], tq), lambda b, i: (0, i))
    nqi = IDX_HEADS * IDX_DIM
    return pl.pallas_call(
        _proj_odd_kv_kernel,
        grid=(bsz, nq),
        in_specs=[pl.BlockSpec((tq, d), lambda b, i: (b * nq + i, 0)), _resident(g.shape),
                  _resident(wt.shape), _resident(kvn.shape),
                  tab(cos32), tab(sin32), tab(cos16), tab(sin16)],
        out_specs=[fm(MLA_LATENT), fm(MLA_ROPE), fm(nqi), fm(IDX_DIM), fm(IDX_HEADS)],
        out_shape=[jax.ShapeDtypeStruct((bsz, MLA_LATENT, seq), BF16),
                   jax.ShapeDtypeStruct((bsz, MLA_ROPE, seq), BF16),
                   jax.ShapeDtypeStruct((bsz, nqi, seq), BF16),
                   jax.ShapeDtypeStruct((bsz, IDX_DIM, seq), BF16),
                   jax.ShapeDtypeStruct((bsz, IDX_HEADS, seq), F32)],
        compiler_params=_cparams(("parallel", "parallel")),
        name="proj_odd_kv",
    )(h, g, wt, kvn, cos32, sin32, cos16, sin16)


def _proj_odd_q_kernel(h_ref, g_ref, wt_ref, cos16_ref, sin16_ref, q_ref):
    xn = _rms_rows(h_ref[...], g_ref[2:3, :]).astype(BF16)
    tq = xn.shape[0]
    cos = cos16_ref[...]
    sin = sin16_ref[...]
    for hh in range(MLA_HEADS):
        y = _dot_nt(wt_ref[hh * MLA_QK:(hh + 1) * MLA_QK, :], xn)
        r = _rope_fmaj(y[MLA_LATENT:].reshape(1, MLA_ROPE, tq), cos, sin)
        q = jnp.concatenate([y[:MLA_LATENT], r.reshape(MLA_ROPE, tq)], axis=0)
        q_ref[:, hh * tq:(hh + 1) * tq] = (q * (MLA_QK ** -0.5 * LOG2E)).astype(q_ref.dtype)


def _proj_odd_q(h, g, wt, cos16, sin16, bsz, seq):
    d = h.shape[1]
    tq = min(TQ_DSA, seq)
    nq = seq // tq
    return pl.pallas_call(
        _proj_odd_q_kernel,
        grid=(bsz, nq),
        in_specs=[pl.BlockSpec((tq, d), lambda b, i: (b * nq + i, 0)), _resident(g.shape),
                  _resident(wt.shape),
                  pl.BlockSpec((cos16.shape[0], tq), lambda b, i: (0, i)),
                  pl.BlockSpec((sin16.shape[0], tq), lambda b, i: (0, i))],
        out_specs=pl.BlockSpec((None, None, MLA_QK, MLA_HEADS * tq), lambda b, i: (b, i, 0, 0)),
        out_shape=jax.ShapeDtypeStruct((bsz, nq, MLA_QK, MLA_HEADS * tq), BF16),
        compiler_params=_cparams(("parallel", "parallel")),
        name="proj_odd_q",
    )(h, g, wt, cos16, sin16)


def _s5_state_kernel(u_ref, bc_ref, s_ref):
    s_ref[...] = _dot(u_ref[...], bc_ref[...])


def _s5_scan_kernel(sr_ref, si_ref, lr_ref, li_ref, xr_ref, xi_ref):
    n_chunk = sr_ref.shape[0]
    lr = lr_ref[...]
    li = li_ref[...]

    def body(c, carry):
        xr, xi = carry
        xr_ref[c] = xr
        xi_ref[c] = xi
        return (lr * xr - li * xi + sr_ref[c], lr * xi + li * xr + si_ref[c])

    zero = jnp.zeros(sr_ref.shape[1:], F32)
    lax.fori_loop(0, n_chunk, body, (zero, zero))


def _s5_out_kernel(u_ref, x_ref, m_ref, cc_ref, y_ref):
    x = x_ref[...]
    hi = x.astype(BF16)
    lo = (x - hi.astype(F32)).astype(BF16)
    cc = cc_ref[...]
    y_ref[...] = _dot(u_ref[...], m_ref[...]) + _dot(hi, cc) + _dot(lo, cc)


def _s5_matrices(a_re, a_im, b_re, b_im, c_re, c_im, log_dt, tc):
    hp = lax.Precision.HIGHEST
    dt = jnp.exp(log_dt)[:, None]
    lam = lax.complex(a_re, a_im)
    lam_dt = lam * dt
    lam_bar = jnp.exp(lam_dt)
    b_bar = ((lam_bar - 1.0) / lam)[..., None] * lax.complex(b_re, b_im)
    k = jnp.arange(tc + 1, dtype=F32)[:, None, None]
    pw = jnp.exp(lam_dt[None] * k)
    pr, pi = jnp.real(pw), jnp.imag(pw)
    bbr, bbi = jnp.real(b_bar), jnp.imag(b_bar)
    cpr = c_re[None] * pr[:, :, None, :] - c_im[None] * pi[:, :, None, :]
    cpi = c_re[None] * pi[:, :, None, :] + c_im[None] * pr[:, :, None, :]
    kk = (jnp.einsum('kgpn,gnq->kgpq', cpr[:tc], bbr, precision=hp)
          - jnp.einsum('kgpn,gnq->kgpq', cpi[:tc], bbi, precision=hp))
    n_g, n_p = a_re.shape[0], b_re.shape[2]
    kq = jnp.pad(kk.transpose(1, 3, 0, 2).astype(BF16), ((0, 0), (0, 0), (tc, 0), (0, 0)))
    lag = tc + jnp.arange(tc)[None, :] - jnp.arange(tc)[:, None]
    m = kq[:, :, lag, :].reshape(n_g, n_p * tc, tc * n_p)
    rev = pw[tc - 1 - jnp.arange(tc)]
    bc = rev[:, :, :, None] * b_bar[None]
    bc = bc.transpose(1, 3, 0, 2).reshape(n_g, n_p * tc, -1)
    bc = jnp.concatenate([jnp.real(bc), jnp.imag(bc)], axis=-1)
    mr = cpr[1:tc + 1].transpose(1, 3, 0, 2).reshape(n_g, -1, tc * n_p)
    mi = cpi[1:tc + 1].transpose(1, 3, 0, 2).reshape(n_g, -1, tc * n_p)
    cc = jnp.concatenate([mr, -mi], axis=1)
    ltc = pw[tc].reshape(1, -1)
    return m, bc.astype(BF16), cc.astype(BF16), jnp.real(ltc), jnp.imag(ltc)


def _s5_scan(u, mats, bsz, seq):
    m, bc, cc, lr, li = mats
    tc = S5_CHUNK
    n_c = seq // tc
    n_g, n_p, n_s = S5_GROUPS, S5_GROUP, S5_STATE
    rows = bsz * n_c
    kd = tc * n_p
    ug = (u.reshape(bsz, n_c, tc, n_g, n_p).transpose(3, 0, 1, 4, 2)
          .reshape(n_g, rows, kd).astype(BF16))
    grp = lambda a, b: pl.BlockSpec((None, a, b), lambda gi: (gi, 0, 0))
    s = pl.pallas_call(
        _s5_state_kernel,
        grid=(n_g,),
        in_specs=[grp(rows, kd), grp(kd, 2 * n_s)],
        out_specs=grp(rows, 2 * n_s),
        out_shape=jax.ShapeDtypeStruct((n_g, rows, 2 * n_s), F32),
        compiler_params=_cparams(("parallel",)),
        name="s5_state",
    )(ug, bc)
    s5 = s.reshape(n_g, bsz, n_c, 2, n_s).transpose(3, 2, 1, 0, 4).reshape(2, n_c, bsz, n_g * n_s)
    full = lambda shp: pl.BlockSpec(shp, lambda: (0,) * len(shp))
    xr, xi = pl.pallas_call(
        _s5_scan_kernel,
        in_specs=[full(s5.shape[1:]), full(s5.shape[1:]), full(lr.shape), full(li.shape)],
        out_specs=[full(s5.shape[1:]), full(s5.shape[1:])],
        out_shape=[jax.ShapeDtypeStruct(s5.shape[1:], F32)] * 2,
        compiler_params=pltpu.CompilerParams(vmem_limit_bytes=VMEM_LIMIT),
        name="s5_scan",
    )(s5[0], s5[1], lr, li)
    x = jnp.stack([xr, xi]).reshape(2, n_c, bsz, n_g, n_s).transpose(3, 2, 1, 0, 4)
    x = x.reshape(n_g, rows, 2 * n_s)
    y = pl.pallas_call(
        _s5_out_kernel,
        grid=(n_g,),
        in_specs=[grp(rows, kd), grp(rows, 2 * n_s), grp(kd, kd), grp(2 * n_s, kd)],
        out_specs=grp(rows, kd),
        out_shape=jax.ShapeDtypeStruct((n_g, rows, kd), F32),
        compiler_params=_cparams(("parallel",)),
        name="s5_out",
    )(ug, x, m, cc)
    return (y.reshape(n_g, bsz, n_c, tc, n_p).transpose(1, 2, 3, 0, 4)
            .reshape(bsz * seq, n_g * n_p))


def _compress_kernel(x_ref, pe_ref, w1_ref, w2_ref, o_ref):
    x = x_ref[...].astype(F32)
    nb = x.shape[0]
    a = _dot((x + pe_ref[0:1, :]).astype(BF16), w1_ref[0])
    b = _dot((x + pe_ref[1:2, :]).astype(BF16), w1_ref[1])
    pre = a + pltpu.roll(b, nb - 1, 0)
    o_ref[...] = _dot(jax.nn.gelu(pre).astype(BF16), w2_ref[...]).astype(o_ref.dtype)


def _compress(x, pe, w1, w2):
    bsz, hk, nb, kd = x.shape
    hd = w2.shape[1]
    return pl.pallas_call(
        _compress_kernel,
        grid=(bsz, hk),
        in_specs=[pl.BlockSpec((None, None, nb, kd), lambda b, h: (b, h, 0, 0)),
                  _resident(pe.shape), _resident(w1.shape), _resident(w2.shape)],
        out_specs=pl.BlockSpec((None, None, nb, hd), lambda b, h: (b, h, 0, 0)),
        out_shape=jax.ShapeDtypeStruct((bsz, hk, nb, hd), BF16),
        compiler_params=_cparams(("parallel", "parallel")),
        name="nsa_compress",
    )(x, pe, w1, w2)


def _nsa_kernel(q_ref, g_ref, kc_ref, vct_ref, ks_ref, vst_ref, kw_ref, vwt_ref, ov_ref,
                o_ref, sel_ref, zero_ref, sa_ref, sb_ref, *, tq, tk, top_n):
    gq, hd = NSA_GQA, NSA_HEAD_DIM
    n_sb = sel_ref.shape[0]
    nb = kc_ref.shape[0]
    wd = gq * tq
    i = pl.program_id(2)
    s0 = i * tq
    q = q_ref[...]
    qt = jnp.concatenate([q[g * hd:(g + 1) * hd, :] for g in range(gq)], axis=1).astype(BF16)
    t1 = s0 + lax.broadcasted_iota(I32, (1, tq), 1)
    t4 = s0 + (lax.broadcasted_iota(I32, (1, wd), 1) & (tq - 1))

    sc = _dot(kc_ref[...], qt)
    c_last = lax.broadcasted_iota(I32, (nb, 1), 0) * CMP_STRIDE + (CMP_LEN - 1)
    bias_c = jnp.where(c_last <= t1, 0.0, NEG)
    sm = sc + jnp.concatenate([bias_c] * gq, axis=1)
    e = jnp.exp2(sm - jnp.max(sm, axis=0, keepdims=True))
    p_c = e * jnp.where(t4 >= CMP_LEN - 1, 1.0 / jnp.sum(e, axis=0, keepdims=True), 0.0)
    o_c = _dot(vct_ref[...], p_c.astype(BF16))
    psum = p_c[:, 0:tq]
    for g in range(1, gq):
        psum = psum + p_c[:, g * tq:(g + 1) * tq]
    hi = psum.astype(BF16)
    lo = (psum - hi.astype(F32)).astype(BF16)
    imp = _dot(ov_ref[...], hi) + _dot(ov_ref[...], lo)
    blk = lax.broadcasted_iota(I32, (n_sb, 1), 0)
    cur = t1 >> int(math.log2(SEL_BLOCK))
    imp = imp + jnp.where(blk == 0, FORCE_BONUS,
                          jnp.where(blk == cur, FORCE_BONUS, jnp.where(blk == cur - 1, FORCE_BONUS, 0.0)))
    imp = jnp.where(blk * SEL_BLOCK <= t1, imp, NEG)

    rowf = lax.broadcasted_iota(I32, (n_sb, tq), 0).astype(F32)
    sel = jnp.zeros((n_sb, tq), F32)
    x = imp
    for _ in range(top_n):
        mx = jnp.max(x, axis=0, keepdims=True)
        first = jnp.min(jnp.where(x == mx, rowf, float(n_sb)), axis=0, keepdims=True)
        hit = rowf == first
        sel = jnp.where(hit, 1.0, sel)
        x = jnp.where(hit, -jnp.inf, x)
    sel_ref[...] = sel

    zero_ref[...] = jnp.zeros(zero_ref.shape, F32)
    init = (jnp.full((1, wd), NEG, F32), jnp.zeros((1, wd), F32), zero_ref[...])

    bpt = tk // SEL_BLOCK
    key_row = lax.broadcasted_iota(I32, (tk, 1), 0)

    last_tile = ks_ref.shape[0] - 1

    def scores(j):
        return _dot(ks_ref[jnp.minimum(j, last_tile)], qt)

    def consume(j, s_ref, carry):
        m, l, acc, pv = carry
        jc = jnp.minimum(j, last_tile)
        rows = [jnp.broadcast_to(sel_ref[pl.ds(jc * bpt + b, 1), :], (SEL_BLOCK, tq))
                for b in range(bpt)]
        causal = jnp.where(j * tk + key_row <= t1, 0.0, NEG)
        bias = jnp.where(jnp.concatenate(rows, axis=0) > 0.5, causal, NEG)
        s = s_ref[...] + jnp.concatenate([bias] * gq, axis=1)
        m_new = jnp.maximum(m, jnp.max(s, axis=0, keepdims=True))
        alpha = jnp.exp2(m - m_new)
        p = jnp.exp2(s - m_new)
        l = alpha * l + jnp.sum(p, axis=0, keepdims=True)
        return m_new, l, alpha * (acc + pv), _dot(vst_ref[jc], p.astype(BF16))

    def tile_pair(k, carry):
        j = 2 * k
        sb_ref[...] = scores(j + 1)
        carry = consume(j, sa_ref, carry)
        sa_ref[...] = scores(j + 2)
        return consume(j + 1, sb_ref, carry)

    sa_ref[...] = scores(0)
    _, l_s, a_s, pv_s = lax.fori_loop(0, (s0 // tk + 2) // 2, tile_pair, init + (zero_ref[...],))
    o_s = (a_s + pv_s) / l_s

    n_wt = (WINDOW + tq) // LANE
    j_lo = jnp.clip(i * (tq // LANE) - WINDOW // LANE, 0, kw_ref.shape[0] - n_wt)
    kwin = jnp.concatenate([kw_ref[j_lo + w] for w in range(n_wt)], axis=0)
    vwin = jnp.concatenate([vwt_ref[j_lo + w] for w in range(n_wt)], axis=1)
    sw = _dot(kwin, qt)
    diff = t1 - (j_lo * LANE + lax.broadcasted_iota(I32, (n_wt * LANE, 1), 0))
    bias_w = jnp.where(diff >= 0, jnp.where(diff < WINDOW, 0.0, NEG), NEG)
    sw = sw + jnp.concatenate([bias_w] * gq, axis=1)
    pw = jnp.exp2(sw - jnp.max(sw, axis=0, keepdims=True))
    o_w = _dot(vwin, pw.astype(BF16)) / jnp.sum(pw, axis=0, keepdims=True)

    gt = g_ref[...]
    outs = []
    for g in range(gq):
        cs = slice(g * tq, (g + 1) * tq)
        outs.append(gt[3 * g:3 * g + 1, :] * o_c[:, cs] + gt[3 * g + 1:3 * g + 2, :] * o_s[:, cs]
                    + gt[3 * g + 2:3 * g + 3, :] * o_w[:, cs])
    o_ref[...] = jnp.concatenate(outs, axis=0).T


def _nsa(q, gates, kcmp, vcmp_t, ks, vs_t, kw, vw_t, ov_t, seq):
    bsz = q.shape[0]
    tq = min(TQ_NSA, seq)
    tk = min(TK_NSA, seq)
    n_sb = seq // SEL_BLOCK
    gq, hd = NSA_GQA, NSA_HEAD_DIM
    per_head = lambda a: pl.BlockSpec((None, None) + a.shape[2:],
                                      lambda b, h, i: (b, h) + (0,) * (a.ndim - 2),
                                      pipeline_mode=pl.Buffered(1))
    return pl.pallas_call(
        functools.partial(_nsa_kernel, tq=tq, tk=tk, top_n=min(SEL_TOP, n_sb)),
        grid=(bsz, NSA_KV_HEADS, seq // tq),
        in_specs=[pl.BlockSpec((None, gq * hd, tq), lambda b, h, i: (b, h, i)),
                  pl.BlockSpec((None, None, 3 * gq, tq), lambda b, h, i: (b, h, 0, i)),
                  per_head(kcmp), per_head(vcmp_t), per_head(ks), per_head(vs_t),
                  per_head(kw), per_head(vw_t), _resident(ov_t.shape)],
        out_specs=pl.BlockSpec((tq, gq * hd), lambda b, h, i: (b * (seq // tq) + i, h)),
        out_shape=jax.ShapeDtypeStruct((bsz * seq, NSA_Q), F32),
        scratch_shapes=[pltpu.VMEM((n_sb, tq), F32), pltpu.VMEM((hd, gq * tq), F32),
                        pltpu.VMEM((tk, gq * tq), F32), pltpu.VMEM((tk, gq * tq), F32)],
        compiler_params=_cparams(("parallel", "parallel", "arbitrary")),
        name="nsa",
    )(q, gates, kcmp, vcmp_t, ks, vs_t, kw, vw_t, ov_t)


def _dsa_kernel(h_ref, g_ref, q_ref, qi_ref, wi_ref, kcat_ref, ckt_ref, kidx_ref, wuv_ref, wout_ref,
                o_ref, ihi, ilo, acc_ref, m_ref, l_ref, tie_cap, *, tq, top):
    tk = tq
    n_h = MLA_HEADS
    i16 = jnp.int16
    i = pl.program_id(1)
    row = lax.broadcasted_iota(I32, (tk, 1), 0)
    col = lax.broadcasted_iota(I32, (1, tq), 1)
    causal = row <= col

    def tile(ref, j):
        return ref.at[pl.ds(pl.multiple_of(j * tk, tk), tk), :]

    qi = qi_ref[...]
    wi = wi_ref[...] * (IDX_HEADS ** -0.5) * (IDX_DIM ** -0.5)

    def idx_scores(j):
        kj = kidx_ref[j]
        sc = jnp.zeros((tk, tq), F32)
        for h in range(IDX_HEADS):
            d = _dot(kj, qi[h * IDX_DIM:(h + 1) * IDX_DIM, :])
            sc = sc + wi[h:h + 1, :] * jnp.maximum(d, 0.0)
        return sc

    def store_keys(j, sc):
        b = pltpu.bitcast(sc, I32)
        key = b ^ ((b >> 31) & 0x7FFFFFFF)
        tile(ihi, j)[...] = (key >> 16).astype(i16)
        tile(ilo, j)[...] = ((key & 0xFFFF) - 32768).astype(i16)

    def idx_tile(j, _):
        store_keys(j, idx_scores(j))
        return 0

    lax.fori_loop(0, i, idx_tile, 0)
    store_keys(i, jnp.where(causal, idx_scores(i), NEG))

    @pl.when(((i + 1) * tq) % kcat_ref.shape[1] != 0)
    def _():
        store_keys(i + 1, jnp.full((tk, tq), NEG, F32))

    def count(ref, pred):
        def body(j, c):
            hit = jnp.where(pred(tile(ref, j)[...]), jnp.ones((), i16), jnp.zeros((), i16))
            parts = [hit[k * 16:(k + 1) * 16] for k in range(tk // 16)]
            while len(parts) > 1:
                parts = [a + b for a, b in zip(parts[0::2], parts[1::2])]
            return c + parts[0]
        c16 = lax.fori_loop(0, i + 1, body, jnp.zeros((16, tq), i16))
        return jnp.sum(c16.astype(I32), axis=0, keepdims=True)

    def search(ref, need, n_all):
        def bit_step(b, st):
            thr, c_ge, c_gt = st
            cand = thr + (jnp.int32(1) << (15 - b))
            cand16 = cand.astype(i16)
            cnt = count(ref, lambda t: t >= cand16)
            ok = cnt >= need
            return jnp.where(ok, cand, thr), jnp.where(ok, cnt, c_ge), jnp.where(ok, c_gt, cnt)
        init = (jnp.full((1, tq), -32768, I32), n_all, jnp.zeros((1, tq), I32))
        return lax.fori_loop(0, 16, bit_step, init)

    thr_hi, n_ge_hi, n_gt_hi = search(ihi, top, jnp.full((1, tq), 1, I32) * ((i + 1) * tk))
    thr_hi16 = thr_hi.astype(i16)
    need = top - n_gt_hi
    n_eq_hi = n_ge_hi - n_gt_hi

    def mask_lo(j, _):
        lo = tile(ilo, j)
        lo[...] = jnp.where(tile(ihi, j)[...] == thr_hi16, lo[...], jnp.full((), -32768, i16))
        return 0

    lax.fori_loop(0, i + 1, mask_lo, 0)
    thr_lo, n_ge_lo, _ = search(ilo, need, n_eq_hi)

    n_sel = n_gt_hi + n_ge_lo
    tie_cap[...] = jnp.full((1, tq), ihi.shape[0], I32)
    excess = n_sel - top
    max_excess = jnp.max(excess)
    n_bits = (ihi.shape[0] - 1).bit_length()

    thr_lo16 = thr_lo.astype(i16)

    def tie_diff(j):
        return ((tile(ihi, j)[...] ^ thr_hi16) | (tile(ilo, j)[...] ^ thr_lo16)).astype(I32)

    n_top = 3
    max_scans = 2 * n_bits // n_top

    @pl.when((max_excess > 0) & (max_excess <= n_top * max_scans))
    def _():
        def drop(_, st):
            cap, left = st

            def body(j, tops):
                idx = j * tk + row
                v = jnp.where(tie_diff(j) == 0, jnp.where(idx <= cap, idx, -1), -1)
                v = v.reshape(tk // 8, 8, tq)
                for k in range(tk // 8):
                    t = v[k]
                    new = []
                    for m in tops:
                        new.append(jnp.maximum(m, t))
                        t = jnp.minimum(m, t)
                    tops = tuple(new)
                return tops

            tops = list(lax.fori_loop(0, i + 1, body, (jnp.full((8, tq), -1, I32),) * n_top))
            take = jnp.minimum(left, n_top)
            for r in range(n_top):
                g = jnp.max(functools.reduce(jnp.maximum, tops), axis=0, keepdims=True)
                cap = jnp.where(take == r + 1, g - 1, cap)
                tops = [jnp.where(m == g, -1, m) for m in tops]
            return cap, left - take

        n_scan = (max_excess + n_top - 1) // n_top
        tie_cap[...] = lax.fori_loop(0, n_scan, drop, (tie_cap[...], excess))[0]

    @pl.when(max_excess > n_top * max_scans)
    def _():
        def count_ties(below):
            def body(j, c):
                hit = jnp.where(tie_diff(j) == 0, jnp.where(j * tk + row < below, 1, 0), 0)
                return c + jnp.sum(hit.reshape(tk // 8, 8, tq), axis=0)
            c8 = lax.fori_loop(0, i + 1, body, jnp.zeros((8, tq), I32))
            return jnp.sum(c8, axis=0, keepdims=True)

        need_eq = count_ties(ihi.shape[0]) - excess

        def bit_step(b, cap):
            cand = cap + (jnp.int32(1) << (n_bits - 1 - b))
            return jnp.where(count_ties(cand) >= need_eq, cap, cand)

        tie_cap[...] = lax.fori_loop(0, n_bits, bit_step, jnp.zeros((1, tq), I32))

    m_ref[...] = jnp.full(m_ref.shape, NEG, F32)
    l_ref[...] = jnp.zeros(l_ref.shape, F32)
    acc_ref[...] = jnp.zeros(acc_ref.shape, F32)
    cap = tie_cap[...]
    thr_key = (thr_hi << 16) | (thr_lo + 32768)

    ta = kcat_ref.shape[1]
    a_row = lax.broadcasted_iota(I32, (ta, 1), 0)
    q_pos = i * tq + col

    def att_tile(j, _):
        rows = pl.ds(pl.multiple_of(j * ta, ta), ta)
        hi = ihi[rows, :].astype(I32)
        lo = ilo[rows, :].astype(I32)
        k_pos = j * ta + a_row
        key = (hi << 16) | (lo + 32768)
        sel = jnp.where(key > thr_key, 0.0,
                        jnp.where(key == thr_key, jnp.where(k_pos <= cap, 0.0, NEG), NEG))
        bias = jnp.where(k_pos <= q_pos, sel, NEG)
        s_all = _dot(kcat_ref[j], q_ref[...])
        ps, alphas = [], []
        for h in range(n_h):
            cs = slice(h * tq, (h + 1) * tq)
            s = s_all[:, cs] + bias
            m_old = m_ref[:, cs]
            m_new = jnp.maximum(m_old, jnp.max(s, axis=0, keepdims=True))
            alpha = jnp.exp2(m_old - m_new)
            p = jnp.exp2(s - m_new)
            l_ref[:, cs] = alpha * l_ref[:, cs] + jnp.sum(p, axis=0, keepdims=True)
            m_ref[:, cs] = m_new
            ps.append(p.astype(BF16))
            alphas.append(alpha)
        p_all = jnp.concatenate(ps, axis=1)
        acc_ref[...] = acc_ref[...] * jnp.concatenate(alphas, axis=1) + _dot(ckt_ref[j], p_all)
        return 0

    lax.fori_loop(0, (i * tq) // ta + 1, att_tile, 0)

    outs = []
    for h in range(n_h):
        cs = slice(h * tq, (h + 1) * tq)
        o = acc_ref[:, cs] / l_ref[:, cs]
        outs.append(_dot(wuv_ref[h], o.astype(BF16)))
    mix = jnp.concatenate(outs, axis=0).T
    y = _dot(mix.astype(BF16), wout_ref[...])
    o_ref[...] = h_ref[...] + _rms_rows(y, g_ref[3:4, :])


def _dsa(h, g, q, qi, wi, kcat, ckt, kidx, wuv_t, w_out, seq):
    bsz = q.shape[0]
    d = h.shape[1]
    tq = min(TQ_DSA, seq)
    nq = seq // tq
    top = min(DSA_TOP, seq // 4)
    wd = MLA_HEADS * tq
    per_b = lambda a: pl.BlockSpec((None,) + a.shape[1:], lambda b, i: (b,) + (0,) * (a.ndim - 1),
                                   pipeline_mode=pl.Buffered(1))
    return pl.pallas_call(
        functools.partial(_dsa_kernel, tq=tq, top=top),
        grid=(bsz, nq),
        in_specs=[pl.BlockSpec((tq, d), lambda b, i: (b * nq + i, 0)), _resident(g.shape),
                  pl.BlockSpec((None, None, MLA_QK, wd), lambda b, i: (b, i, 0, 0)),
                  pl.BlockSpec((None, IDX_HEADS * IDX_DIM, tq), lambda b, i: (b, 0, i)),
                  pl.BlockSpec((None, IDX_HEADS, tq), lambda b, i: (b, 0, i)),
                  per_b(kcat), per_b(ckt), per_b(kidx), _resident(wuv_t.shape),
                  _resident(w_out.shape)],
        out_specs=pl.BlockSpec((tq, d), lambda b, i: (b * nq + i, 0)),
        out_shape=jax.ShapeDtypeStruct(h.shape, F32),
        scratch_shapes=[pltpu.VMEM((seq, tq), jnp.int16),
                        pltpu.VMEM((seq, tq), jnp.int16),
                        pltpu.VMEM((MLA_LATENT, wd), F32),
                        pltpu.VMEM((1, wd), F32),
                        pltpu.VMEM((1, wd), F32),
                        pltpu.VMEM((1, tq), I32)],
        compiler_params=_cparams(("parallel", "arbitrary")),
        name="dsa",
    )(h, g, q, qi, wi, kcat, ckt, kidx, wuv_t, w_out)


def _rope_tables(seq, dim):
    half = dim // 2
    inv = ROPE_THETA ** (-jnp.arange(half, dtype=F32) / half)
    ang = inv[:, None] * jnp.arange(seq, dtype=F32)[None, :]
    return jnp.cos(ang), jnp.sin(ang)


def _even_mixer(h, g, bsz, seq, w_in, w_out, s5, pe_k, pe_v, wk1, wk2, wv1, wv2, tabs):
    cos32, sin32 = tabs[64]
    hd, hk = NSA_HEAD_DIM, NSA_KV_HEADS
    o = 0
    cols = {}
    for name, size in (("u", S5_WIDTH), ("q", NSA_Q), ("kc", NSA_KV), ("vc", NSA_KV), ("ks", NSA_KV),
                       ("vs", NSA_KV), ("kw", NSA_KV), ("vw", NSA_KV), ("gt", 3 * NSA_HEADS)):
        cols[name] = w_in[:, o:o + size]
        o += size
    wu = cols["u"].astype(BF16)
    wt = jnp.concatenate([cols[n] for n in ("q", "kc", "ks", "kw", "vc", "vs", "vw", "gt")], axis=1).T
    wt = jnp.pad(wt, ((0, (-wt.shape[0]) % 16), (0, 0))).astype(BF16)
    u, q_t, k_t, v_t, gt_t = _proj_even(h, g, wu, wt, cos32, sin32, bsz, seq)

    a_re, a_im, b_re, b_im, c_re, c_im, log_dt, d_skip, w_glu = s5
    mats = _s5_matrices(a_re, a_im, b_re, b_im, c_re, c_im, log_dt, S5_CHUNK)
    ys = _s5_scan(u, mats, bsz, seq)

    k5 = k_t.reshape(bsz, 3, hk, hd, seq)
    v5 = v_t.reshape(bsz, 3, hk, hd, seq)
    nb = seq // CMP_STRIDE
    half_blk = lambda a: (a.reshape(bsz, hk, hd, nb, CMP_STRIDE).transpose(0, 1, 3, 4, 2)
                          .reshape(bsz, hk, nb, CMP_STRIDE * hd))
    pe2 = lambda pe: pe.reshape(2, CMP_STRIDE * hd)
    w1s = lambda w: w.reshape(2, CMP_STRIDE * hd, hd).astype(BF16)
    kcmp = _compress(half_blk(k5[:, 0]), pe2(pe_k), w1s(wk1), wk2.astype(BF16))
    vcmp = _compress(half_blk(v5[:, 0]), pe2(pe_v), w1s(wv1), wv2.astype(BF16))
    vcmp_t = vcmp.transpose(0, 1, 3, 2)
    tk = min(TK_NSA, seq)
    ks = k5[:, 1].reshape(bsz, hk, hd, seq // tk, tk).transpose(0, 1, 3, 4, 2)
    vs_t = v5[:, 1].reshape(bsz, hk, hd, seq // tk, tk).transpose(0, 1, 3, 2, 4)
    kw = k5[:, 2].reshape(bsz, hk, hd, seq // LANE, LANE).transpose(0, 1, 3, 4, 2)
    vw_t = v5[:, 2].reshape(bsz, hk, hd, seq // LANE, LANE).transpose(0, 1, 3, 2, 4)
    n_sb = seq // SEL_BLOCK
    c_start = jnp.arange(nb) * CMP_STRIDE
    b_start = jnp.arange(n_sb) * SEL_BLOCK
    ov_t = ((c_start[None, :] < b_start[:, None] + SEL_BLOCK)
            & (c_start[None, :] + CMP_LEN > b_start[:, None])
            & (jnp.arange(nb)[None, :] < nb - 1)).astype(BF16)
    gates = gt_t.reshape(bsz, hk, 3 * NSA_GQA, seq)
    b_out = _nsa(q_t, gates, kcmp, vcmp_t, ks, vs_t, kw, vw_t, ov_t, seq)
    return _outproj_even(h, ys, u, b_out, g, d_skip.reshape(1, -1), w_glu.astype(BF16),
                         w_out[:S5_WIDTH].astype(BF16), w_out[S5_WIDTH:].astype(BF16))


def _odd_mixer(h, g, bsz, seq, w_in, kv_norm, w_uv, w_out, tabs):
    cos32, sin32 = tabs[64]
    cos16, sin16 = tabs[32]
    d = h.shape[1]
    sizes = (MLA_HEADS * MLA_LATENT, MLA_HEADS * MLA_ROPE, MLA_LATENT, MLA_ROPE,
             IDX_HEADS * IDX_DIM, IDX_DIM, IDX_HEADS)
    parts = []
    o = 0
    for s in sizes:
        parts.append(w_in[:, o:o + s])
        o += s
    w_ql, w_qr, w_c, w_kr, w_qi, w_ki, w_wi = parts
    wq = jnp.concatenate([w_ql.reshape(d, MLA_HEADS, MLA_LATENT), w_qr.reshape(d, MLA_HEADS, MLA_ROPE)],
                         axis=2).reshape(d, MLA_HEADS * MLA_QK).T.astype(BF16)
    wkv = jnp.concatenate([w_c, w_kr, w_qi, w_ki, w_wi], axis=1).T
    wkv = jnp.pad(wkv, ((0, (-wkv.shape[0]) % 16), (0, 0))).astype(BF16)
    ckv_t, kr_t, qi_t, ki_t, wi_t = _proj_odd_kv(h, g, wkv, kv_norm.reshape(-1, 1), cos32, sin32,
                                                 cos16, sin16, bsz, seq)
    q = _proj_odd_q(h, g, wq, cos16, sin16, bsz, seq)
    tk = min(TQ_DSA, seq)
    ta = min(TA_DSA, seq)
    kcat = jnp.concatenate([ckv_t, kr_t], axis=1).transpose(0, 2, 1).reshape(bsz, seq // ta, ta, MLA_QK)
    ckt = ckv_t.reshape(bsz, MLA_LATENT, seq // ta, ta).transpose(0, 2, 1, 3)
    kidx = ki_t.transpose(0, 2, 1).reshape(bsz, seq // tk, tk, IDX_DIM)
    wuv_t = w_uv.transpose(0, 2, 1).astype(BF16)
    return _dsa(h, g, q, qi_t, wi_t, kcat, ckt, kidx, wuv_t, w_out.astype(BF16), seq)


def kernel(x, p, norm_g, ffn1_w_in, ffn1_w_out, ffn2_w_in, ffn2_w_out, ple_w_gate, ple_w_proj,
           ev_w_in, ev_w_out, s5_a_re, s5_a_im, s5_b_re, s5_b_im, s5_c_re, s5_c_im, s5_log_dt, s5_d,
           s5_w_glu, nsa_pe_k, nsa_pe_v, nsa_wk1, nsa_wk2, nsa_wv1, nsa_wv2,
           od_w_in, od_kv_norm, od_w_uv, od_w_out):
    bsz, seq, d = x.shape
    depth = norm_g.shape[0]
    tabs = {64: _rope_tables(seq, 64), 32: _rope_tables(seq, 32)}
    h = x.reshape(bsz * seq, d)
    for i in range(depth):
        g = norm_g[i]
        j = i // 2
        h = _ffn(h, g, ffn1_w_in[i].astype(BF16), ffn1_w_out[i].astype(BF16), 0)
        if i % 2 == 0:
            s5 = (s5_a_re[j], s5_a_im[j], s5_b_re[j], s5_b_im[j], s5_c_re[j], s5_c_im[j],
                  s5_log_dt[j], s5_d[j], s5_w_glu[j])
            h = _even_mixer(h, g, bsz, seq, ev_w_in[j], ev_w_out[j], s5, nsa_pe_k[j], nsa_pe_v[j],
                            nsa_wk1[j], nsa_wk2[j], nsa_wv1[j], nsa_wv2[j], tabs)
        else:
            h = _odd_mixer(h, g, bsz, seq, od_w_in[j], od_kv_norm[j], od_w_uv[j], od_w_out[j], tabs)
        h = _ffn_ple(h, p[i].reshape(bsz * seq, -1), g, ffn2_w_in[i].astype(BF16),
                     ffn2_w_out[i].astype(BF16), ple_w_gate[i].astype(BF16),
                     ple_w_proj[i].astype(BF16))
    return h.reshape(bsz, seq, d)
```

```python
import functools
import math

import jax
import jax.numpy as jnp
from jax import lax
from jax.experimental import pallas as pl
from jax.experimental.pallas import tpu as pltpu

F32 = jnp.float32
BF16 = jnp.bfloat16
I32 = jnp.int32

ROPE_THETA = 10000.0
EPS = 1e-6
NEG = -1e30
LOG2E = math.log2(math.e)
D_FF = 2816
S5_WIDTH = 512
S5_GROUP = 16
S5_GROUPS = S5_WIDTH // S5_GROUP
S5_STATE = 64
NSA_HEADS = 8
NSA_KV_HEADS = 2
NSA_GQA = NSA_HEADS // NSA_KV_HEADS
NSA_HEAD_DIM = 64
CMP_LEN = 32
CMP_STRIDE = 16
SEL_BLOCK = 64
SEL_TOP = 16
WINDOW = 512
FORCE_BONUS = 1000.0
NSA_Q = NSA_HEADS * NSA_HEAD_DIM
NSA_KV = NSA_KV_HEADS * NSA_HEAD_DIM
MLA_HEADS = 16
MLA_LATENT = 256
MLA_ROPE = 32
MLA_QK = MLA_LATENT + MLA_ROPE
MLA_V_DIM = 64
IDX_HEADS = 8
IDX_DIM = 64
DSA_TOP = 256

V7X_VMEM_BYTES = 64 * 2**20
VMEM_LIMIT = V7X_VMEM_BYTES - 8 * 2**20
LANE = 128
TM_FFN = 1024
TQ_PROJ = 256
TQ_NSA = 128
TK_NSA = 512
TQ_DSA = 256
TA_DSA = 512
S5_CHUNK = 64


def _cparams(sem):
    return pltpu.CompilerParams(dimension_semantics=sem, vmem_limit_bytes=VMEM_LIMIT)


def _resident(shape):
    nd = len(shape)
    return pl.BlockSpec(shape, lambda *_: (0,) * nd, pipeline_mode=pl.Buffered(1))


def _dot(a, b):
    return jnp.dot(a, b, preferred_element_type=F32)


def _dot_nt(a, b):
    return lax.dot_general(a, b, (((1,), (1,)), ((), ())), preferred_element_type=F32)


def _rms_rows(x, g):
    return x * lax.rsqrt(jnp.mean(x * x, axis=-1, keepdims=True) + EPS) * g


def _rope_fmaj(y, cos, sin):
    half = y.shape[1] // 2
    t1 = y[:, :half, :]
    t2 = y[:, half:, :]
    return jnp.concatenate([t1 * cos - t2 * sin, t2 * cos + t1 * sin], axis=1)


def _ffn_tile(x, g_ref, g0, win_ref, wout_ref, n_chunk):
    xn = _rms_rows(x, g_ref[g0:g0 + 1, :]).astype(BF16)
    ck = D_FF // n_chunk
    acc = None
    for c in range(n_chunk):
        a = _dot(xn, win_ref[:, c * ck:(c + 1) * ck])
        u = _dot(xn, win_ref[:, D_FF + c * ck:D_FF + (c + 1) * ck])
        act = (jax.nn.silu(a) * u).astype(BF16)
        y = _dot(act, wout_ref[c * ck:(c + 1) * ck, :])
        acc = y if acc is None else acc + y
    return x + 0.5 * _rms_rows(acc, g_ref[g0 + 1:g0 + 2, :])


def _ffn_kernel(h_ref, g_ref, win_ref, wout_ref, o_ref, *, g0, n_chunk):
    o_ref[...] = _ffn_tile(h_ref[...], g_ref, g0, win_ref, wout_ref, n_chunk)


def _ffn_ple_kernel(h_ref, p_ref, g_ref, win_ref, wout_ref, wg_ref, wp_ref, o_ref, *, n_chunk):
    x = _ffn_tile(h_ref[...], g_ref, 4, win_ref, wout_ref, n_chunk)
    gate = jax.nn.sigmoid(_dot(_rms_rows(x, g_ref[6:7, :]).astype(BF16), wg_ref[...]))
    e = _dot(p_ref[...].astype(BF16), wp_ref[...]) * gate
    o_ref[...] = x + _rms_rows(e, g_ref[7:8, :])


def _ffn_ple(h, p, g, w_in, w_out, w_gate, w_proj):
    t, d = h.shape
    tm = min(TM_FFN, t)
    return pl.pallas_call(
        functools.partial(_ffn_ple_kernel, n_chunk=2),
        grid=(t // tm,),
        in_specs=[pl.BlockSpec((tm, d), lambda i: (i, 0)),
                  pl.BlockSpec((tm, p.shape[1]), lambda i: (i, 0)), _resident(g.shape),
                  _resident(w_in.shape), _resident(w_out.shape), _resident(w_gate.shape),
                  _resident(w_proj.shape)],
        out_specs=pl.BlockSpec((tm, d), lambda i: (i, 0)),
        out_shape=jax.ShapeDtypeStruct((t, d), F32),
        compiler_params=_cparams(("parallel",)),
        name="ffn_ple",
    )(h, p, g, w_in, w_out, w_gate, w_proj)


def _ffn(h, g, w_in, w_out, g0):
    t, d = h.shape
    tm = min(TM_FFN, t)
    return pl.pallas_call(
        functools.partial(_ffn_kernel, g0=g0, n_chunk=2),
        grid=(t // tm,),
        in_specs=[pl.BlockSpec((tm, d), lambda i: (i, 0)), _resident(g.shape),
                  _resident(w_in.shape), _resident(w_out.shape)],
        out_specs=pl.BlockSpec((tm, d), lambda i: (i, 0)),
        out_shape=jax.ShapeDtypeStruct((t, d), F32),
        compiler_params=_cparams(("parallel",)),
        name="ffn",
    )(h, g, w_in, w_out)


def _outproj_even_kernel(h_ref, ys_ref, u_ref, b_ref, g_ref, d_ref, wglu_ref, wa_ref, wb_ref, o_ref):
    y = ys_ref[...] + d_ref[...] * u_ref[...]
    z = jax.nn.gelu(y)
    a = z * jax.nn.sigmoid(_dot(z.astype(BF16), wglu_ref[...]))
    mix = _dot(a.astype(BF16), wa_ref[...]) + _dot(b_ref[...].astype(BF16), wb_ref[...])
    o_ref[...] = h_ref[...] + _rms_rows(mix, g_ref[3:4, :])


def _outproj_even(h, ys, u, b_out, g, d_skip, w_glu, w_a, w_b):
    t, d = h.shape
    tm = min(TM_FFN, t)
    tok = lambda w: pl.BlockSpec((tm, w), lambda i: (i, 0))
    return pl.pallas_call(
        _outproj_even_kernel,
        grid=(t // tm,),
        in_specs=[tok(d), tok(S5_WIDTH), tok(S5_WIDTH), tok(NSA_Q), _resident(g.shape),
                  _resident(d_skip.shape), _resident(w_glu.shape), _resident(w_a.shape),
                  _resident(w_b.shape)],
        out_specs=tok(d),
        out_shape=jax.ShapeDtypeStruct((t, d), F32),
        compiler_params=_cparams(("parallel",)),
        name="outproj_even",
    )(h, ys, u, b_out, g, d_skip, w_glu, w_a, w_b)


def _proj_even_kernel(h_ref, g_ref, wu_ref, wt_ref, cos_ref, sin_ref,
                      u_ref, q_ref, k_ref, v_ref, gt_ref):
    xn = _rms_rows(h_ref[...], g_ref[2:3, :]).astype(BF16)
    u_ref[...] = _dot(xn, wu_ref[...])
    y = _dot_nt(wt_ref[...], xn)
    tq = y.shape[1]
    cos = cos_ref[...]
    sin = sin_ref[...]
    hd = NSA_HEAD_DIM
    q = _rope_fmaj(y[0:NSA_Q].reshape(NSA_HEADS, hd, tq), cos, sin)
    q_ref[...] = (q * (hd ** -0.5 * LOG2E)).reshape(NSA_Q, tq)
    k0 = NSA_Q
    nk = 3 * NSA_KV
    k = _rope_fmaj(y[k0:k0 + nk].reshape(3 * NSA_KV_HEADS, hd, tq), cos, sin)
    k_ref[...] = k.reshape(nk, tq).astype(k_ref.dtype)
    v0 = k0 + nk
    v_ref[...] = y[v0:v0 + nk].astype(v_ref.dtype)
    g0 = v0 + nk
    gt_ref[...] = jax.nn.sigmoid(y[g0:g0 + 3 * NSA_HEADS])


def _proj_even(h, g, wu, wt, cos, sin, bsz, seq):
    d = h.shape[1]
    tq = min(TQ_PROJ, seq)
    nq = seq // tq
    nk = 3 * NSA_KV
    fm = lambda rows: pl.BlockSpec((None, rows, tq), lambda b, i: (b, 0, i))
    return pl.pallas_call(
        _proj_even_kernel,
        grid=(bsz, nq),
        in_specs=[pl.BlockSpec((tq, d), lambda b, i: (b * nq + i, 0)), _resident(g.shape),
                  _resident(wu.shape), _resident(wt.shape),
                  pl.BlockSpec((cos.shape[0], tq), lambda b, i: (0, i)),
                  pl.BlockSpec((sin.shape[0], tq), lambda b, i: (0, i))],
        out_specs=[pl.BlockSpec((tq, S5_WIDTH), lambda b, i: (b * nq + i, 0)),
                   fm(NSA_Q), fm(nk), fm(nk), fm(3 * NSA_HEADS)],
        out_shape=[jax.ShapeDtypeStruct((bsz * seq, S5_WIDTH), F32),
                   jax.ShapeDtypeStruct((bsz, NSA_Q, seq), F32),
                   jax.ShapeDtypeStruct((bsz, nk, seq), BF16),
                   jax.ShapeDtypeStruct((bsz, nk, seq), BF16),
                   jax.ShapeDtypeStruct((bsz, 3 * NSA_HEADS, seq), F32)],
        compiler_params=_cparams(("parallel", "parallel")),
        name="proj_even",
    )(h, g, wu, wt, cos, sin)


def _proj_odd_kv_kernel(h_ref, g_ref, wt_ref, kvn_ref, cos32_ref, sin32_ref, cos16_ref, sin16_ref,
                        ckv_ref, kr_ref, qi_ref, ki_ref, wi_ref):
    xn = _rms_rows(h_ref[...], g_ref[2:3, :]).astype(BF16)
    y = _dot_nt(wt_ref[...], xn)
    tq = y.shape[1]
    c = y[0:MLA_LATENT]
    c = c * lax.rsqrt(jnp.mean(c * c, axis=0, keepdims=True) + EPS) * kvn_ref[...]
    ckv_ref[...] = c.astype(ckv_ref.dtype)
    r0 = MLA_LATENT
    kr = _rope_fmaj(y[r0:r0 + MLA_ROPE].reshape(1, MLA_ROPE, tq), cos16_ref[...], sin16_ref[...])
    kr_ref[...] = kr.reshape(MLA_ROPE, tq).astype(kr_ref.dtype)
    q0 = r0 + MLA_ROPE
    nqi = IDX_HEADS * IDX_DIM
    qi = _rope_fmaj(y[q0:q0 + nqi].reshape(IDX_HEADS, IDX_DIM, tq), cos32_ref[...], sin32_ref[...])
    qi_ref[...] = qi.reshape(nqi, tq).astype(qi_ref.dtype)
    k0 = q0 + nqi
    ki = _rope_fmaj(y[k0:k0 + IDX_DIM].reshape(1, IDX_DIM, tq), cos32_ref[...], sin32_ref[...])
    ki_ref[...] = ki.reshape(IDX_DIM, tq).astype(ki_ref.dtype)
    w0 = k0 + IDX_DIM
    wi_ref[...] = y[w0:w0 + IDX_HEADS]


def _proj_odd_kv(h, g, wt, kvn, cos32, sin32, cos16, sin16, bsz, seq):
    d = h.shape[1]
    tq = min(TQ_PROJ, seq)
    nq = seq // tq
    fm = lambda rows: pl.BlockSpec((None, rows, tq), lambda b, i: (b, 0, i))
    tab = lambda a: pl.BlockSpec((a.shape[0], tq), lambda b, i: (0, i))
    nqi = IDX_HEADS * IDX_DIM
    return pl.pallas_call(
        _proj_odd_kv_kernel,
        grid=(bsz, nq),
        in_specs=[pl.BlockSpec((tq, d), lambda b, i: (b * nq + i, 0)), _resident(g.shape),
                  _resident(wt.shape), _resident(kvn.shape),
                  tab(cos32), tab(sin32), tab(cos16), tab(sin16)],
        out_specs=[fm(MLA_LATENT), fm(MLA_ROPE), fm(nqi), fm(IDX_DIM), fm(IDX_HEADS)],
        out_shape=[jax.ShapeDtypeStruct((bsz, MLA_LATENT, seq), BF16),
                   jax.ShapeDtypeStruct((bsz, MLA_ROPE, seq), BF16),
                   jax.ShapeDtypeStruct((bsz, nqi, seq), BF16),
                   jax.ShapeDtypeStruct((bsz, IDX_DIM, seq), BF16),
                   jax.ShapeDtypeStruct((bsz, IDX_HEADS, seq), F32)],
        compiler_params=_cparams(("parallel", "parallel")),
        name="proj_odd_kv",
    )(h, g, wt, kvn, cos32, sin32, cos16, sin16)


def _proj_odd_q_kernel(h_ref, g_ref, wt_ref, cos16_ref, sin16_ref, q_ref):
    xn = _rms_rows(h_ref[...], g_ref[2:3, :]).astype(BF16)
    tq = xn.shape[0]
    cos = cos16_ref[...]
    sin = sin16_ref[...]
    for hh in range(MLA_HEADS):
        y = _dot_nt(wt_ref[hh * MLA_QK:(hh + 1) * MLA_QK, :], xn)
        r = _rope_fmaj(y[MLA_LATENT:].reshape(1, MLA_ROPE, tq), cos, sin)
        q = jnp.concatenate([y[:MLA_LATENT], r.reshape(MLA_ROPE, tq)], axis=0)
        q_ref[:, hh * tq:(hh + 1) * tq] = (q * (MLA_QK ** -0.5 * LOG2E)).astype(q_ref.dtype)


def _proj_odd_q(h, g, wt, cos16, sin16, bsz, seq):
    d = h.shape[1]
    tq = min(TQ_DSA, seq)
    nq = seq // tq
    return pl.pallas_call(
        _proj_odd_q_kernel,
        grid=(bsz, nq),
        in_specs=[pl.BlockSpec((tq, d), lambda b, i: (b * nq + i, 0)), _resident(g.shape),
                  _resident(wt.shape),
                  pl.BlockSpec((cos16.shape[0], tq), lambda b, i: (0, i)),
                  pl.BlockSpec((sin16.shape[0], tq), lambda b, i: (0, i))],
        out_specs=pl.BlockSpec((None, None, MLA_QK, MLA_HEADS * tq), lambda b, i: (b, i, 0, 0)),
        out_shape=jax.ShapeDtypeStruct((bsz, nq, MLA_QK, MLA_HEADS * tq), BF16),
        compiler_params=_cparams(("parallel", "parallel")),
        name="proj_odd_q",
    )(h, g, wt, cos16, sin16)


def _s5_state_kernel(u_ref, bc_ref, s_ref):
    s_ref[...] = _dot(u_ref[...], bc_ref[...])


def _s5_scan_kernel(sr_ref, si_ref, lr_ref, li_ref, xr_ref, xi_ref):
    n_chunk = sr_ref.shape[0]
    lr = lr_ref[...]
    li = li_ref[...]

    def body(c, carry):
        xr, xi = carry
        xr_ref[c] = xr
        xi_ref[c] = xi
        return (lr * xr - li * xi + sr_ref[c], lr * xi + li * xr + si_ref[c])

    zero = jnp.zeros(sr_ref.shape[1:], F32)
    lax.fori_loop(0, n_chunk, body, (zero, zero))


def _s5_out_kernel(u_ref, x_ref, m_ref, cc_ref, y_ref):
    x = x_ref[...]
    hi = x.astype(BF16)
    lo = (x - hi.astype(F32)).astype(BF16)
    cc = cc_ref[...]
    y_ref[...] = _dot(u_ref[...], m_ref[...]) + _dot(hi, cc) + _dot(lo, cc)


def _s5_matrices(a_re, a_im, b_re, b_im, c_re, c_im, log_dt, tc):
    hp = lax.Precision.HIGHEST
    dt = jnp.exp(log_dt)[:, None]
    lam = lax.complex(a_re, a_im)
    lam_dt = lam * dt
    lam_bar = jnp.exp(lam_dt)
    b_bar = ((lam_bar - 1.0) / lam)[..., None] * lax.complex(b_re, b_im)
    k = jnp.arange(tc + 1, dtype=F32)[:, None, None]
    pw = jnp.exp(lam_dt[None] * k)
    pr, pi = jnp.real(pw), jnp.imag(pw)
    bbr, bbi = jnp.real(b_bar), jnp.imag(b_bar)
    cpr = c_re[None] * pr[:, :, None, :] - c_im[None] * pi[:, :, None, :]
    cpi = c_re[None] * pi[:, :, None, :] + c_im[None] * pr[:, :, None, :]
    kk = (jnp.einsum('kgpn,gnq->kgpq', cpr[:tc], bbr, precision=hp)
          - jnp.einsum('kgpn,gnq->kgpq', cpi[:tc], bbi, precision=hp))
    n_g, n_p = a_re.shape[0], b_re.shape[2]
    kq = jnp.pad(kk.transpose(1, 3, 0, 2).astype(BF16), ((0, 0), (0, 0), (tc, 0), (0, 0)))
    lag = tc + jnp.arange(tc)[None, :] - jnp.arange(tc)[:, None]
    m = kq[:, :, lag, :].reshape(n_g, n_p * tc, tc * n_p)
    rev = pw[tc - 1 - jnp.arange(tc)]
    bc = rev[:, :, :, None] * b_bar[None]
    bc = bc.transpose(1, 3, 0, 2).reshape(n_g, n_p * tc, -1)
    bc = jnp.concatenate([jnp.real(bc), jnp.imag(bc)], axis=-1)
    mr = cpr[1:tc + 1].transpose(1, 3, 0, 2).reshape(n_g, -1, tc * n_p)
    mi = cpi[1:tc + 1].transpose(1, 3, 0, 2).reshape(n_g, -1, tc * n_p)
    cc = jnp.concatenate([mr, -mi], axis=1)
    ltc = pw[tc].reshape(1, -1)
    return m, bc.astype(BF16), cc.astype(BF16), jnp.real(ltc), jnp.imag(ltc)


def _s5_scan(u, mats, bsz, seq):
    m, bc, cc, lr, li = mats
    tc = S5_CHUNK
    n_c = seq // tc
    n_g, n_p, n_s = S5_GROUPS, S5_GROUP, S5_STATE
    rows = bsz * n_c
    kd = tc * n_p
    ug = (u.reshape(bsz, n_c, tc, n_g, n_p).transpose(3, 0, 1, 4, 2)
          .reshape(n_g, rows, kd).astype(BF16))
    grp = lambda a, b: pl.BlockSpec((None, a, b), lambda gi: (gi, 0, 0))
    s = pl.pallas_call(
        _s5_state_kernel,
        grid=(n_g,),
        in_specs=[grp(rows, kd), grp(kd, 2 * n_s)],
        out_specs=grp(rows, 2 * n_s),
        out_shape=jax.ShapeDtypeStruct((n_g, rows, 2 * n_s), F32),
        compiler_params=_cparams(("parallel",)),
        name="s5_state",
    )(ug, bc)
    s5 = s.reshape(n_g, bsz, n_c, 2, n_s).transpose(3, 2, 1, 0, 4).reshape(2, n_c, bsz, n_g * n_s)
    full = lambda shp: pl.BlockSpec(shp, lambda: (0,) * len(shp))
    xr, xi = pl.pallas_call(
        _s5_scan_kernel,
        in_specs=[full(s5.shape[1:]), full(s5.shape[1:]), full(lr.shape), full(li.shape)],
        out_specs=[full(s5.shape[1:]), full(s5.shape[1:])],
        out_shape=[jax.ShapeDtypeStruct(s5.shape[1:], F32)] * 2,
        compiler_params=pltpu.CompilerParams(vmem_limit_bytes=VMEM_LIMIT),
        name="s5_scan",
    )(s5[0], s5[1], lr, li)
    x = jnp.stack([xr, xi]).reshape(2, n_c, bsz, n_g, n_s).transpose(3, 2, 1, 0, 4)
    x = x.reshape(n_g, rows, 2 * n_s)
    y = pl.pallas_call(
        _s5_out_kernel,
        grid=(n_g,),
        in_specs=[grp(rows, kd), grp(rows, 2 * n_s), grp(kd, kd), grp(2 * n_s, kd)],
        out_specs=grp(rows, kd),
        out_shape=jax.ShapeDtypeStruct((n_g, rows, kd), F32),
        compiler_params=_cparams(("parallel",)),
        name="s5_out",
    )(ug, x, m, cc)
    return (y.reshape(n_g, bsz, n_c, tc, n_p).transpose(1, 2, 3, 0, 4)
            .reshape(bsz * seq, n_g * n_p))


def _compress_kernel(x_ref, pe_ref, w1_ref, w2_ref, o_ref):
    x = x_ref[...].astype(F32)
    nb = x.shape[0]
    a = _dot((x + pe_ref[0:1, :]).astype(BF16), w1_ref[0])
    b = _dot((x + pe_ref[1:2, :]).astype(BF16), w1_ref[1])
    pre = a + pltpu.roll(b, nb - 1, 0)
    o_ref[...] = _dot(jax.nn.gelu(pre).astype(BF16), w2_ref[...]).astype(o_ref.dtype)


def _compress(x, pe, w1, w2):
    bsz, hk, nb, kd = x.shape
    hd = w2.shape[1]
    return pl.pallas_call(
        _compress_kernel,
        grid=(bsz, hk),
        in_specs=[pl.BlockSpec((None, None, nb, kd), lambda b, h: (b, h, 0, 0)),
                  _resident(pe.shape), _resident(w1.shape), _resident(w2.shape)],
        out_specs=pl.BlockSpec((None, None, nb, hd), lambda b, h: (b, h, 0, 0)),
        out_shape=jax.ShapeDtypeStruct((bsz, hk, nb, hd), BF16),
        compiler_params=_cparams(("parallel", "parallel")),
        name="nsa_compress",
    )(x, pe, w1, w2)


def _nsa_kernel(q_ref, g_ref, kc_ref, vct_ref, ks_ref, vst_ref, kw_ref, vwt_ref, ov_ref,
                o_ref, sel_ref, zero_ref, sa_ref, sb_ref, *, tq, tk, top_n):
    gq, hd = NSA_GQA, NSA_HEAD_DIM
    n_sb = sel_ref.shape[0]
    nb = kc_ref.shape[0]
    wd = gq * tq
    i = pl.program_id(2)
    s0 = i * tq
    q = q_ref[...]
    qt = jnp.concatenate([q[g * hd:(g + 1) * hd, :] for g in range(gq)], axis=1).astype(BF16)
    t1 = s0 + lax.broadcasted_iota(I32, (1, tq), 1)
    t4 = s0 + (lax.broadcasted_iota(I32, (1, wd), 1) & (tq - 1))

    sc = _dot(kc_ref[...], qt)
    c_last = lax.broadcasted_iota(I32, (nb, 1), 0) * CMP_STRIDE + (CMP_LEN - 1)
    bias_c = jnp.where(c_last <= t1, 0.0, NEG)
    sm = sc + jnp.concatenate([bias_c] * gq, axis=1)
    e = jnp.exp2(sm - jnp.max(sm, axis=0, keepdims=True))
    p_c = e * jnp.where(t4 >= CMP_LEN - 1, 1.0 / jnp.sum(e, axis=0, keepdims=True), 0.0)
    o_c = _dot(vct_ref[...], p_c.astype(BF16))
    psum = p_c[:, 0:tq]
    for g in range(1, gq):
        psum = psum + p_c[:, g * tq:(g + 1) * tq]
    hi = psum.astype(BF16)
    lo = (psum - hi.astype(F32)).astype(BF16)
    imp = _dot(ov_ref[...], hi) + _dot(ov_ref[...], lo)
    blk = lax.broadcasted_iota(I32, (n_sb, 1), 0)
    cur = t1 >> int(math.log2(SEL_BLOCK))
    forced = (blk == 0) | (blk == cur) | (blk == cur - 1)
    imp = imp + jnp.where(forced, FORCE_BONUS, 0.0)
    imp = jnp.where(blk * SEL_BLOCK <= t1, imp, NEG)

    rowf = lax.broadcasted_iota(I32, (n_sb, tq), 0).astype(F32)
    sel = jnp.zeros((n_sb, tq), F32)
    x = imp
    for _ in range(top_n):
        mx = jnp.max(x, axis=0, keepdims=True)
        first = jnp.min(jnp.where(x == mx, rowf, float(n_sb)), axis=0, keepdims=True)
        hit = rowf == first
        sel = jnp.where(hit, 1.0, sel)
        x = jnp.where(hit, -jnp.inf, x)
    sel_ref[...] = sel

    zero_ref[...] = jnp.zeros(zero_ref.shape, F32)
    init = (jnp.full((1, wd), NEG, F32), jnp.zeros((1, wd), F32), zero_ref[...])

    bpt = tk // SEL_BLOCK
    key_row = lax.broadcasted_iota(I32, (tk, 1), 0)

    last_tile = ks_ref.shape[0] - 1

    def scores(j):
        return _dot(ks_ref[jnp.minimum(j, last_tile)], qt)

    def consume(j, s_ref, carry):
        m, l, acc, pv = carry
        jc = jnp.minimum(j, last_tile)
        rows = [jnp.broadcast_to(sel_ref[pl.ds(jc * bpt + b, 1), :], (SEL_BLOCK, tq))
                for b in range(bpt)]
        keep = (jnp.concatenate(rows, axis=0) > 0.5) & (j * tk + key_row <= t1)
        bias = jnp.where(keep, 0.0, NEG)
        s = s_ref[...] + jnp.concatenate([bias] * gq, axis=1)
        m_new = jnp.maximum(m, jnp.max(s, axis=0, keepdims=True))
        alpha = jnp.exp2(m - m_new)
        p = jnp.exp2(s - m_new)
        l = alpha * l + jnp.sum(p, axis=0, keepdims=True)
        return m_new, l, alpha * (acc + pv), _dot(vst_ref[jc], p.astype(BF16))

    def tile_pair(k, carry):
        j = 2 * k
        sb_ref[...] = scores(j + 1)
        carry = consume(j, sa_ref, carry)
        sa_ref[...] = scores(j + 2)
        return consume(j + 1, sb_ref, carry)

    sa_ref[...] = scores(0)
    _, l_s, a_s, pv_s = lax.fori_loop(0, (s0 // tk + 2) // 2, tile_pair, init + (zero_ref[...],))
    o_s = (a_s + pv_s) / l_s

    n_wt = (WINDOW + tq) // LANE
    j_lo = jnp.clip(i * (tq // LANE) - WINDOW // LANE, 0, kw_ref.shape[0] - n_wt)
    kwin = jnp.concatenate([kw_ref[j_lo + w] for w in range(n_wt)], axis=0)
    vwin = jnp.concatenate([vwt_ref[j_lo + w] for w in range(n_wt)], axis=1)
    sw = _dot(kwin, qt)
    diff = t1 - (j_lo * LANE + lax.broadcasted_iota(I32, (n_wt * LANE, 1), 0))
    sw = sw + jnp.concatenate([jnp.where((diff >= 0) & (diff < WINDOW), 0.0, NEG)] * gq, axis=1)
    pw = jnp.exp2(sw - jnp.max(sw, axis=0, keepdims=True))
    o_w = _dot(vwin, pw.astype(BF16)) / jnp.sum(pw, axis=0, keepdims=True)

    gt = g_ref[...]
    outs = []
    for g in range(gq):
        cs = slice(g * tq, (g + 1) * tq)
        outs.append(gt[3 * g:3 * g + 1, :] * o_c[:, cs] + gt[3 * g + 1:3 * g + 2, :] * o_s[:, cs]
                    + gt[3 * g + 2:3 * g + 3, :] * o_w[:, cs])
    o_ref[...] = jnp.concatenate(outs, axis=0).T


def _nsa(q, gates, kcmp, vcmp_t, ks, vs_t, kw, vw_t, ov_t, seq):
    bsz = q.shape[0]
    tq = min(TQ_NSA, seq)
    tk = min(TK_NSA, seq)
    n_sb = seq // SEL_BLOCK
    gq, hd = NSA_GQA, NSA_HEAD_DIM
    per_head = lambda a: pl.BlockSpec((None, None) + a.shape[2:],
                                      lambda b, h, i: (b, h) + (0,) * (a.ndim - 2),
                                      pipeline_mode=pl.Buffered(1))
    return pl.pallas_call(
        functools.partial(_nsa_kernel, tq=tq, tk=tk, top_n=min(SEL_TOP, n_sb)),
        grid=(bsz, NSA_KV_HEADS, seq // tq),
        in_specs=[pl.BlockSpec((None, gq * hd, tq), lambda b, h, i: (b, h, i)),
                  pl.BlockSpec((None, None, 3 * gq, tq), lambda b, h, i: (b, h, 0, i)),
                  per_head(kcmp), per_head(vcmp_t), per_head(ks), per_head(vs_t),
                  per_head(kw), per_head(vw_t), _resident(ov_t.shape)],
        out_specs=pl.BlockSpec((tq, gq * hd), lambda b, h, i: (b * (seq // tq) + i, h)),
        out_shape=jax.ShapeDtypeStruct((bsz * seq, NSA_Q), F32),
        scratch_shapes=[pltpu.VMEM((n_sb, tq), F32), pltpu.VMEM((hd, gq * tq), F32),
                        pltpu.VMEM((tk, gq * tq), F32), pltpu.VMEM((tk, gq * tq), F32)],
        compiler_params=_cparams(("parallel", "parallel", "arbitrary")),
        name="nsa",
    )(q, gates, kcmp, vcmp_t, ks, vs_t, kw, vw_t, ov_t)


def _dsa_kernel(h_ref, g_ref, q_ref, qi_ref, wi_ref, kcat_ref, ckt_ref, kidx_ref, wuv_ref, wout_ref,
                o_ref, ihi, ilo, acc_ref, m_ref, l_ref, tie_cap, *, tq, top):
    tk = tq
    n_h = MLA_HEADS
    i16 = jnp.int16
    i = pl.program_id(1)
    row = lax.broadcasted_iota(I32, (tk, 1), 0)
    col = lax.broadcasted_iota(I32, (1, tq), 1)
    causal = row <= col

    def tile(ref, j):
        return ref.at[pl.ds(pl.multiple_of(j * tk, tk), tk), :]

    qi = qi_ref[...]
    wi = wi_ref[...] * (IDX_HEADS ** -0.5) * (IDX_DIM ** -0.5)

    def idx_scores(j):
        kj = kidx_ref[j]
        sc = jnp.zeros((tk, tq), F32)
        for h in range(IDX_HEADS):
            d = _dot(kj, qi[h * IDX_DIM:(h + 1) * IDX_DIM, :])
            sc = sc + wi[h:h + 1, :] * jnp.maximum(d, 0.0)
        return sc

    def store_keys(j, sc):
        b = pltpu.bitcast(sc, I32)
        key = b ^ ((b >> 31) & 0x7FFFFFFF)
        tile(ihi, j)[...] = (key >> 16).astype(i16)
        tile(ilo, j)[...] = ((key & 0xFFFF) - 32768).astype(i16)

    def idx_tile(j, _):
        store_keys(j, idx_scores(j))
        return 0

    lax.fori_loop(0, i, idx_tile, 0)
    store_keys(i, jnp.where(causal, idx_scores(i), NEG))

    @pl.when(((i + 1) * tq) % kcat_ref.shape[1] != 0)
    def _():
        store_keys(i + 1, jnp.full((tk, tq), NEG, F32))

    def count(ref, pred):
        def body(j, c):
            hit = jnp.where(pred(tile(ref, j)[...]), jnp.ones((), i16), jnp.zeros((), i16))
            parts = [hit[k * 16:(k + 1) * 16] for k in range(tk // 16)]
            while len(parts) > 1:
                parts = [a + b for a, b in zip(parts[0::2], parts[1::2])]
            return c + parts[0]
        c16 = lax.fori_loop(0, i + 1, body, jnp.zeros((16, tq), i16))
        return jnp.sum(c16.astype(I32), axis=0, keepdims=True)

    def search(ref, need, n_all):
        def bit_step(b, st):
            thr, c_ge, c_gt = st
            cand = thr + (jnp.int32(1) << (15 - b))
            cand16 = cand.astype(i16)
            cnt = count(ref, lambda t: t >= cand16)
            ok = cnt >= need
            return jnp.where(ok, cand, thr), jnp.where(ok, cnt, c_ge), jnp.where(ok, c_gt, cnt)
        init = (jnp.full((1, tq), -32768, I32), n_all, jnp.zeros((1, tq), I32))
        return lax.fori_loop(0, 16, bit_step, init)

    thr_hi, n_ge_hi, n_gt_hi = search(ihi, top, jnp.full((1, tq), 1, I32) * ((i + 1) * tk))
    thr_hi16 = thr_hi.astype(i16)
    need = top - n_gt_hi
    n_eq_hi = n_ge_hi - n_gt_hi

    def mask_lo(j, _):
        lo = tile(ilo, j)
        lo[...] = jnp.where(tile(ihi, j)[...] == thr_hi16, lo[...], jnp.full((), -32768, i16))
        return 0

    lax.fori_loop(0, i + 1, mask_lo, 0)
    thr_lo, n_ge_lo, _ = search(ilo, need, n_eq_hi)

    n_sel = n_gt_hi + n_ge_lo
    tie_cap[...] = jnp.full((1, tq), ihi.shape[0], I32)
    excess = n_sel - top
    max_excess = jnp.max(excess)
    n_bits = (ihi.shape[0] - 1).bit_length()

    thr_lo16 = thr_lo.astype(i16)

    def tie_diff(j):
        return ((tile(ihi, j)[...] ^ thr_hi16) | (tile(ilo, j)[...] ^ thr_lo16)).astype(I32)

    n_top = 3
    max_scans = 2 * n_bits // n_top

    @pl.when((max_excess > 0) & (max_excess <= n_top * max_scans))
    def _():
        def drop(_, st):
            cap, left = st

            def body(j, tops):
                idx = j * tk + row
                v = jnp.where(tie_diff(j) == 0, jnp.where(idx <= cap, idx, -1), -1)
                v = v.reshape(tk // 8, 8, tq)
                for k in range(tk // 8):
                    t = v[k]
                    new = []
                    for m in tops:
                        new.append(jnp.maximum(m, t))
                        t = jnp.minimum(m, t)
                    tops = tuple(new)
                return tops

            tops = list(lax.fori_loop(0, i + 1, body, (jnp.full((8, tq), -1, I32),) * n_top))
            take = jnp.minimum(left, n_top)
            for r in range(n_top):
                g = jnp.max(functools.reduce(jnp.maximum, tops), axis=0, keepdims=True)
                cap = jnp.where(take == r + 1, g - 1, cap)
                tops = [jnp.where(m == g, -1, m) for m in tops]
            return cap, left - take

        n_scan = (max_excess + n_top - 1) // n_top
        tie_cap[...] = lax.fori_loop(0, n_scan, drop, (tie_cap[...], excess))[0]

    @pl.when(max_excess > n_top * max_scans)
    def _():
        def count_ties(below):
            def body(j, c):
                hit = jnp.where(tie_diff(j) == 0, jnp.where(j * tk + row < below, 1, 0), 0)
                return c + jnp.sum(hit.reshape(tk // 8, 8, tq), axis=0)
            c8 = lax.fori_loop(0, i + 1, body, jnp.zeros((8, tq), I32))
            return jnp.sum(c8, axis=0, keepdims=True)

        need_eq = count_ties(ihi.shape[0]) - excess

        def bit_step(b, cap):
            cand = cap + (jnp.int32(1) << (n_bits - 1 - b))
            return jnp.where(count_ties(cand) >= need_eq, cap, cand)

        tie_cap[...] = lax.fori_loop(0, n_bits, bit_step, jnp.zeros((1, tq), I32))

    m_ref[...] = jnp.full(m_ref.shape, NEG, F32)
    l_ref[...] = jnp.zeros(l_ref.shape, F32)
    acc_ref[...] = jnp.zeros(acc_ref.shape, F32)
    cap = tie_cap[...]
    thr_key = (thr_hi << 16) | (thr_lo + 32768)

    ta = kcat_ref.shape[1]
    a_row = lax.broadcasted_iota(I32, (ta, 1), 0)
    q_pos = i * tq + col

    def att_tile(j, _):
        rows = pl.ds(pl.multiple_of(j * ta, ta), ta)
        hi = ihi[rows, :].astype(I32)
        lo = ilo[rows, :].astype(I32)
        k_pos = j * ta + a_row
        key = (hi << 16) | (lo + 32768)
        sel = jnp.where(key > thr_key, 0.0,
                        jnp.where(key == thr_key, jnp.where(k_pos <= cap, 0.0, NEG), NEG))
        bias = jnp.where(k_pos <= q_pos, sel, NEG)
        s_all = _dot(kcat_ref[j], q_ref[...])
        ps, alphas = [], []
        for h in range(n_h):
            cs = slice(h * tq, (h + 1) * tq)
            s = s_all[:, cs] + bias
            m_old = m_ref[:, cs]
            m_new = jnp.maximum(m_old, jnp.max(s, axis=0, keepdims=True))
            alpha = jnp.exp2(m_old - m_new)
            p = jnp.exp2(s - m_new)
            l_ref[:, cs] = alpha * l_ref[:, cs] + jnp.sum(p, axis=0, keepdims=True)
            m_ref[:, cs] = m_new
            ps.append(p.astype(BF16))
            alphas.append(alpha)
        p_all = jnp.concatenate(ps, axis=1)
        acc_ref[...] = acc_ref[...] * jnp.concatenate(alphas, axis=1) + _dot(ckt_ref[j], p_all)
        return 0

    lax.fori_loop(0, (i * tq) // ta + 1, att_tile, 0)

    outs = []
    for h in range(n_h):
        cs = slice(h * tq, (h + 1) * tq)
        o = acc_ref[:, cs] / l_ref[:, cs]
        outs.append(_dot(wuv_ref[h], o.astype(BF16)))
    mix = jnp.concatenate(outs, axis=0).T
    y = _dot(mix.astype(BF16), wout_ref[...])
    o_ref[...] = h_ref[...] + _rms_rows(y, g_ref[3:4, :])


def _dsa(h, g, q, qi, wi, kcat, ckt, kidx, wuv_t, w_out, seq):
    bsz = q.shape[0]
    d = h.shape[1]
    tq = min(TQ_DSA, seq)
    nq = seq // tq
    top = min(DSA_TOP, seq // 4)
    wd = MLA_HEADS * tq
    per_b = lambda a: pl.BlockSpec((None,) + a.shape[1:], lambda b, i: (b,) + (0,) * (a.ndim - 1),
                                   pipeline_mode=pl.Buffered(1))
    return pl.pallas_call(
        functools.partial(_dsa_kernel, tq=tq, top=top),
        grid=(bsz, nq),
        in_specs=[pl.BlockSpec((tq, d), lambda b, i: (b * nq + i, 0)), _resident(g.shape),
                  pl.BlockSpec((None, None, MLA_QK, wd), lambda b, i: (b, i, 0, 0)),
                  pl.BlockSpec((None, IDX_HEADS * IDX_DIM, tq), lambda b, i: (b, 0, i)),
                  pl.BlockSpec((None, IDX_HEADS, tq), lambda b, i: (b, 0, i)),
                  per_b(kcat), per_b(ckt), per_b(kidx), _resident(wuv_t.shape),
                  _resident(w_out.shape)],
        out_specs=pl.BlockSpec((tq, d), lambda b, i: (b * nq + i, 0)),
        out_shape=jax.ShapeDtypeStruct(h.shape, F32),
        scratch_shapes=[pltpu.VMEM((seq, tq), jnp.int16),
                        pltpu.VMEM((seq, tq), jnp.int16),
                        pltpu.VMEM((MLA_LATENT, wd), F32),
                        pltpu.VMEM((1, wd), F32),
                        pltpu.VMEM((1, wd), F32),
                        pltpu.VMEM((1, tq), I32)],
        compiler_params=_cparams(("parallel", "arbitrary")),
        name="dsa",
    )(h, g, q, qi, wi, kcat, ckt, kidx, wuv_t, w_out)


def _rope_tables(seq, dim):
    half = dim // 2
    inv = ROPE_THETA ** (-jnp.arange(half, dtype=F32) / half)
    ang = inv[:, None] * jnp.arange(seq, dtype=F32)[None, :]
    return jnp.cos(ang), jnp.sin(ang)


def _even_mixer(h, g, bsz, seq, w_in, w_out, s5, pe_k, pe_v, wk1, wk2, wv1, wv2, tabs):
    cos32, sin32 = tabs[64]
    hd, hk = NSA_HEAD_DIM, NSA_KV_HEADS
    o = 0
    cols = {}
    for name, size in (("u", S5_WIDTH), ("q", NSA_Q), ("kc", NSA_KV), ("vc", NSA_KV), ("ks", NSA_KV),
                       ("vs", NSA_KV), ("kw", NSA_KV), ("vw", NSA_KV), ("gt", 3 * NSA_HEADS)):
        cols[name] = w_in[:, o:o + size]
        o += size
    wu = cols["u"].astype(BF16)
    wt = jnp.concatenate([cols[n] for n in ("q", "kc", "ks", "kw", "vc", "vs", "vw", "gt")], axis=1).T
    wt = jnp.pad(wt, ((0, (-wt.shape[0]) % 16), (0, 0))).astype(BF16)
    u, q_t, k_t, v_t, gt_t = _proj_even(h, g, wu, wt, cos32, sin32, bsz, seq)

    a_re, a_im, b_re, b_im, c_re, c_im, log_dt, d_skip, w_glu = s5
    mats = _s5_matrices(a_re, a_im, b_re, b_im, c_re, c_im, log_dt, S5_CHUNK)
    ys = _s5_scan(u, mats, bsz, seq)

    k5 = k_t.reshape(bsz, 3, hk, hd, seq)
    v5 = v_t.reshape(bsz, 3, hk, hd, seq)
    nb = seq // CMP_STRIDE
    half_blk = lambda a: (a.reshape(bsz, hk, hd, nb, CMP_STRIDE).transpose(0, 1, 3, 4, 2)
                          .reshape(bsz, hk, nb, CMP_STRIDE * hd))
    pe2 = lambda pe: pe.reshape(2, CMP_STRIDE * hd)
    w1s = lambda w: w.reshape(2, CMP_STRIDE * hd, hd).astype(BF16)
    kcmp = _compress(half_blk(k5[:, 0]), pe2(pe_k), w1s(wk1), wk2.astype(BF16))
    vcmp = _compress(half_blk(v5[:, 0]), pe2(pe_v), w1s(wv1), wv2.astype(BF16))
    vcmp_t = vcmp.transpose(0, 1, 3, 2)
    tk = min(TK_NSA, seq)
    ks = k5[:, 1].reshape(bsz, hk, hd, seq // tk, tk).transpose(0, 1, 3, 4, 2)
    vs_t = v5[:, 1].reshape(bsz, hk, hd, seq // tk, tk).transpose(0, 1, 3, 2, 4)
    kw = k5[:, 2].reshape(bsz, hk, hd, seq // LANE, LANE).transpose(0, 1, 3, 4, 2)
    vw_t = v5[:, 2].reshape(bsz, hk, hd, seq // LANE, LANE).transpose(0, 1, 3, 2, 4)
    n_sb = seq // SEL_BLOCK
    c_start = jnp.arange(nb) * CMP_STRIDE
    b_start = jnp.arange(n_sb) * SEL_BLOCK
    ov_t = ((c_start[None, :] < b_start[:, None] + SEL_BLOCK)
            & (c_start[None, :] + CMP_LEN > b_start[:, None])
            & (jnp.arange(nb)[None, :] < nb - 1)).astype(BF16)
    gates = gt_t.reshape(bsz, hk, 3 * NSA_GQA, seq)
    b_out = _nsa(q_t, gates, kcmp, vcmp_t, ks, vs_t, kw, vw_t, ov_t, seq)
    return _outproj_even(h, ys, u, b_out, g, d_skip.reshape(1, -1), w_glu.astype(BF16),
                         w_out[:S5_WIDTH].astype(BF16), w_out[S5_WIDTH:].astype(BF16))


def _odd_mixer(h, g, bsz, seq, w_in, kv_norm, w_uv, w_out, tabs):
    cos32, sin32 = tabs[64]
    cos16, sin16 = tabs[32]
    d = h.shape[1]
    sizes = (MLA_HEADS * MLA_LATENT, MLA_HEADS * MLA_ROPE, MLA_LATENT, MLA_ROPE,
             IDX_HEADS * IDX_DIM, IDX_DIM, IDX_HEADS)
    parts = []
    o = 0
    for s in sizes:
        parts.append(w_in[:, o:o + s])
        o += s
    w_ql, w_qr, w_c, w_kr, w_qi, w_ki, w_wi = parts
    wq = jnp.concatenate([w_ql.reshape(d, MLA_HEADS, MLA_LATENT), w_qr.reshape(d, MLA_HEADS, MLA_ROPE)],
                         axis=2).reshape(d, MLA_HEADS * MLA_QK).T.astype(BF16)
    wkv = jnp.concatenate([w_c, w_kr, w_qi, w_ki, w_wi], axis=1).T
    wkv = jnp.pad(wkv, ((0, (-wkv.shape[0]) % 16), (0, 0))).astype(BF16)
    ckv_t, kr_t, qi_t, ki_t, wi_t = _proj_odd_kv(h, g, wkv, kv_norm.reshape(-1, 1), cos32, sin32,
                                                 cos16, sin16, bsz, seq)
    q = _proj_odd_q(h, g, wq, cos16, sin16, bsz, seq)
    tk = min(TQ_DSA, seq)
    ta = min(TA_DSA, seq)
    kcat = jnp.concatenate([ckv_t, kr_t], axis=1).transpose(0, 2, 1).reshape(bsz, seq // ta, ta, MLA_QK)
    ckt = ckv_t.reshape(bsz, MLA_LATENT, seq // ta, ta).transpose(0, 2, 1, 3)
    kidx = ki_t.transpose(0, 2, 1).reshape(bsz, seq // tk, tk, IDX_DIM)
    wuv_t = w_uv.transpose(0, 2, 1).astype(BF16)
    return _dsa(h, g, q, qi_t, wi_t, kcat, ckt, kidx, wuv_t, w_out.astype(BF16), seq)


def kernel(x, p, norm_g, ffn1_w_in, ffn1_w_out, ffn2_w_in, ffn2_w_out, ple_w_gate, ple_w_proj,
           ev_w_in, ev_w_out, s5_a_re, s5_a_im, s5_b_re, s5_b_im, s5_c_re, s5_c_im, s5_log_dt, s5_d,
           s5_w_glu, nsa_pe_k, nsa_pe_v, nsa_wk1, nsa_wk2, nsa_wv1, nsa_wv2,
           od_w_in, od_kv_norm, od_w_uv, od_w_out):
    bsz, seq, d = x.shape
    depth = norm_g.shape[0]
    tabs = {64: _rope_tables(seq, 64), 32: _rope_tables(seq, 32)}
    h = x.reshape(bsz * seq, d)
    for i in range(depth):
        g = norm_g[i]
        j = i // 2
        h = _ffn(h, g, ffn1_w_in[i].astype(BF16), ffn1_w_out[i].astype(BF16), 0)
        if i % 2 == 0:
            s5 = (s5_a_re[j], s5_a_im[j], s5_b_re[j], s5_b_im[j], s5_c_re[j], s5_c_im[j],
                  s5_log_dt[j], s5_d[j], s5_w_glu[j])
            h = _even_mixer(h, g, bsz, seq, ev_w_in[j], ev_w_out[j], s5, nsa_pe_k[j], nsa_pe_v[j],
                            nsa_wk1[j], nsa_wk2[j], nsa_wv1[j], nsa_wv2[j], tabs)
        else:
            h = _odd_mixer(h, g, bsz, seq, od_w_in[j], od_kv_norm[j], od_w_uv[j], od_w_out[j], tabs)
        h = _ffn_ple(h, p[i].reshape(bsz * seq, -1), g, ffn2_w_in[i].astype(BF16),
                     ffn2_w_out[i].astype(BF16), ple_w_gate[i].astype(BF16),
                     ple_w_proj[i].astype(BF16))
    return h.reshape(bsz, seq, d)
```

```python
import functools
import math

import jax
import jax.numpy as jnp
from jax import lax
from jax.experimental import pallas as pl
from jax.experimental.pallas import tpu as pltpu

F32 = jnp.float32
BF16 = jnp.bfloat16
I32 = jnp.int32

ROPE_THETA = 10000.0
EPS = 1e-6
NEG = -1e30
LOG2E = math.log2(math.e)
D_FF = 2816
S5_WIDTH = 512
S5_GROUP = 16
S5_GROUPS = S5_WIDTH // S5_GROUP
S5_STATE = 64
NSA_HEADS = 8
NSA_KV_HEADS = 2
NSA_GQA = NSA_HEADS // NSA_KV_HEADS
NSA_HEAD_DIM = 64
CMP_LEN = 32
CMP_STRIDE = 16
SEL_BLOCK = 64
SEL_TOP = 16
WINDOW = 512
FORCE_BONUS = 1000.0
NSA_Q = NSA_HEADS * NSA_HEAD_DIM
NSA_KV = NSA_KV_HEADS * NSA_HEAD_DIM
MLA_HEADS = 16
MLA_LATENT = 256
MLA_ROPE = 32
MLA_QK = MLA_LATENT + MLA_ROPE
MLA_V_DIM = 64
IDX_HEADS = 8
IDX_DIM = 64
DSA_TOP = 256

V7X_VMEM_BYTES = 64 * 2**20
VMEM_LIMIT = V7X_VMEM_BYTES - 8 * 2**20
LANE = 128
TM_FFN = 1024
TQ_PROJ = 1024
TQ_NSA = 256
TK_NSA = 512
TQ_DSA = 256
TA_DSA = 512
S5_CHUNK = 64


def _cparams(sem):
    return pltpu.CompilerParams(dimension_semantics=sem, vmem_limit_bytes=VMEM_LIMIT)


def _resident(shape):
    nd = len(shape)
    return pl.BlockSpec(shape, lambda *_: (0,) * nd, pipeline_mode=pl.Buffered(1))


def _dot(a, b):
    return jnp.dot(a, b, preferred_element_type=F32)


def _dot_nt(a, b):
    return lax.dot_general(a, b, (((1,), (1,)), ((), ())), preferred_element_type=F32)


def _rms_rows(x, g):
    return x * lax.rsqrt(jnp.mean(x * x, axis=-1, keepdims=True) + EPS) * g


def _rope_fmaj(y, cos, sin):
    half = y.shape[1] // 2
    t1 = y[:, :half, :]
    t2 = y[:, half:, :]
    return jnp.concatenate([t1 * cos - t2 * sin, t2 * cos + t1 * sin], axis=1)


def _ffn_tile(x, g_ref, g0, win_ref, wout_ref, n_chunk):
    xn = _rms_rows(x, g_ref[g0:g0 + 1, :]).astype(BF16)
    ck = D_FF // n_chunk
    acc = None
    for c in range(n_chunk):
        a = _dot(xn, win_ref[:, c * ck:(c + 1) * ck])
        u = _dot(xn, win_ref[:, D_FF + c * ck:D_FF + (c + 1) * ck])
        act = (jax.nn.silu(a) * u).astype(BF16)
        y = _dot(act, wout_ref[c * ck:(c + 1) * ck, :])
        acc = y if acc is None else acc + y
    return x + 0.5 * _rms_rows(acc, g_ref[g0 + 1:g0 + 2, :])


def _ffn_kernel(h_ref, g_ref, win_ref, wout_ref, o_ref, *, g0, n_chunk):
    o_ref[...] = _ffn_tile(h_ref[...], g_ref, g0, win_ref, wout_ref, n_chunk)


def _ffn_ple_kernel(h_ref, p_ref, g_ref, win_ref, wout_ref, wg_ref, wp_ref, o_ref, *, n_chunk):
    x = _ffn_tile(h_ref[...], g_ref, 4, win_ref, wout_ref, n_chunk)
    gate = jax.nn.sigmoid(_dot(_rms_rows(x, g_ref[6:7, :]).astype(BF16), wg_ref[...]))
    e = _dot(p_ref[...].astype(BF16), wp_ref[...]) * gate
    o_ref[...] = x + _rms_rows(e, g_ref[7:8, :])


def _ffn_ple(h, p, g, w_in, w_out, w_gate, w_proj):
    t, d = h.shape
    tm = min(TM_FFN, t)
    return pl.pallas_call(
        functools.partial(_ffn_ple_kernel, n_chunk=2),
        grid=(t // tm,),
        in_specs=[pl.BlockSpec((tm, d), lambda i: (i, 0)),
                  pl.BlockSpec((tm, p.shape[1]), lambda i: (i, 0)), _resident(g.shape),
                  _resident(w_in.shape), _resident(w_out.shape), _resident(w_gate.shape),
                  _resident(w_proj.shape)],
        out_specs=pl.BlockSpec((tm, d), lambda i: (i, 0)),
        out_shape=jax.ShapeDtypeStruct((t, d), F32),
        compiler_params=_cparams(("parallel",)),
        name="ffn_ple",
    )(h, p, g, w_in, w_out, w_gate, w_proj)


def _ffn(h, g, w_in, w_out, g0):
    t, d = h.shape
    tm = min(TM_FFN, t)
    return pl.pallas_call(
        functools.partial(_ffn_kernel, g0=g0, n_chunk=2),
        grid=(t // tm,),
        in_specs=[pl.BlockSpec((tm, d), lambda i: (i, 0)), _resident(g.shape),
                  _resident(w_in.shape), _resident(w_out.shape)],
        out_specs=pl.BlockSpec((tm, d), lambda i: (i, 0)),
        out_shape=jax.ShapeDtypeStruct((t, d), F32),
        compiler_params=_cparams(("parallel",)),
        name="ffn",
    )(h, g, w_in, w_out)


def _outproj_even_kernel(h_ref, ys_ref, u_ref, b_ref, g_ref, d_ref, wglu_ref, wa_ref, wb_ref, o_ref):
    y = ys_ref[...] + d_ref[...] * u_ref[...]
    z = jax.nn.gelu(y)
    a = z * jax.nn.sigmoid(_dot(z.astype(BF16), wglu_ref[...]))
    mix = _dot(a.astype(BF16), wa_ref[...]) + _dot(b_ref[...].astype(BF16), wb_ref[...])
    o_ref[...] = h_ref[...] + _rms_rows(mix, g_ref[3:4, :])


def _outproj_even(h, ys, u, b_out, g, d_skip, w_glu, w_a, w_b):
    t, d = h.shape
    tm = min(TM_FFN, t)
    tok = lambda w: pl.BlockSpec((tm, w), lambda i: (i, 0))
    return pl.pallas_call(
        _outproj_even_kernel,
        grid=(t // tm,),
        in_specs=[tok(d), tok(S5_WIDTH), tok(S5_WIDTH), tok(NSA_Q), _resident(g.shape),
                  _resident(d_skip.shape), _resident(w_glu.shape), _resident(w_a.shape),
                  _resident(w_b.shape)],
        out_specs=tok(d),
        out_shape=jax.ShapeDtypeStruct((t, d), F32),
        compiler_params=_cparams(("parallel",)),
        name="outproj_even",
    )(h, ys, u, b_out, g, d_skip, w_glu, w_a, w_b)


def _proj_even_kernel(h_ref, g_ref, wu_ref, wt_ref, cos_ref, sin_ref,
                      u_ref, q_ref, k_ref, v_ref, gt_ref):
    xn = _rms_rows(h_ref[...], g_ref[2:3, :]).astype(BF16)
    u_ref[...] = _dot(xn, wu_ref[...])
    y = _dot_nt(wt_ref[...], xn)
    tq = y.shape[1]
    cos = cos_ref[...]
    sin = sin_ref[...]
    hd = NSA_HEAD_DIM
    q = _rope_fmaj(y[0:NSA_Q].reshape(NSA_HEADS, hd, tq), cos, sin)
    q_ref[...] = (q * (hd ** -0.5 * LOG2E)).reshape(NSA_Q, tq)
    k0 = NSA_Q
    nk = 3 * NSA_KV
    k = _rope_fmaj(y[k0:k0 + nk].reshape(3 * NSA_KV_HEADS, hd, tq), cos, sin)
    k_ref[...] = k.reshape(nk, tq).astype(k_ref.dtype)
    v0 = k0 + nk
    v_ref[...] = y[v0:v0 + nk].astype(v_ref.dtype)
    g0 = v0 + nk
    gt_ref[...] = jax.nn.sigmoid(y[g0:g0 + 3 * NSA_HEADS])


def _proj_even(h, g, wu, wt, cos, sin, bsz, seq):
    d = h.shape[1]
    tq = min(TQ_PROJ, seq)
    nq = seq // tq
    nk = 3 * NSA_KV
    fm = lambda rows: pl.BlockSpec((None, rows, tq), lambda b, i: (b, 0, i))
    return pl.pallas_call(
        _proj_even_kernel,
        grid=(bsz, nq),
        in_specs=[pl.BlockSpec((tq, d), lambda b, i: (b * nq + i, 0)), _resident(g.shape),
                  _resident(wu.shape), _resident(wt.shape),
                  pl.BlockSpec((cos.shape[0], tq), lambda b, i: (0, i)),
                  pl.BlockSpec((sin.shape[0], tq), lambda b, i: (0, i))],
        out_specs=[pl.BlockSpec((tq, S5_WIDTH), lambda b, i: (b * nq + i, 0)),
                   fm(NSA_Q), fm(nk), fm(nk), fm(3 * NSA_HEADS)],
        out_shape=[jax.ShapeDtypeStruct((bsz * seq, S5_WIDTH), F32),
                   jax.ShapeDtypeStruct((bsz, NSA_Q, seq), F32),
                   jax.ShapeDtypeStruct((bsz, nk, seq), BF16),
                   jax.ShapeDtypeStruct((bsz, nk, seq), BF16),
                   jax.ShapeDtypeStruct((bsz, 3 * NSA_HEADS, seq), F32)],
        compiler_params=_cparams(("parallel", "parallel")),
        name="proj_even",
    )(h, g, wu, wt, cos, sin)


def _proj_odd_kv_kernel(h_ref, g_ref, wt_ref, kvn_ref, cos32_ref, sin32_ref, cos16_ref, sin16_ref,
                        ckv_ref, kr_ref, qi_ref, ki_ref, wi_ref):
    xn = _rms_rows(h_ref[...], g_ref[2:3, :]).astype(BF16)
    y = _dot_nt(wt_ref[...], xn)
    tq = y.shape[1]
    c = y[0:MLA_LATENT]
    c = c * lax.rsqrt(jnp.mean(c * c, axis=0, keepdims=True) + EPS) * kvn_ref[...]
    ckv_ref[...] = c.astype(ckv_ref.dtype)
    r0 = MLA_LATENT
    kr = _rope_fmaj(y[r0:r0 + MLA_ROPE].reshape(1, MLA_ROPE, tq), cos16_ref[...], sin16_ref[...])
    kr_ref[...] = kr.reshape(MLA_ROPE, tq).astype(kr_ref.dtype)
    q0 = r0 + MLA_ROPE
    nqi = IDX_HEADS * IDX_DIM
    qi = _rope_fmaj(y[q0:q0 + nqi].reshape(IDX_HEADS, IDX_DIM, tq), cos32_ref[...], sin32_ref[...])
    qi_ref[...] = qi.reshape(nqi, tq).astype(qi_ref.dtype)
    k0 = q0 + nqi
    ki = _rope_fmaj(y[k0:k0 + IDX_DIM].reshape(1, IDX_DIM, tq), cos32_ref[...], sin32_ref[...])
    ki_ref[...] = ki.reshape(IDX_DIM, tq).astype(ki_ref.dtype)
    w0 = k0 + IDX_DIM
    wi_ref[...] = y[w0:w0 + IDX_HEADS]


def _proj_odd_kv(h, g, wt, kvn, cos32, sin32, cos16, sin16, bsz, seq):
    d = h.shape[1]
    tq = min(TQ_PROJ, seq)
    nq = seq // tq
    fm = lambda rows: pl.BlockSpec((None, rows, tq), lambda b, i: (b, 0, i))
    tab = lambda a: pl.BlockSpec((a.shape[0], tq), lambda b, i: (0, i))
    nqi = IDX_HEADS * IDX_DIM
    return pl.pallas_call(
        _proj_odd_kv_kernel,
        grid=(bsz, nq),
        in_specs=[pl.BlockSpec((tq, d), lambda b, i: (b * nq + i, 0)), _resident(g.shape),
                  _resident(wt.shape), _resident(kvn.shape),
                  tab(cos32), tab(sin32), tab(cos16), tab(sin16)],
        out_specs=[fm(MLA_LATENT), fm(MLA_ROPE), fm(nqi), fm(IDX_DIM), fm(IDX_HEADS)],
        out_shape=[jax.ShapeDtypeStruct((bsz, MLA_LATENT, seq), BF16),
                   jax.ShapeDtypeStruct((bsz, MLA_ROPE, seq), BF16),
                   jax.ShapeDtypeStruct((bsz, nqi, seq), BF16),
                   jax.ShapeDtypeStruct((bsz, IDX_DIM, seq), BF16),
                   jax.ShapeDtypeStruct((bsz, IDX_HEADS, seq), F32)],
        compiler_params=_cparams(("parallel", "parallel")),
        name="proj_odd_kv",
    )(h, g, wt, kvn, cos32, sin32, cos16, sin16)


def _proj_odd_q_kernel(h_ref, g_ref, wt_ref, cos16_ref, sin16_ref, q_ref):
    xn = _rms_rows(h_ref[...], g_ref[2:3, :]).astype(BF16)
    tq = xn.shape[0]
    cos = cos16_ref[...]
    sin = sin16_ref[...]
    for hh in range(MLA_HEADS):
        y = _dot_nt(wt_ref[hh * MLA_QK:(hh + 1) * MLA_QK, :], xn)
        r = _rope_fmaj(y[MLA_LATENT:].reshape(1, MLA_ROPE, tq), cos, sin)
        q = jnp.concatenate([y[:MLA_LATENT], r.reshape(MLA_ROPE, tq)], axis=0)
        q_ref[:, hh * tq:(hh + 1) * tq] = (q * (MLA_QK ** -0.5 * LOG2E)).astype(q_ref.dtype)


def _proj_odd_q(h, g, wt, cos16, sin16, bsz, seq):
    d = h.shape[1]
    tq = min(TQ_DSA, seq)
    nq = seq // tq
    return pl.pallas_call(
        _proj_odd_q_kernel,
        grid=(bsz, nq),
        in_specs=[pl.BlockSpec((tq, d), lambda b, i: (b * nq + i, 0)), _resident(g.shape),
                  _resident(wt.shape),
                  pl.BlockSpec((cos16.shape[0], tq), lambda b, i: (0, i)),
                  pl.BlockSpec((sin16.shape[0], tq), lambda b, i: (0, i))],
        out_specs=pl.BlockSpec((None, None, MLA_QK, MLA_HEADS * tq), lambda b, i: (b, i, 0, 0)),
        out_shape=jax.ShapeDtypeStruct((bsz, nq, MLA_QK, MLA_HEADS * tq), BF16),
        compiler_params=_cparams(("parallel", "parallel")),
        name="proj_odd_q",
    )(h, g, wt, cos16, sin16)


def _s5_state_kernel(u_ref, bc_ref, s_ref):
    s_ref[...] = _dot(u_ref[...], bc_ref[...])


def _s5_scan_kernel(sr_ref, si_ref, lr_ref, li_ref, xr_ref, xi_ref):
    n_chunk = sr_ref.shape[0]
    lr = lr_ref[...]
    li = li_ref[...]

    def body(c, carry):
        xr, xi = carry
        xr_ref[c] = xr
        xi_ref[c] = xi
        return (lr * xr - li * xi + sr_ref[c], lr * xi + li * xr + si_ref[c])

    zero = jnp.zeros(sr_ref.shape[1:], F32)
    lax.fori_loop(0, n_chunk, body, (zero, zero))


def _s5_out_kernel(u_ref, x_ref, m_ref, cc_ref, y_ref):
    x = x_ref[...]
    hi = x.astype(BF16)
    lo = (x - hi.astype(F32)).astype(BF16)
    cc = cc_ref[...]
    y_ref[...] = _dot(u_ref[...], m_ref[...]) + _dot(hi, cc) + _dot(lo, cc)


def _s5_matrices(a_re, a_im, b_re, b_im, c_re, c_im, log_dt, tc):
    hp = lax.Precision.HIGHEST
    dt = jnp.exp(log_dt)[:, None]
    lam = lax.complex(a_re, a_im)
    lam_dt = lam * dt
    lam_bar = jnp.exp(lam_dt)
    b_bar = ((lam_bar - 1.0) / lam)[..., None] * lax.complex(b_re, b_im)
    k = jnp.arange(tc + 1, dtype=F32)[:, None, None]
    pw = jnp.exp(lam_dt[None] * k)
    pr, pi = jnp.real(pw), jnp.imag(pw)
    bbr, bbi = jnp.real(b_bar), jnp.imag(b_bar)
    cpr = c_re[None] * pr[:, :, None, :] - c_im[None] * pi[:, :, None, :]
    cpi = c_re[None] * pi[:, :, None, :] + c_im[None] * pr[:, :, None, :]
    kk = (jnp.einsum('kgpn,gnq->kgpq', cpr[:tc], bbr, precision=hp)
          - jnp.einsum('kgpn,gnq->kgpq', cpi[:tc], bbi, precision=hp))
    n_g, n_p = a_re.shape[0], b_re.shape[2]
    kq = jnp.pad(kk.transpose(1, 3, 0, 2).astype(BF16), ((0, 0), (0, 0), (tc, 0), (0, 0)))
    lag = tc + jnp.arange(tc)[None, :] - jnp.arange(tc)[:, None]
    m = kq[:, :, lag, :].reshape(n_g, n_p * tc, tc * n_p)
    rev = pw[tc - 1 - jnp.arange(tc)]
    bc = rev[:, :, :, None] * b_bar[None]
    bc = bc.transpose(1, 3, 0, 2).reshape(n_g, n_p * tc, -1)
    bc = jnp.concatenate([jnp.real(bc), jnp.imag(bc)], axis=-1)
    mr = cpr[1:tc + 1].transpose(1, 3, 0, 2).reshape(n_g, -1, tc * n_p)
    mi = cpi[1:tc + 1].transpose(1, 3, 0, 2).reshape(n_g, -1, tc * n_p)
    cc = jnp.concatenate([mr, -mi], axis=1)
    ltc = pw[tc].reshape(1, -1)
    return m, bc.astype(BF16), cc.astype(BF16), jnp.real(ltc), jnp.imag(ltc)


def _s5_scan(u, mats, bsz, seq):
    m, bc, cc, lr, li = mats
    tc = S5_CHUNK
    n_c = seq // tc
    n_g, n_p, n_s = S5_GROUPS, S5_GROUP, S5_STATE
    rows = bsz * n_c
    kd = tc * n_p
    ug = (u.reshape(bsz, n_c, tc, n_g, n_p).transpose(3, 0, 1, 4, 2)
          .reshape(n_g, rows, kd).astype(BF16))
    grp = lambda a, b: pl.BlockSpec((None, a, b), lambda gi: (gi, 0, 0))
    s = pl.pallas_call(
        _s5_state_kernel,
        grid=(n_g,),
        in_specs=[grp(rows, kd), grp(kd, 2 * n_s)],
        out_specs=grp(rows, 2 * n_s),
        out_shape=jax.ShapeDtypeStruct((n_g, rows, 2 * n_s), F32),
        compiler_params=_cparams(("parallel",)),
        name="s5_state",
    )(ug, bc)
    s5 = s.reshape(n_g, bsz, n_c, 2, n_s).transpose(3, 2, 1, 0, 4).reshape(2, n_c, bsz, n_g * n_s)
    full = lambda shp: pl.BlockSpec(shp, lambda: (0,) * len(shp))
    xr, xi = pl.pallas_call(
        _s5_scan_kernel,
        in_specs=[full(s5.shape[1:]), full(s5.shape[1:]), full(lr.shape), full(li.shape)],
        out_specs=[full(s5.shape[1:]), full(s5.shape[1:])],
        out_shape=[jax.ShapeDtypeStruct(s5.shape[1:], F32)] * 2,
        compiler_params=pltpu.CompilerParams(vmem_limit_bytes=VMEM_LIMIT),
        name="s5_scan",
    )(s5[0], s5[1], lr, li)
    x = jnp.stack([xr, xi]).reshape(2, n_c, bsz, n_g, n_s).transpose(3, 2, 1, 0, 4)
    x = x.reshape(n_g, rows, 2 * n_s)
    y = pl.pallas_call(
        _s5_out_kernel,
        grid=(n_g,),
        in_specs=[grp(rows, kd), grp(rows, 2 * n_s), grp(kd, kd), grp(2 * n_s, kd)],
        out_specs=grp(rows, kd),
        out_shape=jax.ShapeDtypeStruct((n_g, rows, kd), F32),
        compiler_params=_cparams(("parallel",)),
        name="s5_out",
    )(ug, x, m, cc)
    return (y.reshape(n_g, bsz, n_c, tc, n_p).transpose(1, 2, 3, 0, 4)
            .reshape(bsz * seq, n_g * n_p))


def _compress_kernel(x_ref, pe_ref, w1_ref, w2_ref, o_ref):
    x = x_ref[...].astype(F32)
    nb = x.shape[0]
    a = _dot((x + pe_ref[0:1, :]).astype(BF16), w1_ref[0])
    b = _dot((x + pe_ref[1:2, :]).astype(BF16), w1_ref[1])
    pre = a + pltpu.roll(b, nb - 1, 0)
    o_ref[...] = _dot(jax.nn.gelu(pre).astype(BF16), w2_ref[...]).astype(o_ref.dtype)


def _compress(x, pe, w1, w2):
    bsz, hk, nb, kd = x.shape
    hd = w2.shape[1]
    return pl.pallas_call(
        _compress_kernel,
        grid=(bsz, hk),
        in_specs=[pl.BlockSpec((None, None, nb, kd), lambda b, h: (b, h, 0, 0)),
                  _resident(pe.shape), _resident(w1.shape), _resident(w2.shape)],
        out_specs=pl.BlockSpec((None, None, nb, hd), lambda b, h: (b, h, 0, 0)),
        out_shape=jax.ShapeDtypeStruct((bsz, hk, nb, hd), BF16),
        compiler_params=_cparams(("parallel", "parallel")),
        name="nsa_compress",
    )(x, pe, w1, w2)


def _nsa_kernel(q_ref, g_ref, kc_ref, vct_ref, ks_ref, vst_ref, kw_ref, vwt_ref, ov_ref,
                o_ref, sel_ref, zero_ref, sa_ref, sb_ref, *, tq, tk, top_n):
    gq, hd = NSA_GQA, NSA_HEAD_DIM
    n_sb = sel_ref.shape[0]
    nb = kc_ref.shape[0]
    wd = gq * tq
    i = pl.program_id(2)
    s0 = i * tq
    q = q_ref[...]
    qt = jnp.concatenate([q[g * hd:(g + 1) * hd, :] for g in range(gq)], axis=1).astype(BF16)
    t1 = s0 + lax.broadcasted_iota(I32, (1, tq), 1)
    t4 = s0 + (lax.broadcasted_iota(I32, (1, wd), 1) & (tq - 1))

    sc = _dot(kc_ref[...], qt)
    c_last = lax.broadcasted_iota(I32, (nb, 1), 0) * CMP_STRIDE + (CMP_LEN - 1)
    bias_c = jnp.where(c_last <= t1, 0.0, NEG)
    sm = sc + jnp.concatenate([bias_c] * gq, axis=1)
    e = jnp.exp2(sm - jnp.max(sm, axis=0, keepdims=True))
    p_c = e * jnp.where(t4 >= CMP_LEN - 1, 1.0 / jnp.sum(e, axis=0, keepdims=True), 0.0)
    o_c = _dot(vct_ref[...], p_c.astype(BF16))
    psum = p_c[:, 0:tq]
    for g in range(1, gq):
        psum = psum + p_c[:, g * tq:(g + 1) * tq]
    hi = psum.astype(BF16)
    lo = (psum - hi.astype(F32)).astype(BF16)
    imp = _dot(ov_ref[...], hi) + _dot(ov_ref[...], lo)
    blk = lax.broadcasted_iota(I32, (n_sb, 1), 0)
    cur = t1 >> int(math.log2(SEL_BLOCK))
    forced = (blk == 0) | (blk == cur) | (blk == cur - 1)
    imp = imp + jnp.where(forced, FORCE_BONUS, 0.0)
    imp = jnp.where(blk * SEL_BLOCK <= t1, imp, NEG)

    rowf = lax.broadcasted_iota(I32, (n_sb, tq), 0).astype(F32)
    sel = jnp.zeros((n_sb, tq), F32)
    x = imp
    for _ in range(top_n):
        mx = jnp.max(x, axis=0, keepdims=True)
        first = jnp.min(jnp.where(x == mx, rowf, float(n_sb)), axis=0, keepdims=True)
        hit = rowf == first
        sel = jnp.where(hit, 1.0, sel)
        x = jnp.where(hit, -jnp.inf, x)
    sel_ref[...] = sel

    zero_ref[...] = jnp.zeros(zero_ref.shape, F32)
    init = (jnp.full((1, wd), NEG, F32), jnp.zeros((1, wd), F32), zero_ref[...])

    bpt = tk // SEL_BLOCK
    key_row = lax.broadcasted_iota(I32, (tk, 1), 0)

    last_tile = ks_ref.shape[0] - 1

    def scores(j):
        return _dot(ks_ref[jnp.minimum(j, last_tile)], qt)

    def consume(j, s_ref, carry):
        m, l, acc, pv = carry
        jc = jnp.minimum(j, last_tile)
        rows = [jnp.broadcast_to(sel_ref[pl.ds(jc * bpt + b, 1), :], (SEL_BLOCK, tq))
                for b in range(bpt)]
        keep = (jnp.concatenate(rows, axis=0) > 0.5) & (j * tk + key_row <= t1)
        bias = jnp.where(keep, 0.0, NEG)
        s = s_ref[...] + jnp.concatenate([bias] * gq, axis=1)
        m_new = jnp.maximum(m, jnp.max(s, axis=0, keepdims=True))
        alpha = jnp.exp2(m - m_new)
        p = jnp.exp2(s - m_new)
        l = alpha * l + jnp.sum(p, axis=0, keepdims=True)
        return m_new, l, alpha * (acc + pv), _dot(vst_ref[jc], p.astype(BF16))

    def tile_pair(k, carry):
        j = 2 * k
        sb_ref[...] = scores(j + 1)
        carry = consume(j, sa_ref, carry)
        sa_ref[...] = scores(j + 2)
        return consume(j + 1, sb_ref, carry)

    sa_ref[...] = scores(0)
    _, l_s, a_s, pv_s = lax.fori_loop(0, (s0 // tk + 2) // 2, tile_pair, init + (zero_ref[...],))
    o_s = (a_s + pv_s) / l_s

    n_wt = (WINDOW + tq) // LANE
    j_lo = jnp.clip(i * (tq // LANE) - WINDOW // LANE, 0, kw_ref.shape[0] - n_wt)
    kwin = jnp.concatenate([kw_ref[j_lo + w] for w in range(n_wt)], axis=0)
    vwin = jnp.concatenate([vwt_ref[j_lo + w] for w in range(n_wt)], axis=1)
    sw = _dot(kwin, qt)
    diff = t1 - (j_lo * LANE + lax.broadcasted_iota(I32, (n_wt * LANE, 1), 0))
    sw = sw + jnp.concatenate([jnp.where((diff >= 0) & (diff < WINDOW), 0.0, NEG)] * gq, axis=1)
    pw = jnp.exp2(sw - jnp.max(sw, axis=0, keepdims=True))
    o_w = _dot(vwin, pw.astype(BF16)) / jnp.sum(pw, axis=0, keepdims=True)

    gt = g_ref[...]
    outs = []
    for g in range(gq):
        cs = slice(g * tq, (g + 1) * tq)
        outs.append(gt[3 * g:3 * g + 1, :] * o_c[:, cs] + gt[3 * g + 1:3 * g + 2, :] * o_s[:, cs]
                    + gt[3 * g + 2:3 * g + 3, :] * o_w[:, cs])
    o_ref[...] = jnp.concatenate(outs, axis=0).T


def _nsa(q, gates, kcmp, vcmp_t, ks, vs_t, kw, vw_t, ov_t, seq):
    bsz = q.shape[0]
    tq = min(TQ_NSA, seq)
    tk = min(TK_NSA, seq)
    n_sb = seq // SEL_BLOCK
    gq, hd = NSA_GQA, NSA_HEAD_DIM
    per_head = lambda a: pl.BlockSpec((None, None) + a.shape[2:],
                                      lambda b, h, i: (b, h) + (0,) * (a.ndim - 2),
                                      pipeline_mode=pl.Buffered(1))
    return pl.pallas_call(
        functools.partial(_nsa_kernel, tq=tq, tk=tk, top_n=min(SEL_TOP, n_sb)),
        grid=(bsz, NSA_KV_HEADS, seq // tq),
        in_specs=[pl.BlockSpec((None, gq * hd, tq), lambda b, h, i: (b, h, i)),
                  pl.BlockSpec((None, None, 3 * gq, tq), lambda b, h, i: (b, h, 0, i)),
                  per_head(kcmp), per_head(vcmp_t), per_head(ks), per_head(vs_t),
                  per_head(kw), per_head(vw_t), _resident(ov_t.shape)],
        out_specs=pl.BlockSpec((tq, gq * hd), lambda b, h, i: (b * (seq // tq) + i, h)),
        out_shape=jax.ShapeDtypeStruct((bsz * seq, NSA_Q), F32),
        scratch_shapes=[pltpu.VMEM((n_sb, tq), F32), pltpu.VMEM((hd, gq * tq), F32),
                        pltpu.VMEM((tk, gq * tq), F32), pltpu.VMEM((tk, gq * tq), F32)],
        compiler_params=_cparams(("parallel", "parallel", "arbitrary")),
        name="nsa",
    )(q, gates, kcmp, vcmp_t, ks, vs_t, kw, vw_t, ov_t)


def _dsa_kernel(h_ref, g_ref, q_ref, qi_ref, wi_ref, kcat_ref, ckt_ref, kidx_ref, wuv_ref, wout_ref,
                o_ref, ihi, ilo, acc_ref, m_ref, l_ref, tie_cap, *, tq, top):
    tk = tq
    n_h = MLA_HEADS
    i16 = jnp.int16
    i = pl.program_id(1)
    row = lax.broadcasted_iota(I32, (tk, 1), 0)
    col = lax.broadcasted_iota(I32, (1, tq), 1)
    causal = row <= col

    def tile(ref, j):
        return ref.at[pl.ds(pl.multiple_of(j * tk, tk), tk), :]

    qi = qi_ref[...]
    wi = wi_ref[...] * (IDX_HEADS ** -0.5) * (IDX_DIM ** -0.5)

    def idx_scores(j):
        kj = kidx_ref[j]
        sc = jnp.zeros((tk, tq), F32)
        for h in range(IDX_HEADS):
            d = _dot(kj, qi[h * IDX_DIM:(h + 1) * IDX_DIM, :])
            sc = sc + wi[h:h + 1, :] * jnp.maximum(d, 0.0)
        return sc

    def store_keys(j, sc):
        b = pltpu.bitcast(sc, I32)
        key = b ^ ((b >> 31) & 0x7FFFFFFF)
        tile(ihi, j)[...] = (key >> 16).astype(i16)
        tile(ilo, j)[...] = ((key & 0xFFFF) - 32768).astype(i16)

    def idx_tile(j, _):
        store_keys(j, idx_scores(j))
        return 0

    lax.fori_loop(0, i, idx_tile, 0)
    store_keys(i, jnp.where(causal, idx_scores(i), NEG))

    @pl.when(((i + 1) * tq) % kcat_ref.shape[1] != 0)
    def _():
        store_keys(i + 1, jnp.full((tk, tq), NEG, F32))

    def count(ref, pred):
        def body(j, c):
            hit = jnp.where(pred(tile(ref, j)[...]), jnp.ones((), i16), jnp.zeros((), i16))
            parts = [hit[k * 16:(k + 1) * 16] for k in range(tk // 16)]
            while len(parts) > 1:
                parts = [a + b for a, b in zip(parts[0::2], parts[1::2])]
            return c + parts[0]
        c16 = lax.fori_loop(0, i + 1, body, jnp.zeros((16, tq), i16))
        return jnp.sum(c16.astype(I32), axis=0, keepdims=True)

    def search(ref, need, n_all):
        def bit_step(b, st):
            thr, c_ge, c_gt = st
            cand = thr + (jnp.int32(1) << (15 - b))
            cand16 = cand.astype(i16)
            cnt = count(ref, lambda t: t >= cand16)
            ok = cnt >= need
            return jnp.where(ok, cand, thr), jnp.where(ok, cnt, c_ge), jnp.where(ok, c_gt, cnt)
        init = (jnp.full((1, tq), -32768, I32), n_all, jnp.zeros((1, tq), I32))
        return lax.fori_loop(0, 16, bit_step, init)

    thr_hi, n_ge_hi, n_gt_hi = search(ihi, top, jnp.full((1, tq), 1, I32) * ((i + 1) * tk))
    thr_hi16 = thr_hi.astype(i16)
    need = top - n_gt_hi
    n_eq_hi = n_ge_hi - n_gt_hi

    def mask_lo(j, _):
        lo = tile(ilo, j)
        lo[...] = jnp.where(tile(ihi, j)[...] == thr_hi16, lo[...], jnp.full((), -32768, i16))
        return 0

    lax.fori_loop(0, i + 1, mask_lo, 0)
    thr_lo, n_ge_lo, _ = search(ilo, need, n_eq_hi)

    n_sel = n_gt_hi + n_ge_lo
    tie_cap[...] = jnp.full((1, tq), ihi.shape[0], I32)
    excess = n_sel - top
    max_excess = jnp.max(excess)
    n_bits = (ihi.shape[0] - 1).bit_length()

    thr_lo16 = thr_lo.astype(i16)

    def tie_diff(j):
        return ((tile(ihi, j)[...] ^ thr_hi16) | (tile(ilo, j)[...] ^ thr_lo16)).astype(I32)

    n_top = 3
    max_scans = 2 * n_bits // n_top

    @pl.when((max_excess > 0) & (max_excess <= n_top * max_scans))
    def _():
        def drop(_, st):
            cap, left = st

            def body(j, tops):
                idx = j * tk + row
                v = jnp.where(tie_diff(j) == 0, jnp.where(idx <= cap, idx, -1), -1)
                v = v.reshape(tk // 8, 8, tq)
                for k in range(tk // 8):
                    t = v[k]
                    new = []
                    for m in tops:
                        new.append(jnp.maximum(m, t))
                        t = jnp.minimum(m, t)
                    tops = tuple(new)
                return tops

            tops = list(lax.fori_loop(0, i + 1, body, (jnp.full((8, tq), -1, I32),) * n_top))
            take = jnp.minimum(left, n_top)
            for r in range(n_top):
                g = jnp.max(functools.reduce(jnp.maximum, tops), axis=0, keepdims=True)
                cap = jnp.where(take == r + 1, g - 1, cap)
                tops = [jnp.where(m == g, -1, m) for m in tops]
            return cap, left - take

        n_scan = (max_excess + n_top - 1) // n_top
        tie_cap[...] = lax.fori_loop(0, n_scan, drop, (tie_cap[...], excess))[0]

    @pl.when(max_excess > n_top * max_scans)
    def _():
        def count_ties(below):
            def body(j, c):
                hit = jnp.where(tie_diff(j) == 0, jnp.where(j * tk + row < below, 1, 0), 0)
                return c + jnp.sum(hit.reshape(tk // 8, 8, tq), axis=0)
            c8 = lax.fori_loop(0, i + 1, body, jnp.zeros((8, tq), I32))
            return jnp.sum(c8, axis=0, keepdims=True)

        need_eq = count_ties(ihi.shape[0]) - excess

        def bit_step(b, cap):
            cand = cap + (jnp.int32(1) << (n_bits - 1 - b))
            return jnp.where(count_ties(cand) >= need_eq, cap, cand)

        tie_cap[...] = lax.fori_loop(0, n_bits, bit_step, jnp.zeros((1, tq), I32))

    m_ref[...] = jnp.full(m_ref.shape, NEG, F32)
    l_ref[...] = jnp.zeros(l_ref.shape, F32)
    acc_ref[...] = jnp.zeros(acc_ref.shape, F32)
    cap = tie_cap[...]
    thr_key = (thr_hi << 16) | (thr_lo + 32768)

    ta = kcat_ref.shape[1]
    a_row = lax.broadcasted_iota(I32, (ta, 1), 0)
    q_pos = i * tq + col

    def att_tile(j, _):
        rows = pl.ds(pl.multiple_of(j * ta, ta), ta)
        hi = ihi[rows, :].astype(I32)
        lo = ilo[rows, :].astype(I32)
        k_pos = j * ta + a_row
        key = (hi << 16) | (lo + 32768)
        sel = jnp.where(key > thr_key, 0.0,
                        jnp.where(key == thr_key, jnp.where(k_pos <= cap, 0.0, NEG), NEG))
        bias = jnp.where(k_pos <= q_pos, sel, NEG)
        s_all = _dot(kcat_ref[j], q_ref[...])
        ps, alphas = [], []
        for h in range(n_h):
            cs = slice(h * tq, (h + 1) * tq)
            s = s_all[:, cs] + bias
            m_old = m_ref[:, cs]
            m_new = jnp.maximum(m_old, jnp.max(s, axis=0, keepdims=True))
            alpha = jnp.exp2(m_old - m_new)
            p = jnp.exp2(s - m_new)
            l_ref[:, cs] = alpha * l_ref[:, cs] + jnp.sum(p, axis=0, keepdims=True)
            m_ref[:, cs] = m_new
            ps.append(p.astype(BF16))
            alphas.append(alpha)
        p_all = jnp.concatenate(ps, axis=1)
        acc_ref[...] = acc_ref[...] * jnp.concatenate(alphas, axis=1) + _dot(ckt_ref[j], p_all)
        return 0

    lax.fori_loop(0, (i * tq) // ta + 1, att_tile, 0)

    outs = []
    for h in range(n_h):
        cs = slice(h * tq, (h + 1) * tq)
        o = acc_ref[:, cs] / l_ref[:, cs]
        outs.append(_dot(wuv_ref[h], o.astype(BF16)))
    mix = jnp.concatenate(outs, axis=0).T
    y = _dot(mix.astype(BF16), wout_ref[...])
    o_ref[...] = h_ref[...] + _rms_rows(y, g_ref[3:4, :])


def _dsa(h, g, q, qi, wi, kcat, ckt, kidx, wuv_t, w_out, seq):
    bsz = q.shape[0]
    d = h.shape[1]
    tq = min(TQ_DSA, seq)
    nq = seq // tq
    top = min(DSA_TOP, seq // 4)
    wd = MLA_HEADS * tq
    per_b = lambda a: pl.BlockSpec((None,) + a.shape[1:], lambda b, i: (b,) + (0,) * (a.ndim - 1),
                                   pipeline_mode=pl.Buffered(1))
    return pl.pallas_call(
        functools.partial(_dsa_kernel, tq=tq, top=top),
        grid=(bsz, nq),
        in_specs=[pl.BlockSpec((tq, d), lambda b, i: (b * nq + i, 0)), _resident(g.shape),
                  pl.BlockSpec((None, None, MLA_QK, wd), lambda b, i: (b, i, 0, 0)),
                  pl.BlockSpec((None, IDX_HEADS * IDX_DIM, tq), lambda b, i: (b, 0, i)),
                  pl.BlockSpec((None, IDX_HEADS, tq), lambda b, i: (b, 0, i)),
                  per_b(kcat), per_b(ckt), per_b(kidx), _resident(wuv_t.shape),
                  _resident(w_out.shape)],
        out_specs=pl.BlockSpec((tq, d), lambda b, i: (b * nq + i, 0)),
        out_shape=jax.ShapeDtypeStruct(h.shape, F32),
        scratch_shapes=[pltpu.VMEM((seq, tq), jnp.int16),
                        pltpu.VMEM((seq, tq), jnp.int16),
                        pltpu.VMEM((MLA_LATENT, wd), F32),
                        pltpu.VMEM((1, wd), F32),
                        pltpu.VMEM((1, wd), F32),
                        pltpu.VMEM((1, tq), I32)],
        compiler_params=_cparams(("parallel", "arbitrary")),
        name="dsa",
    )(h, g, q, qi, wi, kcat, ckt, kidx, wuv_t, w_out)


def _rope_tables(seq, dim):
    half = dim // 2
    inv = ROPE_THETA ** (-jnp.arange(half, dtype=F32) / half)
    ang = inv[:, None] * jnp.arange(seq, dtype=F32)[None, :]
    return jnp.cos(ang), jnp.sin(ang)


def _even_mixer(h, g, bsz, seq, w_in, w_out, s5, pe_k, pe_v, wk1, wk2, wv1, wv2, tabs):
    cos32, sin32 = tabs[64]
    hd, hk = NSA_HEAD_DIM, NSA_KV_HEADS
    o = 0
    cols = {}
    for name, size in (("u", S5_WIDTH), ("q", NSA_Q), ("kc", NSA_KV), ("vc", NSA_KV), ("ks", NSA_KV),
                       ("vs", NSA_KV), ("kw", NSA_KV), ("vw", NSA_KV), ("gt", 3 * NSA_HEADS)):
        cols[name] = w_in[:, o:o + size]
        o += size
    wu = cols["u"].astype(BF16)
    wt = jnp.concatenate([cols[n] for n in ("q", "kc", "ks", "kw", "vc", "vs", "vw", "gt")], axis=1).T
    wt = jnp.pad(wt, ((0, (-wt.shape[0]) % 16), (0, 0))).astype(BF16)
    u, q_t, k_t, v_t, gt_t = _proj_even(h, g, wu, wt, cos32, sin32, bsz, seq)

    a_re, a_im, b_re, b_im, c_re, c_im, log_dt, d_skip, w_glu = s5
    mats = _s5_matrices(a_re, a_im, b_re, b_im, c_re, c_im, log_dt, S5_CHUNK)
    ys = _s5_scan(u, mats, bsz, seq)

    k5 = k_t.reshape(bsz, 3, hk, hd, seq)
    v5 = v_t.reshape(bsz, 3, hk, hd, seq)
    nb = seq // CMP_STRIDE
    half_blk = lambda a: (a.reshape(bsz, hk, hd, nb, CMP_STRIDE).transpose(0, 1, 3, 4, 2)
                          .reshape(bsz, hk, nb, CMP_STRIDE * hd))
    pe2 = lambda pe: pe.reshape(2, CMP_STRIDE * hd)
    w1s = lambda w: w.reshape(2, CMP_STRIDE * hd, hd).astype(BF16)
    kcmp = _compress(half_blk(k5[:, 0]), pe2(pe_k), w1s(wk1), wk2.astype(BF16))
    vcmp = _compress(half_blk(v5[:, 0]), pe2(pe_v), w1s(wv1), wv2.astype(BF16))
    vcmp_t = vcmp.transpose(0, 1, 3, 2)
    tk = min(TK_NSA, seq)
    ks = k5[:, 1].reshape(bsz, hk, hd, seq // tk, tk).transpose(0, 1, 3, 4, 2)
    vs_t = v5[:, 1].reshape(bsz, hk, hd, seq // tk, tk).transpose(0, 1, 3, 2, 4)
    kw = k5[:, 2].reshape(bsz, hk, hd, seq // LANE, LANE).transpose(0, 1, 3, 4, 2)
    vw_t = v5[:, 2].reshape(bsz, hk, hd, seq // LANE, LANE).transpose(0, 1, 3, 2, 4)
    n_sb = seq // SEL_BLOCK
    c_start = jnp.arange(nb) * CMP_STRIDE
    b_start = jnp.arange(n_sb) * SEL_BLOCK
    ov_t = ((c_start[None, :] < b_start[:, None] + SEL_BLOCK)
            & (c_start[None, :] + CMP_LEN > b_start[:, None])
            & (jnp.arange(nb)[None, :] < nb - 1)).astype(BF16)
    gates = gt_t.reshape(bsz, hk, 3 * NSA_GQA, seq)
    b_out = _nsa(q_t, gates, kcmp, vcmp_t, ks, vs_t, kw, vw_t, ov_t, seq)
    return _outproj_even(h, ys, u, b_out, g, d_skip.reshape(1, -1), w_glu.astype(BF16),
                         w_out[:S5_WIDTH].astype(BF16), w_out[S5_WIDTH:].astype(BF16))


def _odd_mixer(h, g, bsz, seq, w_in, kv_norm, w_uv, w_out, tabs):
    cos32, sin32 = tabs[64]
    cos16, sin16 = tabs[32]
    d = h.shape[1]
    sizes = (MLA_HEADS * MLA_LATENT, MLA_HEADS * MLA_ROPE, MLA_LATENT, MLA_ROPE,
             IDX_HEADS * IDX_DIM, IDX_DIM, IDX_HEADS)
    parts = []
    o = 0
    for s in sizes:
        parts.append(w_in[:, o:o + s])
        o += s
    w_ql, w_qr, w_c, w_kr, w_qi, w_ki, w_wi = parts
    wq = jnp.concatenate([w_ql.reshape(d, MLA_HEADS, MLA_LATENT), w_qr.reshape(d, MLA_HEADS, MLA_ROPE)],
                         axis=2).reshape(d, MLA_HEADS * MLA_QK).T.astype(BF16)
    wkv = jnp.concatenate([w_c, w_kr, w_qi, w_ki, w_wi], axis=1).T
    wkv = jnp.pad(wkv, ((0, (-wkv.shape[0]) % 16), (0, 0))).astype(BF16)
    ckv_t, kr_t, qi_t, ki_t, wi_t = _proj_odd_kv(h, g, wkv, kv_norm.reshape(-1, 1), cos32, sin32,
                                                 cos16, sin16, bsz, seq)
    q = _proj_odd_q(h, g, wq, cos16, sin16, bsz, seq)
    tk = min(TQ_DSA, seq)
    ta = min(TA_DSA, seq)
    kcat = jnp.concatenate([ckv_t, kr_t], axis=1).transpose(0, 2, 1).reshape(bsz, seq // ta, ta, MLA_QK)
    ckt = ckv_t.reshape(bsz, MLA_LATENT, seq // ta, ta).transpose(0, 2, 1, 3)
    kidx = ki_t.transpose(0, 2, 1).reshape(bsz, seq // tk, tk, IDX_DIM)
    wuv_t = w_uv.transpose(0, 2, 1).astype(BF16)
    return _dsa(h, g, q, qi_t, wi_t, kcat, ckt, kidx, wuv_t, w_out.astype(BF16), seq)


def kernel(x, p, norm_g, ffn1_w_in, ffn1_w_out, ffn2_w_in, ffn2_w_out, ple_w_gate, ple_w_proj,
           ev_w_in, ev_w_out, s5_a_re, s5_a_im, s5_b_re, s5_b_im, s5_c_re, s5_c_im, s5_log_dt, s5_d,
           s5_w_glu, nsa_pe_k, nsa_pe_v, nsa_wk1, nsa_wk2, nsa_wv1, nsa_wv2,
           od_w_in, od_kv_norm, od_w_uv, od_w_out):
    bsz, seq, d = x.shape
    depth = norm_g.shape[0]
    tabs = {64: _rope_tables(seq, 64), 32: _rope_tables(seq, 32)}
    h = x.reshape(bsz * seq, d)
    for i in range(depth):
        g = norm_g[i]
        j = i // 2
        h = _ffn(h, g, ffn1_w_in[i].astype(BF16), ffn1_w_out[i].astype(BF16), 0)
        if i % 2 == 0:
            s5 = (s5_a_re[j], s5_a_im[j], s5_b_re[j], s5_b_im[j], s5_c_re[j], s5_c_im[j],
                  s5_log_dt[j], s5_d[j], s5_w_glu[j])
            h = _even_mixer(h, g, bsz, seq, ev_w_in[j], ev_w_out[j], s5, nsa_pe_k[j], nsa_pe_v[j],
                            nsa_wk1[j], nsa_wk2[j], nsa_wv1[j], nsa_wv2[j], tabs)
        else:
            h = _odd_mixer(h, g, bsz, seq, od_w_in[j], od_kv_norm[j], od_w_uv[j], od_w_out[j], tabs)
        h = _ffn_ple(h, p[i].reshape(bsz * seq, -1), g, ffn2_w_in[i].astype(BF16),
                     ffn2_w_out[i].astype(BF16), ple_w_gate[i].astype(BF16),
                     ple_w_proj[i].astype(BF16))
    return h.reshape(bsz, seq, d)
```

```python
import functools
import math

import jax
import jax.numpy as jnp
from jax import lax
from jax.experimental import pallas as pl
from jax.experimental.pallas import tpu as pltpu

F32 = jnp.float32
BF16 = jnp.bfloat16
I32 = jnp.int32

ROPE_THETA = 10000.0
EPS = 1e-6
NEG = -1e30
LOG2E = math.log2(math.e)
D_FF = 2816
S5_WIDTH = 512
S5_GROUP = 16
S5_GROUPS = S5_WIDTH // S5_GROUP
S5_STATE = 64
NSA_HEADS = 8
NSA_KV_HEADS = 2
NSA_GQA = NSA_HEADS // NSA_KV_HEADS
NSA_HEAD_DIM = 64
CMP_LEN = 32
CMP_STRIDE = 16
SEL_BLOCK = 64
SEL_TOP = 16
WINDOW = 512
FORCE_BONUS = 1000.0
NSA_Q = NSA_HEADS * NSA_HEAD_DIM
NSA_KV = NSA_KV_HEADS * NSA_HEAD_DIM
MLA_HEADS = 16
MLA_LATENT = 256
MLA_ROPE = 32
MLA_QK = MLA_LATENT + MLA_ROPE
MLA_V_DIM = 64
IDX_HEADS = 8
IDX_DIM = 64
DSA_TOP = 256

V7X_VMEM_BYTES = 64 * 2**20
VMEM_LIMIT = V7X_VMEM_BYTES - 8 * 2**20
LANE = 128
TM_FFN = 1024
TQ_PROJ = 1024
TQ_NSA = 256
TK_NSA = 512
TQ_DSA = 256
TA_DSA = 512
S5_CHUNK = 32


def _cparams(sem):
    return pltpu.CompilerParams(dimension_semantics=sem, vmem_limit_bytes=VMEM_LIMIT)


def _resident(shape):
    nd = len(shape)
    return pl.BlockSpec(shape, lambda *_: (0,) * nd, pipeline_mode=pl.Buffered(1))


def _dot(a, b):
    return jnp.dot(a, b, preferred_element_type=F32)


def _dot_nt(a, b):
    return lax.dot_general(a, b, (((1,), (1,)), ((), ())), preferred_element_type=F32)


def _rms_rows(x, g):
    return x * lax.rsqrt(jnp.mean(x * x, axis=-1, keepdims=True) + EPS) * g


def _rope_fmaj(y, cos, sin):
    half = y.shape[1] // 2
    t1 = y[:, :half, :]
    t2 = y[:, half:, :]
    return jnp.concatenate([t1 * cos - t2 * sin, t2 * cos + t1 * sin], axis=1)


def _ffn_tile(x, g_ref, g0, win_ref, wout_ref, n_chunk):
    xn = _rms_rows(x, g_ref[g0:g0 + 1, :]).astype(BF16)
    ck = D_FF // n_chunk
    acc = None
    for c in range(n_chunk):
        a = _dot(xn, win_ref[:, c * ck:(c + 1) * ck])
        u = _dot(xn, win_ref[:, D_FF + c * ck:D_FF + (c + 1) * ck])
        act = (jax.nn.silu(a) * u).astype(BF16)
        y = _dot(act, wout_ref[c * ck:(c + 1) * ck, :])
        acc = y if acc is None else acc + y
    return x + 0.5 * _rms_rows(acc, g_ref[g0 + 1:g0 + 2, :])


def _ffn_kernel(h_ref, g_ref, win_ref, wout_ref, o_ref, *, g0, n_chunk):
    o_ref[...] = _ffn_tile(h_ref[...], g_ref, g0, win_ref, wout_ref, n_chunk)


def _ffn_ple_kernel(h_ref, p_ref, g_ref, win_ref, wout_ref, wg_ref, wp_ref, o_ref, *, n_chunk):
    x = _ffn_tile(h_ref[...], g_ref, 4, win_ref, wout_ref, n_chunk)
    gate = jax.nn.sigmoid(_dot(_rms_rows(x, g_ref[6:7, :]).astype(BF16), wg_ref[...]))
    e = _dot(p_ref[...].astype(BF16), wp_ref[...]) * gate
    o_ref[...] = x + _rms_rows(e, g_ref[7:8, :])


def _ffn_ple(h, p, g, w_in, w_out, w_gate, w_proj):
    t, d = h.shape
    tm = min(TM_FFN, t)
    return pl.pallas_call(
        functools.partial(_ffn_ple_kernel, n_chunk=2),
        grid=(t // tm,),
        in_specs=[pl.BlockSpec((tm, d), lambda i: (i, 0)),
                  pl.BlockSpec((tm, p.shape[1]), lambda i: (i, 0)), _resident(g.shape),
                  _resident(w_in.shape), _resident(w_out.shape), _resident(w_gate.shape),
                  _resident(w_proj.shape)],
        out_specs=pl.BlockSpec((tm, d), lambda i: (i, 0)),
        out_shape=jax.ShapeDtypeStruct((t, d), F32),
        compiler_params=_cparams(("parallel",)),
        name="ffn_ple",
    )(h, p, g, w_in, w_out, w_gate, w_proj)


def _ffn(h, g, w_in, w_out, g0):
    t, d = h.shape
    tm = min(TM_FFN, t)
    return pl.pallas_call(
        functools.partial(_ffn_kernel, g0=g0, n_chunk=2),
        grid=(t // tm,),
        in_specs=[pl.BlockSpec((tm, d), lambda i: (i, 0)), _resident(g.shape),
                  _resident(w_in.shape), _resident(w_out.shape)],
        out_specs=pl.BlockSpec((tm, d), lambda i: (i, 0)),
        out_shape=jax.ShapeDtypeStruct((t, d), F32),
        compiler_params=_cparams(("parallel",)),
        name="ffn",
    )(h, g, w_in, w_out)


def _outproj_even_kernel(h_ref, ys_ref, u_ref, b_ref, g_ref, d_ref, wglu_ref, wa_ref, wb_ref, o_ref):
    y = ys_ref[...] + d_ref[...] * u_ref[...]
    z = jax.nn.gelu(y)
    a = z * jax.nn.sigmoid(_dot(z.astype(BF16), wglu_ref[...]))
    mix = _dot(a.astype(BF16), wa_ref[...]) + _dot(b_ref[...].astype(BF16), wb_ref[...])
    o_ref[...] = h_ref[...] + _rms_rows(mix, g_ref[3:4, :])


def _outproj_even(h, ys, u, b_out, g, d_skip, w_glu, w_a, w_b):
    t, d = h.shape
    tm = min(TM_FFN, t)
    tok = lambda w: pl.BlockSpec((tm, w), lambda i: (i, 0))
    return pl.pallas_call(
        _outproj_even_kernel,
        grid=(t // tm,),
        in_specs=[tok(d), tok(S5_WIDTH), tok(S5_WIDTH), tok(NSA_Q), _resident(g.shape),
                  _resident(d_skip.shape), _resident(w_glu.shape), _resident(w_a.shape),
                  _resident(w_b.shape)],
        out_specs=tok(d),
        out_shape=jax.ShapeDtypeStruct((t, d), F32),
        compiler_params=_cparams(("parallel",)),
        name="outproj_even",
    )(h, ys, u, b_out, g, d_skip, w_glu, w_a, w_b)


def _proj_even_kernel(h_ref, g_ref, wu_ref, wt_ref, cos_ref, sin_ref,
                      u_ref, q_ref, k_ref, v_ref, gt_ref):
    xn = _rms_rows(h_ref[...], g_ref[2:3, :]).astype(BF16)
    u_ref[...] = _dot(xn, wu_ref[...])
    y = _dot_nt(wt_ref[...], xn)
    tq = y.shape[1]
    cos = cos_ref[...]
    sin = sin_ref[...]
    hd = NSA_HEAD_DIM
    q = _rope_fmaj(y[0:NSA_Q].reshape(NSA_HEADS, hd, tq), cos, sin)
    q_ref[...] = (q * (hd ** -0.5 * LOG2E)).reshape(NSA_Q, tq)
    k0 = NSA_Q
    nk = 3 * NSA_KV
    k = _rope_fmaj(y[k0:k0 + nk].reshape(3 * NSA_KV_HEADS, hd, tq), cos, sin)
    k_ref[...] = k.reshape(nk, tq).astype(k_ref.dtype)
    v0 = k0 + nk
    v_ref[...] = y[v0:v0 + nk].astype(v_ref.dtype)
    g0 = v0 + nk
    gt_ref[...] = jax.nn.sigmoid(y[g0:g0 + 3 * NSA_HEADS])


def _proj_even(h, g, wu, wt, cos, sin, bsz, seq):
    d = h.shape[1]
    tq = min(TQ_PROJ, seq)
    nq = seq // tq
    nk = 3 * NSA_KV
    fm = lambda rows: pl.BlockSpec((None, rows, tq), lambda b, i: (b, 0, i))
    return pl.pallas_call(
        _proj_even_kernel,
        grid=(bsz, nq),
        in_specs=[pl.BlockSpec((tq, d), lambda b, i: (b * nq + i, 0)), _resident(g.shape),
                  _resident(wu.shape), _resident(wt.shape),
                  pl.BlockSpec((cos.shape[0], tq), lambda b, i: (0, i)),
                  pl.BlockSpec((sin.shape[0], tq), lambda b, i: (0, i))],
        out_specs=[pl.BlockSpec((tq, S5_WIDTH), lambda b, i: (b * nq + i, 0)),
                   fm(NSA_Q), fm(nk), fm(nk), fm(3 * NSA_HEADS)],
        out_shape=[jax.ShapeDtypeStruct((bsz * seq, S5_WIDTH), F32),
                   jax.ShapeDtypeStruct((bsz, NSA_Q, seq), F32),
                   jax.ShapeDtypeStruct((bsz, nk, seq), BF16),
                   jax.ShapeDtypeStruct((bsz, nk, seq), BF16),
                   jax.ShapeDtypeStruct((bsz, 3 * NSA_HEADS, seq), F32)],
        compiler_params=_cparams(("parallel", "parallel")),
        name="proj_even",
    )(h, g, wu, wt, cos, sin)


def _proj_odd_kv_kernel(h_ref, g_ref, wt_ref, kvn_ref, cos32_ref, sin32_ref, cos16_ref, sin16_ref,
                        ckv_ref, kr_ref, qi_ref, ki_ref, wi_ref):
    xn = _rms_rows(h_ref[...], g_ref[2:3, :]).astype(BF16)
    y = _dot_nt(wt_ref[...], xn)
    tq = y.shape[1]
    c = y[0:MLA_LATENT]
    c = c * lax.rsqrt(jnp.mean(c * c, axis=0, keepdims=True) + EPS) * kvn_ref[...]
    ckv_ref[...] = c.astype(ckv_ref.dtype)
    r0 = MLA_LATENT
    kr = _rope_fmaj(y[r0:r0 + MLA_ROPE].reshape(1, MLA_ROPE, tq), cos16_ref[...], sin16_ref[...])
    kr_ref[...] = kr.reshape(MLA_ROPE, tq).astype(kr_ref.dtype)
    q0 = r0 + MLA_ROPE
    nqi = IDX_HEADS * IDX_DIM
    qi = _rope_fmaj(y[q0:q0 + nqi].reshape(IDX_HEADS, IDX_DIM, tq), cos32_ref[...], sin32_ref[...])
    qi_ref[...] = qi.reshape(nqi, tq).astype(qi_ref.dtype)
    k0 = q0 + nqi
    ki = _rope_fmaj(y[k0:k0 + IDX_DIM].reshape(1, IDX_DIM, tq), cos32_ref[...], sin32_ref[...])
    ki_ref[...] = ki.reshape(IDX_DIM, tq).astype(ki_ref.dtype)
    w0 = k0 + IDX_DIM
    wi_ref[...] = y[w0:w0 + IDX_HEADS]


def _proj_odd_kv(h, g, wt, kvn, cos32, sin32, cos16, sin16, bsz, seq):
    d = h.shape[1]
    tq = min(TQ_PROJ, seq)
    nq = seq // tq
    fm = lambda rows: pl.BlockSpec((None, rows, tq), lambda b, i: (b, 0, i))
    tab = lambda a: pl.BlockSpec((a.shape[0], tq), lambda b, i: (0, i))
    nqi = IDX_HEADS * IDX_DIM
    return pl.pallas_call(
        _proj_odd_kv_kernel,
        grid=(bsz, nq),
        in_specs=[pl.BlockSpec((tq, d), lambda b, i: (b * nq + i, 0)), _resident(g.shape),
                  _resident(wt.shape), _resident(kvn.shape),
                  tab(cos32), tab(sin32), tab(cos16), tab(sin16)],
        out_specs=[fm(MLA_LATENT), fm(MLA_ROPE), fm(nqi), fm(IDX_DIM), fm(IDX_HEADS)],
        out_shape=[jax.ShapeDtypeStruct((bsz, MLA_LATENT, seq), BF16),
                   jax.ShapeDtypeStruct((bsz, MLA_ROPE, seq), BF16),
                   jax.ShapeDtypeStruct((bsz, nqi, seq), BF16),
                   jax.ShapeDtypeStruct((bsz, IDX_DIM, seq), BF16),
                   jax.ShapeDtypeStruct((bsz, IDX_HEADS, seq), F32)],
        compiler_params=_cparams(("parallel", "parallel")),
        name="proj_odd_kv",
    )(h, g, wt, kvn, cos32, sin32, cos16, sin16)


def _proj_odd_q_kernel(h_ref, g_ref, wt_ref, cos16_ref, sin16_ref, q_ref):
    xn = _rms_rows(h_ref[...], g_ref[2:3, :]).astype(BF16)
    tq = xn.shape[0]
    cos = cos16_ref[...]
    sin = sin16_ref[...]
    for hh in range(MLA_HEADS):
        y = _dot_nt(wt_ref[hh * MLA_QK:(hh + 1) * MLA_QK, :], xn)
        r = _rope_fmaj(y[MLA_LATENT:].reshape(1, MLA_ROPE, tq), cos, sin)
        q = jnp.concatenate([y[:MLA_LATENT], r.reshape(MLA_ROPE, tq)], axis=0)
        q_ref[:, hh * tq:(hh + 1) * tq] = (q * (MLA_QK ** -0.5 * LOG2E)).astype(q_ref.dtype)


def _proj_odd_q(h, g, wt, cos16, sin16, bsz, seq):
    d = h.shape[1]
    tq = min(TQ_DSA, seq)
    nq = seq // tq
    return pl.pallas_call(
        _proj_odd_q_kernel,
        grid=(bsz, nq),
        in_specs=[pl.BlockSpec((tq, d), lambda b, i: (b * nq + i, 0)), _resident(g.shape),
                  _resident(wt.shape),
                  pl.BlockSpec((cos16.shape[0], tq), lambda b, i: (0, i)),
                  pl.BlockSpec((sin16.shape[0], tq), lambda b, i: (0, i))],
        out_specs=pl.BlockSpec((None, None, MLA_QK, MLA_HEADS * tq), lambda b, i: (b, i, 0, 0)),
        out_shape=jax.ShapeDtypeStruct((bsz, nq, MLA_QK, MLA_HEADS * tq), BF16),
        compiler_params=_cparams(("parallel", "parallel")),
        name="proj_odd_q",
    )(h, g, wt, cos16, sin16)


def _s5_state_kernel(u_ref, bc_ref, s_ref):
    s_ref[...] = _dot(u_ref[...], bc_ref[...])


def _s5_scan_kernel(sr_ref, si_ref, lr_ref, li_ref, xr_ref, xi_ref):
    n_chunk = sr_ref.shape[0]
    lr = lr_ref[...]
    li = li_ref[...]

    def body(c, carry):
        xr, xi = carry
        xr_ref[c] = xr
        xi_ref[c] = xi
        return (lr * xr - li * xi + sr_ref[c], lr * xi + li * xr + si_ref[c])

    zero = jnp.zeros(sr_ref.shape[1:], F32)
    lax.fori_loop(0, n_chunk, body, (zero, zero))


def _s5_out_kernel(u_ref, x_ref, m_ref, cc_ref, y_ref):
    x = x_ref[...]
    hi = x.astype(BF16)
    lo = (x - hi.astype(F32)).astype(BF16)
    cc = cc_ref[...]
    y_ref[...] = _dot(u_ref[...], m_ref[...]) + _dot(hi, cc) + _dot(lo, cc)


def _s5_matrices(a_re, a_im, b_re, b_im, c_re, c_im, log_dt, tc):
    hp = lax.Precision.HIGHEST
    dt = jnp.exp(log_dt)[:, None]
    lam = lax.complex(a_re, a_im)
    lam_dt = lam * dt
    lam_bar = jnp.exp(lam_dt)
    b_bar = ((lam_bar - 1.0) / lam)[..., None] * lax.complex(b_re, b_im)
    k = jnp.arange(tc + 1, dtype=F32)[:, None, None]
    pw = jnp.exp(lam_dt[None] * k)
    pr, pi = jnp.real(pw), jnp.imag(pw)
    bbr, bbi = jnp.real(b_bar), jnp.imag(b_bar)
    cpr = c_re[None] * pr[:, :, None, :] - c_im[None] * pi[:, :, None, :]
    cpi = c_re[None] * pi[:, :, None, :] + c_im[None] * pr[:, :, None, :]
    kk = (jnp.einsum('kgpn,gnq->kgpq', cpr[:tc], bbr, precision=hp)
          - jnp.einsum('kgpn,gnq->kgpq', cpi[:tc], bbi, precision=hp))
    n_g, n_p = a_re.shape[0], b_re.shape[2]
    kq = jnp.pad(kk.transpose(1, 3, 0, 2).astype(BF16), ((0, 0), (0, 0), (tc, 0), (0, 0)))
    lag = tc + jnp.arange(tc)[None, :] - jnp.arange(tc)[:, None]
    m = kq[:, :, lag, :].reshape(n_g, n_p * tc, tc * n_p)
    rev = pw[tc - 1 - jnp.arange(tc)]
    bc = rev[:, :, :, None] * b_bar[None]
    bc = bc.transpose(1, 3, 0, 2).reshape(n_g, n_p * tc, -1)
    bc = jnp.concatenate([jnp.real(bc), jnp.imag(bc)], axis=-1)
    mr = cpr[1:tc + 1].transpose(1, 3, 0, 2).reshape(n_g, -1, tc * n_p)
    mi = cpi[1:tc + 1].transpose(1, 3, 0, 2).reshape(n_g, -1, tc * n_p)
    cc = jnp.concatenate([mr, -mi], axis=1)
    ltc = pw[tc].reshape(1, -1)
    return m, bc.astype(BF16), cc.astype(BF16), jnp.real(ltc), jnp.imag(ltc)


def _s5_scan(u, mats, bsz, seq):
    m, bc, cc, lr, li = mats
    tc = S5_CHUNK
    n_c = seq // tc
    n_g, n_p, n_s = S5_GROUPS, S5_GROUP, S5_STATE
    rows = bsz * n_c
    kd = tc * n_p
    ug = (u.reshape(bsz, n_c, tc, n_g, n_p).transpose(3, 0, 1, 4, 2)
          .reshape(n_g, rows, kd).astype(BF16))
    grp = lambda a, b: pl.BlockSpec((None, a, b), lambda gi: (gi, 0, 0))
    s = pl.pallas_call(
        _s5_state_kernel,
        grid=(n_g,),
        in_specs=[grp(rows, kd), grp(kd, 2 * n_s)],
        out_specs=grp(rows, 2 * n_s),
        out_shape=jax.ShapeDtypeStruct((n_g, rows, 2 * n_s), F32),
        compiler_params=_cparams(("parallel",)),
        name="s5_state",
    )(ug, bc)
    s5 = s.reshape(n_g, bsz, n_c, 2, n_s).transpose(3, 2, 1, 0, 4).reshape(2, n_c, bsz, n_g * n_s)
    full = lambda shp: pl.BlockSpec(shp, lambda: (0,) * len(shp))
    xr, xi = pl.pallas_call(
        _s5_scan_kernel,
        in_specs=[full(s5.shape[1:]), full(s5.shape[1:]), full(lr.shape), full(li.shape)],
        out_specs=[full(s5.shape[1:]), full(s5.shape[1:])],
        out_shape=[jax.ShapeDtypeStruct(s5.shape[1:], F32)] * 2,
        compiler_params=pltpu.CompilerParams(vmem_limit_bytes=VMEM_LIMIT),
        name="s5_scan",
    )(s5[0], s5[1], lr, li)
    x = jnp.stack([xr, xi]).reshape(2, n_c, bsz, n_g, n_s).transpose(3, 2, 1, 0, 4)
    x = x.reshape(n_g, rows, 2 * n_s)
    y = pl.pallas_call(
        _s5_out_kernel,
        grid=(n_g,),
        in_specs=[grp(rows, kd), grp(rows, 2 * n_s), grp(kd, kd), grp(2 * n_s, kd)],
        out_specs=grp(rows, kd),
        out_shape=jax.ShapeDtypeStruct((n_g, rows, kd), F32),
        compiler_params=_cparams(("parallel",)),
        name="s5_out",
    )(ug, x, m, cc)
    return (y.reshape(n_g, bsz, n_c, tc, n_p).transpose(1, 2, 3, 0, 4)
            .reshape(bsz * seq, n_g * n_p))


def _compress_kernel(x_ref, pe_ref, w1_ref, w2_ref, o_ref):
    x = x_ref[...].astype(F32)
    nb = x.shape[0]
    a = _dot((x + pe_ref[0:1, :]).astype(BF16), w1_ref[0])
    b = _dot((x + pe_ref[1:2, :]).astype(BF16), w1_ref[1])
    pre = a + pltpu.roll(b, nb - 1, 0)
    o_ref[...] = _dot(jax.nn.gelu(pre).astype(BF16), w2_ref[...]).astype(o_ref.dtype)


def _compress(x, pe, w1, w2):
    bsz, hk, nb, kd = x.shape
    hd = w2.shape[1]
    return pl.pallas_call(
        _compress_kernel,
        grid=(bsz, hk),
        in_specs=[pl.BlockSpec((None, None, nb, kd), lambda b, h: (b, h, 0, 0)),
                  _resident(pe.shape), _resident(w1.shape), _resident(w2.shape)],
        out_specs=pl.BlockSpec((None, None, nb, hd), lambda b, h: (b, h, 0, 0)),
        out_shape=jax.ShapeDtypeStruct((bsz, hk, nb, hd), BF16),
        compiler_params=_cparams(("parallel", "parallel")),
        name="nsa_compress",
    )(x, pe, w1, w2)


def _nsa_kernel(q_ref, g_ref, kc_ref, vct_ref, ks_ref, vst_ref, kw_ref, vwt_ref, ov_ref,
                o_ref, sel_ref, zero_ref, sa_ref, sb_ref, *, tq, tk, top_n):
    gq, hd = NSA_GQA, NSA_HEAD_DIM
    n_sb = sel_ref.shape[0]
    nb = kc_ref.shape[0]
    wd = gq * tq
    i = pl.program_id(2)
    s0 = i * tq
    q = q_ref[...]
    qt = jnp.concatenate([q[g * hd:(g + 1) * hd, :] for g in range(gq)], axis=1).astype(BF16)
    t1 = s0 + lax.broadcasted_iota(I32, (1, tq), 1)
    t4 = s0 + (lax.broadcasted_iota(I32, (1, wd), 1) & (tq - 1))

    sc = _dot(kc_ref[...], qt)
    c_last = lax.broadcasted_iota(I32, (nb, 1), 0) * CMP_STRIDE + (CMP_LEN - 1)
    bias_c = jnp.where(c_last <= t1, 0.0, NEG)
    sm = sc + jnp.concatenate([bias_c] * gq, axis=1)
    e = jnp.exp2(sm - jnp.max(sm, axis=0, keepdims=True))
    p_c = e * jnp.where(t4 >= CMP_LEN - 1, 1.0 / jnp.sum(e, axis=0, keepdims=True), 0.0)
    o_c = _dot(vct_ref[...], p_c.astype(BF16))
    psum = p_c[:, 0:tq]
    for g in range(1, gq):
        psum = psum + p_c[:, g * tq:(g + 1) * tq]
    hi = psum.astype(BF16)
    lo = (psum - hi.astype(F32)).astype(BF16)
    imp = _dot(ov_ref[...], hi) + _dot(ov_ref[...], lo)
    blk = lax.broadcasted_iota(I32, (n_sb, 1), 0)
    cur = t1 >> int(math.log2(SEL_BLOCK))
    forced = (blk == 0) | (blk == cur) | (blk == cur - 1)
    imp = imp + jnp.where(forced, FORCE_BONUS, 0.0)
    imp = jnp.where(blk * SEL_BLOCK <= t1, imp, NEG)

    rowf = lax.broadcasted_iota(I32, (n_sb, tq), 0).astype(F32)
    sel = jnp.zeros((n_sb, tq), F32)
    x = imp
    for _ in range(top_n):
        mx = jnp.max(x, axis=0, keepdims=True)
        first = jnp.min(jnp.where(x == mx, rowf, float(n_sb)), axis=0, keepdims=True)
        hit = rowf == first
        sel = jnp.where(hit, 1.0, sel)
        x = jnp.where(hit, -jnp.inf, x)
    sel_ref[...] = sel

    zero_ref[...] = jnp.zeros(zero_ref.shape, F32)
    init = (jnp.full((1, wd), NEG, F32), jnp.zeros((1, wd), F32), zero_ref[...])

    bpt = tk // SEL_BLOCK
    key_row = lax.broadcasted_iota(I32, (tk, 1), 0)

    last_tile = ks_ref.shape[0] - 1

    def scores(j):
        return _dot(ks_ref[jnp.minimum(j, last_tile)], qt)

    def consume(j, s_ref, carry):
        m, l, acc, pv = carry
        jc = jnp.minimum(j, last_tile)
        rows = [jnp.broadcast_to(sel_ref[pl.ds(jc * bpt + b, 1), :], (SEL_BLOCK, tq))
                for b in range(bpt)]
        keep = (jnp.concatenate(rows, axis=0) > 0.5) & (j * tk + key_row <= t1)
        bias = jnp.where(keep, 0.0, NEG)
        s = s_ref[...] + jnp.concatenate([bias] * gq, axis=1)
        m_new = jnp.maximum(m, jnp.max(s, axis=0, keepdims=True))
        alpha = jnp.exp2(m - m_new)
        p = jnp.exp2(s - m_new)
        l = alpha * l + jnp.sum(p, axis=0, keepdims=True)
        return m_new, l, alpha * (acc + pv), _dot(vst_ref[jc], p.astype(BF16))

    def tile_pair(k, carry):
        j = 2 * k
        sb_ref[...] = scores(j + 1)
        carry = consume(j, sa_ref, carry)
        sa_ref[...] = scores(j + 2)
        return consume(j + 1, sb_ref, carry)

    sa_ref[...] = scores(0)
    _, l_s, a_s, pv_s = lax.fori_loop(0, (s0 // tk + 2) // 2, tile_pair, init + (zero_ref[...],))
    o_s = (a_s + pv_s) / l_s

    n_wt = (WINDOW + tq) // LANE
    j_lo = jnp.clip(i * (tq // LANE) - WINDOW // LANE, 0, kw_ref.shape[0] - n_wt)
    kwin = jnp.concatenate([kw_ref[j_lo + w] for w in range(n_wt)], axis=0)
    vwin = jnp.concatenate([vwt_ref[j_lo + w] for w in range(n_wt)], axis=1)
    sw = _dot(kwin, qt)
    diff = t1 - (j_lo * LANE + lax.broadcasted_iota(I32, (n_wt * LANE, 1), 0))
    sw = sw + jnp.concatenate([jnp.where((diff >= 0) & (diff < WINDOW), 0.0, NEG)] * gq, axis=1)
    pw = jnp.exp2(sw - jnp.max(sw, axis=0, keepdims=True))
    o_w = _dot(vwin, pw.astype(BF16)) / jnp.sum(pw, axis=0, keepdims=True)

    gt = g_ref[...]
    outs = []
    for g in range(gq):
        cs = slice(g * tq, (g + 1) * tq)
        outs.append(gt[3 * g:3 * g + 1, :] * o_c[:, cs] + gt[3 * g + 1:3 * g + 2, :] * o_s[:, cs]
                    + gt[3 * g + 2:3 * g + 3, :] * o_w[:, cs])
    o_ref[...] = jnp.concatenate(outs, axis=0).T


def _nsa(q, gates, kcmp, vcmp_t, ks, vs_t, kw, vw_t, ov_t, seq):
    bsz = q.shape[0]
    tq = min(TQ_NSA, seq)
    tk = min(TK_NSA, seq)
    n_sb = seq // SEL_BLOCK
    gq, hd = NSA_GQA, NSA_HEAD_DIM
    per_head = lambda a: pl.BlockSpec((None, None) + a.shape[2:],
                                      lambda b, h, i: (b, h) + (0,) * (a.ndim - 2),
                                      pipeline_mode=pl.Buffered(1))
    return pl.pallas_call(
        functools.partial(_nsa_kernel, tq=tq, tk=tk, top_n=min(SEL_TOP, n_sb)),
        grid=(bsz, NSA_KV_HEADS, seq // tq),
        in_specs=[pl.BlockSpec((None, gq * hd, tq), lambda b, h, i: (b, h, i)),
                  pl.BlockSpec((None, None, 3 * gq, tq), lambda b, h, i: (b, h, 0, i)),
                  per_head(kcmp), per_head(vcmp_t), per_head(ks), per_head(vs_t),
                  per_head(kw), per_head(vw_t), _resident(ov_t.shape)],
        out_specs=pl.BlockSpec((tq, gq * hd), lambda b, h, i: (b * (seq // tq) + i, h)),
        out_shape=jax.ShapeDtypeStruct((bsz * seq, NSA_Q), F32),
        scratch_shapes=[pltpu.VMEM((n_sb, tq), F32), pltpu.VMEM((hd, gq * tq), F32),
                        pltpu.VMEM((tk, gq * tq), F32), pltpu.VMEM((tk, gq * tq), F32)],
        compiler_params=_cparams(("parallel", "parallel", "arbitrary")),
        name="nsa",
    )(q, gates, kcmp, vcmp_t, ks, vs_t, kw, vw_t, ov_t)


def _dsa_kernel(h_ref, g_ref, q_ref, qi_ref, wi_ref, kcat_ref, ckt_ref, kidx_ref, wuv_ref, wout_ref,
                o_ref, ihi, ilo, acc_ref, m_ref, l_ref, tie_cap, *, tq, top):
    tk = tq
    n_h = MLA_HEADS
    i16 = jnp.int16
    i = pl.program_id(1)
    row = lax.broadcasted_iota(I32, (tk, 1), 0)
    col = lax.broadcasted_iota(I32, (1, tq), 1)
    causal = row <= col

    def tile(ref, j):
        return ref.at[pl.ds(pl.multiple_of(j * tk, tk), tk), :]

    qi = qi_ref[...]
    wi = wi_ref[...] * (IDX_HEADS ** -0.5) * (IDX_DIM ** -0.5)

    def idx_scores(j):
        kj = kidx_ref[j]
        sc = jnp.zeros((tk, tq), F32)
        for h in range(IDX_HEADS):
            d = _dot(kj, qi[h * IDX_DIM:(h + 1) * IDX_DIM, :])
            sc = sc + wi[h:h + 1, :] * jnp.maximum(d, 0.0)
        return sc

    def store_keys(j, sc):
        b = pltpu.bitcast(sc, I32)
        key = b ^ ((b >> 31) & 0x7FFFFFFF)
        tile(ihi, j)[...] = (key >> 16).astype(i16)
        tile(ilo, j)[...] = ((key & 0xFFFF) - 32768).astype(i16)

    def idx_tile(j, _):
        store_keys(j, idx_scores(j))
        return 0

    lax.fori_loop(0, i, idx_tile, 0)
    store_keys(i, jnp.where(causal, idx_scores(i), NEG))

    @pl.when(((i + 1) * tq) % kcat_ref.shape[1] != 0)
    def _():
        store_keys(i + 1, jnp.full((tk, tq), NEG, F32))

    def count(ref, pred):
        def body(j, c):
            hit = jnp.where(pred(tile(ref, j)[...]), jnp.ones((), i16), jnp.zeros((), i16))
            parts = [hit[k * 16:(k + 1) * 16] for k in range(tk // 16)]
            while len(parts) > 1:
                parts = [a + b for a, b in zip(parts[0::2], parts[1::2])]
            return c + parts[0]
        c16 = lax.fori_loop(0, i + 1, body, jnp.zeros((16, tq), i16))
        return jnp.sum(c16.astype(I32), axis=0, keepdims=True)

    def search(ref, need, n_all):
        def bit_step(b, st):
            thr, c_ge, c_gt = st
            cand = thr + (jnp.int32(1) << (15 - b))
            cand16 = cand.astype(i16)
            cnt = count(ref, lambda t: t >= cand16)
            ok = cnt >= need
            return jnp.where(ok, cand, thr), jnp.where(ok, cnt, c_ge), jnp.where(ok, c_gt, cnt)
        init = (jnp.full((1, tq), -32768, I32), n_all, jnp.zeros((1, tq), I32))
        return lax.fori_loop(0, 16, bit_step, init)

    thr_hi, n_ge_hi, n_gt_hi = search(ihi, top, jnp.full((1, tq), 1, I32) * ((i + 1) * tk))
    thr_hi16 = thr_hi.astype(i16)
    need = top - n_gt_hi
    n_eq_hi = n_ge_hi - n_gt_hi

    def mask_lo(j, _):
        lo = tile(ilo, j)
        lo[...] = jnp.where(tile(ihi, j)[...] == thr_hi16, lo[...], jnp.full((), -32768, i16))
        return 0

    lax.fori_loop(0, i + 1, mask_lo, 0)
    thr_lo, n_ge_lo, _ = search(ilo, need, n_eq_hi)

    n_sel = n_gt_hi + n_ge_lo
    tie_cap[...] = jnp.full((1, tq), ihi.shape[0], I32)
    excess = n_sel - top
    max_excess = jnp.max(excess)
    n_bits = (ihi.shape[0] - 1).bit_length()

    thr_lo16 = thr_lo.astype(i16)

    def tie_diff(j):
        return ((tile(ihi, j)[...] ^ thr_hi16) | (tile(ilo, j)[...] ^ thr_lo16)).astype(I32)

    n_top = 3
    max_scans = 2 * n_bits // n_top

    @pl.when((max_excess > 0) & (max_excess <= n_top * max_scans))
    def _():
        def drop(_, st):
            cap, left = st

            def body(j, tops):
                idx = j * tk + row
                v = jnp.where(tie_diff(j) == 0, jnp.where(idx <= cap, idx, -1), -1)
                v = v.reshape(tk // 8, 8, tq)
                for k in range(tk // 8):
                    t = v[k]
                    new = []
                    for m in tops:
                        new.append(jnp.maximum(m, t))
                        t = jnp.minimum(m, t)
                    tops = tuple(new)
                return tops

            tops = list(lax.fori_loop(0, i + 1, body, (jnp.full((8, tq), -1, I32),) * n_top))
            take = jnp.minimum(left, n_top)
            for r in range(n_top):
                g = jnp.max(functools.reduce(jnp.maximum, tops), axis=0, keepdims=True)
                cap = jnp.where(take == r + 1, g - 1, cap)
                tops = [jnp.where(m == g, -1, m) for m in tops]
            return cap, left - take

        n_scan = (max_excess + n_top - 1) // n_top
        tie_cap[...] = lax.fori_loop(0, n_scan, drop, (tie_cap[...], excess))[0]

    @pl.when(max_excess > n_top * max_scans)
    def _():
        def count_ties(below):
            def body(j, c):
                hit = jnp.where(tie_diff(j) == 0, jnp.where(j * tk + row < below, 1, 0), 0)
                return c + jnp.sum(hit.reshape(tk // 8, 8, tq), axis=0)
            c8 = lax.fori_loop(0, i + 1, body, jnp.zeros((8, tq), I32))
            return jnp.sum(c8, axis=0, keepdims=True)

        need_eq = count_ties(ihi.shape[0]) - excess

        def bit_step(b, cap):
            cand = cap + (jnp.int32(1) << (n_bits - 1 - b))
            return jnp.where(count_ties(cand) >= need_eq, cap, cand)

        tie_cap[...] = lax.fori_loop(0, n_bits, bit_step, jnp.zeros((1, tq), I32))

    m_ref[...] = jnp.full(m_ref.shape, NEG, F32)
    l_ref[...] = jnp.zeros(l_ref.shape, F32)
    acc_ref[...] = jnp.zeros(acc_ref.shape, F32)
    cap = tie_cap[...]
    thr_key = (thr_hi << 16) | (thr_lo + 32768)

    ta = kcat_ref.shape[1]
    a_row = lax.broadcasted_iota(I32, (ta, 1), 0)
    q_pos = i * tq + col

    def att_tile(j, _):
        rows = pl.ds(pl.multiple_of(j * ta, ta), ta)
        hi = ihi[rows, :].astype(I32)
        lo = ilo[rows, :].astype(I32)
        k_pos = j * ta + a_row
        key = (hi << 16) | (lo + 32768)
        sel = jnp.where(key > thr_key, 0.0,
                        jnp.where(key == thr_key, jnp.where(k_pos <= cap, 0.0, NEG), NEG))
        bias = jnp.where(k_pos <= q_pos, sel, NEG)
        s_all = _dot(kcat_ref[j], q_ref[...])
        ps, alphas = [], []
        for h in range(n_h):
            cs = slice(h * tq, (h + 1) * tq)
            s = s_all[:, cs] + bias
            m_old = m_ref[:, cs]
            m_new = jnp.maximum(m_old, jnp.max(s, axis=0, keepdims=True))
            alpha = jnp.exp2(m_old - m_new)
            p = jnp.exp2(s - m_new)
            l_ref[:, cs] = alpha * l_ref[:, cs] + jnp.sum(p, axis=0, keepdims=True)
            m_ref[:, cs] = m_new
            ps.append(p.astype(BF16))
            alphas.append(alpha)
        p_all = jnp.concatenate(ps, axis=1)
        acc_ref[...] = acc_ref[...] * jnp.concatenate(alphas, axis=1) + _dot(ckt_ref[j], p_all)
        return 0

    lax.fori_loop(0, (i * tq) // ta + 1, att_tile, 0)

    outs = []
    for h in range(n_h):
        cs = slice(h * tq, (h + 1) * tq)
        o = acc_ref[:, cs] / l_ref[:, cs]
        outs.append(_dot(wuv_ref[h], o.astype(BF16)))
    mix = jnp.concatenate(outs, axis=0).T
    y = _dot(mix.astype(BF16), wout_ref[...])
    o_ref[...] = h_ref[...] + _rms_rows(y, g_ref[3:4, :])


def _dsa(h, g, q, qi, wi, kcat, ckt, kidx, wuv_t, w_out, seq):
    bsz = q.shape[0]
    d = h.shape[1]
    tq = min(TQ_DSA, seq)
    nq = seq // tq
    top = min(DSA_TOP, seq // 4)
    wd = MLA_HEADS * tq
    per_b = lambda a: pl.BlockSpec((None,) + a.shape[1:], lambda b, i: (b,) + (0,) * (a.ndim - 1),
                                   pipeline_mode=pl.Buffered(1))
    return pl.pallas_call(
        functools.partial(_dsa_kernel, tq=tq, top=top),
        grid=(bsz, nq),
        in_specs=[pl.BlockSpec((tq, d), lambda b, i: (b * nq + i, 0)), _resident(g.shape),
                  pl.BlockSpec((None, None, MLA_QK, wd), lambda b, i: (b, i, 0, 0)),
                  pl.BlockSpec((None, IDX_HEADS * IDX_DIM, tq), lambda b, i: (b, 0, i)),
                  pl.BlockSpec((None, IDX_HEADS, tq), lambda b, i: (b, 0, i)),
                  per_b(kcat), per_b(ckt), per_b(kidx), _resident(wuv_t.shape),
                  _resident(w_out.shape)],
        out_specs=pl.BlockSpec((tq, d), lambda b, i: (b * nq + i, 0)),
        out_shape=jax.ShapeDtypeStruct(h.shape, F32),
        scratch_shapes=[pltpu.VMEM((seq, tq), jnp.int16),
                        pltpu.VMEM((seq, tq), jnp.int16),
                        pltpu.VMEM((MLA_LATENT, wd), F32),
                        pltpu.VMEM((1, wd), F32),
                        pltpu.VMEM((1, wd), F32),
                        pltpu.VMEM((1, tq), I32)],
        compiler_params=_cparams(("parallel", "arbitrary")),
        name="dsa",
    )(h, g, q, qi, wi, kcat, ckt, kidx, wuv_t, w_out)


def _rope_tables(seq, dim):
    half = dim // 2
    inv = ROPE_THETA ** (-jnp.arange(half, dtype=F32) / half)
    ang = inv[:, None] * jnp.arange(seq, dtype=F32)[None, :]
    return jnp.cos(ang), jnp.sin(ang)


def _even_mixer(h, g, bsz, seq, w_in, w_out, s5, pe_k, pe_v, wk1, wk2, wv1, wv2, tabs):
    cos32, sin32 = tabs[64]
    hd, hk = NSA_HEAD_DIM, NSA_KV_HEADS
    o = 0
    cols = {}
    for name, size in (("u", S5_WIDTH), ("q", NSA_Q), ("kc", NSA_KV), ("vc", NSA_KV), ("ks", NSA_KV),
                       ("vs", NSA_KV), ("kw", NSA_KV), ("vw", NSA_KV), ("gt", 3 * NSA_HEADS)):
        cols[name] = w_in[:, o:o + size]
        o += size
    wu = cols["u"].astype(BF16)
    wt = jnp.concatenate([cols[n] for n in ("q", "kc", "ks", "kw", "vc", "vs", "vw", "gt")], axis=1).T
    wt = jnp.pad(wt, ((0, (-wt.shape[0]) % 16), (0, 0))).astype(BF16)
    u, q_t, k_t, v_t, gt_t = _proj_even(h, g, wu, wt, cos32, sin32, bsz, seq)

    a_re, a_im, b_re, b_im, c_re, c_im, log_dt, d_skip, w_glu = s5
    mats = _s5_matrices(a_re, a_im, b_re, b_im, c_re, c_im, log_dt, S5_CHUNK)
    ys = _s5_scan(u, mats, bsz, seq)

    k5 = k_t.reshape(bsz, 3, hk, hd, seq)
    v5 = v_t.reshape(bsz, 3, hk, hd, seq)
    nb = seq // CMP_STRIDE
    half_blk = lambda a: (a.reshape(bsz, hk, hd, nb, CMP_STRIDE).transpose(0, 1, 3, 4, 2)
                          .reshape(bsz, hk, nb, CMP_STRIDE * hd))
    pe2 = lambda pe: pe.reshape(2, CMP_STRIDE * hd)
    w1s = lambda w: w.reshape(2, CMP_STRIDE * hd, hd).astype(BF16)
    kcmp = _compress(half_blk(k5[:, 0]), pe2(pe_k), w1s(wk1), wk2.astype(BF16))
    vcmp = _compress(half_blk(v5[:, 0]), pe2(pe_v), w1s(wv1), wv2.astype(BF16))
    vcmp_t = vcmp.transpose(0, 1, 3, 2)
    tk = min(TK_NSA, seq)
    ks = k5[:, 1].reshape(bsz, hk, hd, seq // tk, tk).transpose(0, 1, 3, 4, 2)
    vs_t = v5[:, 1].reshape(bsz, hk, hd, seq // tk, tk).transpose(0, 1, 3, 2, 4)
    kw = k5[:, 2].reshape(bsz, hk, hd, seq // LANE, LANE).transpose(0, 1, 3, 4, 2)
    vw_t = v5[:, 2].reshape(bsz, hk, hd, seq // LANE, LANE).transpose(0, 1, 3, 2, 4)
    n_sb = seq // SEL_BLOCK
    c_start = jnp.arange(nb) * CMP_STRIDE
    b_start = jnp.arange(n_sb) * SEL_BLOCK
    ov_t = ((c_start[None, :] < b_start[:, None] + SEL_BLOCK)
            & (c_start[None, :] + CMP_LEN > b_start[:, None])
            & (jnp.arange(nb)[None, :] < nb - 1)).astype(BF16)
    gates = gt_t.reshape(bsz, hk, 3 * NSA_GQA, seq)
    b_out = _nsa(q_t, gates, kcmp, vcmp_t, ks, vs_t, kw, vw_t, ov_t, seq)
    return _outproj_even(h, ys, u, b_out, g, d_skip.reshape(1, -1), w_glu.astype(BF16),
                         w_out[:S5_WIDTH].astype(BF16), w_out[S5_WIDTH:].astype(BF16))


def _odd_mixer(h, g, bsz, seq, w_in, kv_norm, w_uv, w_out, tabs):
    cos32, sin32 = tabs[64]
    cos16, sin16 = tabs[32]
    d = h.shape[1]
    sizes = (MLA_HEADS * MLA_LATENT, MLA_HEADS * MLA_ROPE, MLA_LATENT, MLA_ROPE,
             IDX_HEADS * IDX_DIM, IDX_DIM, IDX_HEADS)
    parts = []
    o = 0
    for s in sizes:
        parts.append(w_in[:, o:o + s])
        o += s
    w_ql, w_qr, w_c, w_kr, w_qi, w_ki, w_wi = parts
    wq = jnp.concatenate([w_ql.reshape(d, MLA_HEADS, MLA_LATENT), w_qr.reshape(d, MLA_HEADS, MLA_ROPE)],
                         axis=2).reshape(d, MLA_HEADS * MLA_QK).T.astype(BF16)
    wkv = jnp.concatenate([w_c, w_kr, w_qi, w_ki, w_wi], axis=1).T
    wkv = jnp.pad(wkv, ((0, (-wkv.shape[0]) % 16), (0, 0))).astype(BF16)
    ckv_t, kr_t, qi_t, ki_t, wi_t = _proj_odd_kv(h, g, wkv, kv_norm.reshape(-1, 1), cos32, sin32,
                                                 cos16, sin16, bsz, seq)
    q = _proj_odd_q(h, g, wq, cos16, sin16, bsz, seq)
    tk = min(TQ_DSA, seq)
    ta = min(TA_DSA, seq)
    kcat = jnp.concatenate([ckv_t, kr_t], axis=1).transpose(0, 2, 1).reshape(bsz, seq // ta, ta, MLA_QK)
    ckt = ckv_t.reshape(bsz, MLA_LATENT, seq // ta, ta).transpose(0, 2, 1, 3)
    kidx = ki_t.transpose(0, 2, 1).reshape(bsz, seq // tk, tk, IDX_DIM)
    wuv_t = w_uv.transpose(0, 2, 1).astype(BF16)
    return _dsa(h, g, q, qi_t, wi_t, kcat, ckt, kidx, wuv_t, w_out.astype(BF16), seq)


def kernel(x, p, norm_g, ffn1_w_in, ffn1_w_out, ffn2_w_in, ffn2_w_out, ple_w_gate, ple_w_proj,
           ev_w_in, ev_w_out, s5_a_re, s5_a_im, s5_b_re, s5_b_im, s5_c_re, s5_c_im, s5_log_dt, s5_d,
           s5_w_glu, nsa_pe_k, nsa_pe_v, nsa_wk1, nsa_wk2, nsa_wv1, nsa_wv2,
           od_w_in, od_kv_norm, od_w_uv, od_w_out):
    bsz, seq, d = x.shape
    depth = norm_g.shape[0]
    tabs = {64: _rope_tables(seq, 64), 32: _rope_tables(seq, 32)}
    h = x.reshape(bsz * seq, d)
    for i in range(depth):
        g = norm_g[i]
        j = i // 2
        h = _ffn(h, g, ffn1_w_in[i].astype(BF16), ffn1_w_out[i].astype(BF16), 0)
        if i % 2 == 0:
            s5 = (s5_a_re[j], s5_a_im[j], s5_b_re[j], s5_b_im[j], s5_c_re[j], s5_c_im[j],
                  s5_log_dt[j], s5_d[j], s5_w_glu[j])
            h = _even_mixer(h, g, bsz, seq, ev_w_in[j], ev_w_out[j], s5, nsa_pe_k[j], nsa_pe_v[j],
                            nsa_wk1[j], nsa_wk2[j], nsa_wv1[j], nsa_wv2[j], tabs)
        else:
            h = _odd_mixer(h, g, bsz, seq, od_w_in[j], od_kv_norm[j], od_w_uv[j], od_w_out[j], tabs)
        h = _ffn_ple(h, p[i].reshape(bsz * seq, -1), g, ffn2_w_in[i].astype(BF16),
                     ffn2_w_out[i].astype(BF16), ple_w_gate[i].astype(BF16),
                     ple_w_proj[i].astype(BF16))
    return h.reshape(bsz, seq, d)
```

```python
import functools
import math

import jax
import jax.numpy as jnp
from jax import lax
from jax.experimental import pallas as pl
from jax.experimental.pallas import tpu as pltpu

F32 = jnp.float32
BF16 = jnp.bfloat16
I32 = jnp.int32

ROPE_THETA = 10000.0
EPS = 1e-6
NEG = -1e30
LOG2E = math.log2(math.e)
D_FF = 2816
S5_WIDTH = 512
S5_GROUP = 16
S5_GROUPS = S5_WIDTH // S5_GROUP
S5_STATE = 64
NSA_HEADS = 8
NSA_KV_HEADS = 2
NSA_GQA = NSA_HEADS // NSA_KV_HEADS
NSA_HEAD_DIM = 64
CMP_LEN = 32
CMP_STRIDE = 16
SEL_BLOCK = 64
SEL_TOP = 16
WINDOW = 512
FORCE_BONUS = 1000.0
NSA_Q = NSA_HEADS * NSA_HEAD_DIM
NSA_KV = NSA_KV_HEADS * NSA_HEAD_DIM
MLA_HEADS = 16
MLA_LATENT = 256
MLA_ROPE = 32
MLA_QK = MLA_LATENT + MLA_ROPE
MLA_V_DIM = 64
IDX_HEADS = 8
IDX_DIM = 64
DSA_TOP = 256

V7X_VMEM_BYTES = 64 * 2**20
VMEM_LIMIT = V7X_VMEM_BYTES - 8 * 2**20
LANE = 128
TM_FFN = 1024
TQ_PROJ = 1024
TQ_NSA = 256
TK_NSA = 512
TQ_DSA = 256
TA_DSA = 512
S5_CHUNK = 32


def _cparams(sem):
    return pltpu.CompilerParams(dimension_semantics=sem, vmem_limit_bytes=VMEM_LIMIT)


def _resident(shape):
    nd = len(shape)
    return pl.BlockSpec(shape, lambda *_: (0,) * nd, pipeline_mode=pl.Buffered(1))


def _dot(a, b):
    return jnp.dot(a, b, preferred_element_type=F32)


def _dot_nt(a, b):
    return lax.dot_general(a, b, (((1,), (1,)), ((), ())), preferred_element_type=F32)


def _rms_rows(x, g):
    return x * lax.rsqrt(jnp.mean(x * x, axis=-1, keepdims=True) + EPS) * g


def _rope_fmaj(y, cos, sin):
    half = y.shape[1] // 2
    t1 = y[:, :half, :]
    t2 = y[:, half:, :]
    return jnp.concatenate([t1 * cos - t2 * sin, t2 * cos + t1 * sin], axis=1)


def _ffn_tile(x, g_ref, g0, win_ref, wout_ref, n_chunk):
    xn = _rms_rows(x, g_ref[g0:g0 + 1, :]).astype(BF16)
    ck = D_FF // n_chunk
    acc = None
    for c in range(n_chunk):
        a = _dot(xn, win_ref[:, c * ck:(c + 1) * ck])
        u = _dot(xn, win_ref[:, D_FF + c * ck:D_FF + (c + 1) * ck])
        act = (jax.nn.silu(a) * u).astype(BF16)
        y = _dot(act, wout_ref[c * ck:(c + 1) * ck, :])
        acc = y if acc is None else acc + y
    return x + 0.5 * _rms_rows(acc, g_ref[g0 + 1:g0 + 2, :])


def _ffn_kernel(h_ref, g_ref, win_ref, wout_ref, o_ref, *, g0, n_chunk):
    o_ref[...] = _ffn_tile(h_ref[...], g_ref, g0, win_ref, wout_ref, n_chunk)


def _ffn_ple_kernel(h_ref, p_ref, g_ref, win_ref, wout_ref, wg_ref, wp_ref, o_ref, *, n_chunk):
    x = _ffn_tile(h_ref[...], g_ref, 4, win_ref, wout_ref, n_chunk)
    gate = jax.nn.sigmoid(_dot(_rms_rows(x, g_ref[6:7, :]).astype(BF16), wg_ref[...]))
    e = _dot(p_ref[...].astype(BF16), wp_ref[...]) * gate
    o_ref[...] = x + _rms_rows(e, g_ref[7:8, :])


def _ffn_ple(h, p, g, w_in, w_out, w_gate, w_proj):
    t, d = h.shape
    tm = min(TM_FFN, t)
    return pl.pallas_call(
        functools.partial(_ffn_ple_kernel, n_chunk=11),
        grid=(t // tm,),
        in_specs=[pl.BlockSpec((tm, d), lambda i: (i, 0)),
                  pl.BlockSpec((tm, p.shape[1]), lambda i: (i, 0)), _resident(g.shape),
                  _resident(w_in.shape), _resident(w_out.shape), _resident(w_gate.shape),
                  _resident(w_proj.shape)],
        out_specs=pl.BlockSpec((tm, d), lambda i: (i, 0)),
        out_shape=jax.ShapeDtypeStruct((t, d), F32),
        compiler_params=_cparams(("parallel",)),
        name="ffn_ple",
    )(h, p, g, w_in, w_out, w_gate, w_proj)


def _ffn(h, g, w_in, w_out, g0):
    t, d = h.shape
    tm = min(TM_FFN, t)
    return pl.pallas_call(
        functools.partial(_ffn_kernel, g0=g0, n_chunk=11),
        grid=(t // tm,),
        in_specs=[pl.BlockSpec((tm, d), lambda i: (i, 0)), _resident(g.shape),
                  _resident(w_in.shape), _resident(w_out.shape)],
        out_specs=pl.BlockSpec((tm, d), lambda i: (i, 0)),
        out_shape=jax.ShapeDtypeStruct((t, d), F32),
        compiler_params=_cparams(("parallel",)),
        name="ffn",
    )(h, g, w_in, w_out)


def _outproj_even_kernel(h_ref, ys_ref, u_ref, b_ref, g_ref, d_ref, wglu_ref, wa_ref, wb_ref, o_ref):
    y = ys_ref[...] + d_ref[...] * u_ref[...]
    z = jax.nn.gelu(y)
    a = z * jax.nn.sigmoid(_dot(z.astype(BF16), wglu_ref[...]))
    mix = _dot(a.astype(BF16), wa_ref[...]) + _dot(b_ref[...].astype(BF16), wb_ref[...])
    o_ref[...] = h_ref[...] + _rms_rows(mix, g_ref[3:4, :])


def _outproj_even(h, ys, u, b_out, g, d_skip, w_glu, w_a, w_b):
    t, d = h.shape
    tm = min(TM_FFN, t)
    tok = lambda w: pl.BlockSpec((tm, w), lambda i: (i, 0))
    return pl.pallas_call(
        _outproj_even_kernel,
        grid=(t // tm,),
        in_specs=[tok(d), tok(S5_WIDTH), tok(S5_WIDTH), tok(NSA_Q), _resident(g.shape),
                  _resident(d_skip.shape), _resident(w_glu.shape), _resident(w_a.shape),
                  _resident(w_b.shape)],
        out_specs=tok(d),
        out_shape=jax.ShapeDtypeStruct((t, d), F32),
        compiler_params=_cparams(("parallel",)),
        name="outproj_even",
    )(h, ys, u, b_out, g, d_skip, w_glu, w_a, w_b)


def _proj_even_kernel(h_ref, g_ref, wu_ref, wt_ref, cos_ref, sin_ref,
                      u_ref, q_ref, k_ref, v_ref, gt_ref):
    xn = _rms_rows(h_ref[...], g_ref[2:3, :]).astype(BF16)
    u_ref[...] = _dot(xn, wu_ref[...])
    y = _dot_nt(wt_ref[...], xn)
    tq = y.shape[1]
    cos = cos_ref[...]
    sin = sin_ref[...]
    hd = NSA_HEAD_DIM
    q = _rope_fmaj(y[0:NSA_Q].reshape(NSA_HEADS, hd, tq), cos, sin)
    q_ref[...] = (q * (hd ** -0.5 * LOG2E)).reshape(NSA_Q, tq)
    k0 = NSA_Q
    nk = 3 * NSA_KV
    k = _rope_fmaj(y[k0:k0 + nk].reshape(3 * NSA_KV_HEADS, hd, tq), cos, sin)
    k_ref[...] = k.reshape(nk, tq).astype(k_ref.dtype)
    v0 = k0 + nk
    v_ref[...] = y[v0:v0 + nk].astype(v_ref.dtype)
    g0 = v0 + nk
    gt_ref[...] = jax.nn.sigmoid(y[g0:g0 + 3 * NSA_HEADS])


def _proj_even(h, g, wu, wt, cos, sin, bsz, seq):
    d = h.shape[1]
    tq = min(TQ_PROJ, seq)
    nq = seq // tq
    nk = 3 * NSA_KV
    fm = lambda rows: pl.BlockSpec((None, rows, tq), lambda b, i: (b, 0, i))
    return pl.pallas_call(
        _proj_even_kernel,
        grid=(bsz, nq),
        in_specs=[pl.BlockSpec((tq, d), lambda b, i: (b * nq + i, 0)), _resident(g.shape),
                  _resident(wu.shape), _resident(wt.shape),
                  pl.BlockSpec((cos.shape[0], tq), lambda b, i: (0, i)),
                  pl.BlockSpec((sin.shape[0], tq), lambda b, i: (0, i))],
        out_specs=[pl.BlockSpec((tq, S5_WIDTH), lambda b, i: (b * nq + i, 0)),
                   fm(NSA_Q), fm(nk), fm(nk), fm(3 * NSA_HEADS)],
        out_shape=[jax.ShapeDtypeStruct((bsz * seq, S5_WIDTH), F32),
                   jax.ShapeDtypeStruct((bsz, NSA_Q, seq), F32),
                   jax.ShapeDtypeStruct((bsz, nk, seq), BF16),
                   jax.ShapeDtypeStruct((bsz, nk, seq), BF16),
                   jax.ShapeDtypeStruct((bsz, 3 * NSA_HEADS, seq), F32)],
        compiler_params=_cparams(("parallel", "parallel")),
        name="proj_even",
    )(h, g, wu, wt, cos, sin)


def _proj_odd_kv_kernel(h_ref, g_ref, wt_ref, kvn_ref, cos32_ref, sin32_ref, cos16_ref, sin16_ref,
                        ckv_ref, kr_ref, qi_ref, ki_ref, wi_ref):
    xn = _rms_rows(h_ref[...], g_ref[2:3, :]).astype(BF16)
    y = _dot_nt(wt_ref[...], xn)
    tq = y.shape[1]
    c = y[0:MLA_LATENT]
    c = c * lax.rsqrt(jnp.mean(c * c, axis=0, keepdims=True) + EPS) * kvn_ref[...]
    ckv_ref[...] = c.astype(ckv_ref.dtype)
    r0 = MLA_LATENT
    kr = _rope_fmaj(y[r0:r0 + MLA_ROPE].reshape(1, MLA_ROPE, tq), cos16_ref[...], sin16_ref[...])
    kr_ref[...] = kr.reshape(MLA_ROPE, tq).astype(kr_ref.dtype)
    q0 = r0 + MLA_ROPE
    nqi = IDX_HEADS * IDX_DIM
    qi = _rope_fmaj(y[q0:q0 + nqi].reshape(IDX_HEADS, IDX_DIM, tq), cos32_ref[...], sin32_ref[...])
    qi_ref[...] = qi.reshape(nqi, tq).astype(qi_ref.dtype)
    k0 = q0 + nqi
    ki = _rope_fmaj(y[k0:k0 + IDX_DIM].reshape(1, IDX_DIM, tq), cos32_ref[...], sin32_ref[...])
    ki_ref[...] = ki.reshape(IDX_DIM, tq).astype(ki_ref.dtype)
    w0 = k0 + IDX_DIM
    wi_ref[...] = y[w0:w0 + IDX_HEADS]


def _proj_odd_kv(h, g, wt, kvn, cos32, sin32, cos16, sin16, bsz, seq):
    d = h.shape[1]
    tq = min(TQ_PROJ, seq)
    nq = seq // tq
    fm = lambda rows: pl.BlockSpec((None, rows, tq), lambda b, i: (b, 0, i))
    tab = lambda a: pl.BlockSpec((a.shape[0], tq), lambda b, i: (0, i))
    nqi = IDX_HEADS * IDX_DIM
    return pl.pallas_call(
        _proj_odd_kv_kernel,
        grid=(bsz, nq),
        in_specs=[pl.BlockSpec((tq, d), lambda b, i: (b * nq + i, 0)), _resident(g.shape),
                  _resident(wt.shape), _resident(kvn.shape),
                  tab(cos32), tab(sin32), tab(cos16), tab(sin16)],
        out_specs=[fm(MLA_LATENT), fm(MLA_ROPE), fm(nqi), fm(IDX_DIM), fm(IDX_HEADS)],
        out_shape=[jax.ShapeDtypeStruct((bsz, MLA_LATENT, seq), BF16),
                   jax.ShapeDtypeStruct((bsz, MLA_ROPE, seq), BF16),
                   jax.ShapeDtypeStruct((bsz, nqi, seq), BF16),
                   jax.ShapeDtypeStruct((bsz, IDX_DIM, seq), BF16),
                   jax.ShapeDtypeStruct((bsz, IDX_HEADS, seq), F32)],
        compiler_params=_cparams(("parallel", "parallel")),
        name="proj_odd_kv",
    )(h, g, wt, kvn, cos32, sin32, cos16, sin16)


def _proj_odd_q_kernel(h_ref, g_ref, wt_ref, cos16_ref, sin16_ref, q_ref):
    xn = _rms_rows(h_ref[...], g_ref[2:3, :]).astype(BF16)
    tq = xn.shape[0]
    cos = cos16_ref[...]
    sin = sin16_ref[...]
    for hh in range(MLA_HEADS):
        y = _dot_nt(wt_ref[hh * MLA_QK:(hh + 1) * MLA_QK, :], xn)
        r = _rope_fmaj(y[MLA_LATENT:].reshape(1, MLA_ROPE, tq), cos, sin)
        q = jnp.concatenate([y[:MLA_LATENT], r.reshape(MLA_ROPE, tq)], axis=0)
        q_ref[:, hh * tq:(hh + 1) * tq] = (q * (MLA_QK ** -0.5 * LOG2E)).astype(q_ref.dtype)


def _proj_odd_q(h, g, wt, cos16, sin16, bsz, seq):
    d = h.shape[1]
    tq = min(TQ_DSA, seq)
    nq = seq // tq
    return pl.pallas_call(
        _proj_odd_q_kernel,
        grid=(bsz, nq),
        in_specs=[pl.BlockSpec((tq, d), lambda b, i: (b * nq + i, 0)), _resident(g.shape),
                  _resident(wt.shape),
                  pl.BlockSpec((cos16.shape[0], tq), lambda b, i: (0, i)),
                  pl.BlockSpec((sin16.shape[0], tq), lambda b, i: (0, i))],
        out_specs=pl.BlockSpec((None, None, MLA_QK, MLA_HEADS * tq), lambda b, i: (b, i, 0, 0)),
        out_shape=jax.ShapeDtypeStruct((bsz, nq, MLA_QK, MLA_HEADS * tq), BF16),
        compiler_params=_cparams(("parallel", "parallel")),
        name="proj_odd_q",
    )(h, g, wt, cos16, sin16)


def _s5_state_kernel(u_ref, bc_ref, s_ref):
    s_ref[...] = _dot(u_ref[...], bc_ref[...])


def _s5_scan_kernel(sr_ref, si_ref, lr_ref, li_ref, xr_ref, xi_ref):
    n_chunk = sr_ref.shape[0]
    lr = lr_ref[...]
    li = li_ref[...]

    def body(c, carry):
        xr, xi = carry
        xr_ref[c] = xr
        xi_ref[c] = xi
        return (lr * xr - li * xi + sr_ref[c], lr * xi + li * xr + si_ref[c])

    zero = jnp.zeros(sr_ref.shape[1:], F32)
    lax.fori_loop(0, n_chunk, body, (zero, zero))


def _s5_out_kernel(u_ref, x_ref, m_ref, cc_ref, y_ref):
    x = x_ref[...]
    hi = x.astype(BF16)
    lo = (x - hi.astype(F32)).astype(BF16)
    cc = cc_ref[...]
    y_ref[...] = _dot(u_ref[...], m_ref[...]) + _dot(hi, cc) + _dot(lo, cc)


def _s5_matrices(a_re, a_im, b_re, b_im, c_re, c_im, log_dt, tc):
    hp = lax.Precision.HIGHEST
    dt = jnp.exp(log_dt)[:, None]
    lam = lax.complex(a_re, a_im)
    lam_dt = lam * dt
    lam_bar = jnp.exp(lam_dt)
    b_bar = ((lam_bar - 1.0) / lam)[..., None] * lax.complex(b_re, b_im)
    k = jnp.arange(tc + 1, dtype=F32)[:, None, None]
    pw = jnp.exp(lam_dt[None] * k)
    pr, pi = jnp.real(pw), jnp.imag(pw)
    bbr, bbi = jnp.real(b_bar), jnp.imag(b_bar)
    cpr = c_re[None] * pr[:, :, None, :] - c_im[None] * pi[:, :, None, :]
    cpi = c_re[None] * pi[:, :, None, :] + c_im[None] * pr[:, :, None, :]
    kk = (jnp.einsum('kgpn,gnq->kgpq', cpr[:tc], bbr, precision=hp)
          - jnp.einsum('kgpn,gnq->kgpq', cpi[:tc], bbi, precision=hp))
    n_g, n_p = a_re.shape[0], b_re.shape[2]
    kq = jnp.pad(kk.transpose(1, 3, 0, 2).astype(BF16), ((0, 0), (0, 0), (tc, 0), (0, 0)))
    lag = tc + jnp.arange(tc)[None, :] - jnp.arange(tc)[:, None]
    m = kq[:, :, lag, :].reshape(n_g, n_p * tc, tc * n_p)
    rev = pw[tc - 1 - jnp.arange(tc)]
    bc = rev[:, :, :, None] * b_bar[None]
    bc = bc.transpose(1, 3, 0, 2).reshape(n_g, n_p * tc, -1)
    bc = jnp.concatenate([jnp.real(bc), jnp.imag(bc)], axis=-1)
    mr = cpr[1:tc + 1].transpose(1, 3, 0, 2).reshape(n_g, -1, tc * n_p)
    mi = cpi[1:tc + 1].transpose(1, 3, 0, 2).reshape(n_g, -1, tc * n_p)
    cc = jnp.concatenate([mr, -mi], axis=1)
    ltc = pw[tc].reshape(1, -1)
    return m, bc.astype(BF16), cc.astype(BF16), jnp.real(ltc), jnp.imag(ltc)


def _s5_scan(u, mats, bsz, seq):
    m, bc, cc, lr, li = mats
    tc = S5_CHUNK
    n_c = seq // tc
    n_g, n_p, n_s = S5_GROUPS, S5_GROUP, S5_STATE
    rows = bsz * n_c
    kd = tc * n_p
    ug = (u.reshape(bsz, n_c, tc, n_g, n_p).transpose(3, 0, 1, 4, 2)
          .reshape(n_g, rows, kd).astype(BF16))
    grp = lambda a, b: pl.BlockSpec((None, a, b), lambda gi: (gi, 0, 0))
    s = pl.pallas_call(
        _s5_state_kernel,
        grid=(n_g,),
        in_specs=[grp(rows, kd), grp(kd, 2 * n_s)],
        out_specs=grp(rows, 2 * n_s),
        out_shape=jax.ShapeDtypeStruct((n_g, rows, 2 * n_s), F32),
        compiler_params=_cparams(("parallel",)),
        name="s5_state",
    )(ug, bc)
    s5 = s.reshape(n_g, bsz, n_c, 2, n_s).transpose(3, 2, 1, 0, 4).reshape(2, n_c, bsz, n_g * n_s)
    full = lambda shp: pl.BlockSpec(shp, lambda: (0,) * len(shp))
    xr, xi = pl.pallas_call(
        _s5_scan_kernel,
        in_specs=[full(s5.shape[1:]), full(s5.shape[1:]), full(lr.shape), full(li.shape)],
        out_specs=[full(s5.shape[1:]), full(s5.shape[1:])],
        out_shape=[jax.ShapeDtypeStruct(s5.shape[1:], F32)] * 2,
        compiler_params=pltpu.CompilerParams(vmem_limit_bytes=VMEM_LIMIT),
        name="s5_scan",
    )(s5[0], s5[1], lr, li)
    x = jnp.stack([xr, xi]).reshape(2, n_c, bsz, n_g, n_s).transpose(3, 2, 1, 0, 4)
    x = x.reshape(n_g, rows, 2 * n_s)
    y = pl.pallas_call(
        _s5_out_kernel,
        grid=(n_g,),
        in_specs=[grp(rows, kd), grp(rows, 2 * n_s), grp(kd, kd), grp(2 * n_s, kd)],
        out_specs=grp(rows, kd),
        out_shape=jax.ShapeDtypeStruct((n_g, rows, kd), F32),
        compiler_params=_cparams(("parallel",)),
        name="s5_out",
    )(ug, x, m, cc)
    return (y.reshape(n_g, bsz, n_c, tc, n_p).transpose(1, 2, 3, 0, 4)
            .reshape(bsz * seq, n_g * n_p))


def _compress_kernel(x_ref, pe_ref, w1_ref, w2_ref, o_ref):
    x = x_ref[...].astype(F32)
    nb = x.shape[0]
    a = _dot((x + pe_ref[0:1, :]).astype(BF16), w1_ref[0])
    b = _dot((x + pe_ref[1:2, :]).astype(BF16), w1_ref[1])
    pre = a + pltpu.roll(b, nb - 1, 0)
    o_ref[...] = _dot(jax.nn.gelu(pre).astype(BF16), w2_ref[...]).astype(o_ref.dtype)


def _compress(x, pe, w1, w2):
    bsz, hk, nb, kd = x.shape
    hd = w2.shape[1]
    return pl.pallas_call(
        _compress_kernel,
        grid=(bsz, hk),
        in_specs=[pl.BlockSpec((None, None, nb, kd), lambda b, h: (b, h, 0, 0)),
                  _resident(pe.shape), _resident(w1.shape), _resident(w2.shape)],
        out_specs=pl.BlockSpec((None, None, nb, hd), lambda b, h: (b, h, 0, 0)),
        out_shape=jax.ShapeDtypeStruct((bsz, hk, nb, hd), BF16),
        compiler_params=_cparams(("parallel", "parallel")),
        name="nsa_compress",
    )(x, pe, w1, w2)


def _nsa_kernel(q_ref, g_ref, kc_ref, vct_ref, ks_ref, vst_ref, kw_ref, vwt_ref, ov_ref,
                o_ref, sel_ref, zero_ref, sa_ref, sb_ref, *, tq, tk, top_n):
    gq, hd = NSA_GQA, NSA_HEAD_DIM
    n_sb = sel_ref.shape[0]
    nb = kc_ref.shape[0]
    wd = gq * tq
    i = pl.program_id(2)
    s0 = i * tq
    q = q_ref[...]
    qt = jnp.concatenate([q[g * hd:(g + 1) * hd, :] for g in range(gq)], axis=1).astype(BF16)
    t1 = s0 + lax.broadcasted_iota(I32, (1, tq), 1)
    t4 = s0 + (lax.broadcasted_iota(I32, (1, wd), 1) & (tq - 1))

    sc = _dot(kc_ref[...], qt)
    c_last = lax.broadcasted_iota(I32, (nb, 1), 0) * CMP_STRIDE + (CMP_LEN - 1)
    bias_c = jnp.where(c_last <= t1, 0.0, NEG)
    sm = sc + jnp.concatenate([bias_c] * gq, axis=1)
    e = jnp.exp2(sm - jnp.max(sm, axis=0, keepdims=True))
    p_c = e * jnp.where(t4 >= CMP_LEN - 1, 1.0 / jnp.sum(e, axis=0, keepdims=True), 0.0)
    o_c = _dot(vct_ref[...], p_c.astype(BF16))
    psum = p_c[:, 0:tq]
    for g in range(1, gq):
        psum = psum + p_c[:, g * tq:(g + 1) * tq]
    hi = psum.astype(BF16)
    lo = (psum - hi.astype(F32)).astype(BF16)
    imp = _dot(ov_ref[...], hi) + _dot(ov_ref[...], lo)
    blk = lax.broadcasted_iota(I32, (n_sb, 1), 0)
    cur = t1 >> int(math.log2(SEL_BLOCK))
    forced = (blk == 0) | (blk == cur) | (blk == cur - 1)
    imp = imp + jnp.where(forced, FORCE_BONUS, 0.0)
    imp = jnp.where(blk * SEL_BLOCK <= t1, imp, NEG)

    rowf = lax.broadcasted_iota(I32, (n_sb, tq), 0).astype(F32)
    sel = jnp.zeros((n_sb, tq), F32)
    x = imp
    for _ in range(top_n):
        mx = jnp.max(x, axis=0, keepdims=True)
        first = jnp.min(jnp.where(x == mx, rowf, float(n_sb)), axis=0, keepdims=True)
        hit = rowf == first
        sel = jnp.where(hit, 1.0, sel)
        x = jnp.where(hit, -jnp.inf, x)
    sel_ref[...] = sel

    zero_ref[...] = jnp.zeros(zero_ref.shape, F32)
    init = (jnp.full((1, wd), NEG, F32), jnp.zeros((1, wd), F32), zero_ref[...])

    bpt = tk // SEL_BLOCK
    key_row = lax.broadcasted_iota(I32, (tk, 1), 0)

    last_tile = ks_ref.shape[0] - 1

    def scores(j):
        return _dot(ks_ref[jnp.minimum(j, last_tile)], qt)

    def consume(j, s_ref, carry):
        m, l, acc, pv = carry
        jc = jnp.minimum(j, last_tile)
        rows = [jnp.broadcast_to(sel_ref[pl.ds(jc * bpt + b, 1), :], (SEL_BLOCK, tq))
                for b in range(bpt)]
        keep = (jnp.concatenate(rows, axis=0) > 0.5) & (j * tk + key_row <= t1)
        bias = jnp.where(keep, 0.0, NEG)
        s = s_ref[...] + jnp.concatenate([bias] * gq, axis=1)
        m_new = jnp.maximum(m, jnp.max(s, axis=0, keepdims=True))
        alpha = jnp.exp2(m - m_new)
        p = jnp.exp2(s - m_new)
        l = alpha * l + jnp.sum(p, axis=0, keepdims=True)
        return m_new, l, alpha * (acc + pv), _dot(vst_ref[jc], p.astype(BF16))

    def tile_pair(k, carry):
        j = 2 * k
        sb_ref[...] = scores(j + 1)
        carry = consume(j, sa_ref, carry)
        sa_ref[...] = scores(j + 2)
        return consume(j + 1, sb_ref, carry)

    sa_ref[...] = scores(0)
    _, l_s, a_s, pv_s = lax.fori_loop(0, (s0 // tk + 2) // 2, tile_pair, init + (zero_ref[...],))
    o_s = (a_s + pv_s) / l_s

    n_wt = (WINDOW + tq) // LANE
    j_lo = jnp.clip(i * (tq // LANE) - WINDOW // LANE, 0, kw_ref.shape[0] - n_wt)
    kwin = jnp.concatenate([kw_ref[j_lo + w] for w in range(n_wt)], axis=0)
    vwin = jnp.concatenate([vwt_ref[j_lo + w] for w in range(n_wt)], axis=1)
    sw = _dot(kwin, qt)
    diff = t1 - (j_lo * LANE + lax.broadcasted_iota(I32, (n_wt * LANE, 1), 0))
    sw = sw + jnp.concatenate([jnp.where((diff >= 0) & (diff < WINDOW), 0.0, NEG)] * gq, axis=1)
    pw = jnp.exp2(sw - jnp.max(sw, axis=0, keepdims=True))
    o_w = _dot(vwin, pw.astype(BF16)) / jnp.sum(pw, axis=0, keepdims=True)

    gt = g_ref[...]
    outs = []
    for g in range(gq):
        cs = slice(g * tq, (g + 1) * tq)
        outs.append(gt[3 * g:3 * g + 1, :] * o_c[:, cs] + gt[3 * g + 1:3 * g + 2, :] * o_s[:, cs]
                    + gt[3 * g + 2:3 * g + 3, :] * o_w[:, cs])
    o_ref[...] = jnp.concatenate(outs, axis=0).T


def _nsa(q, gates, kcmp, vcmp_t, ks, vs_t, kw, vw_t, ov_t, seq):
    bsz = q.shape[0]
    tq = min(TQ_NSA, seq)
    tk = min(TK_NSA, seq)
    n_sb = seq // SEL_BLOCK
    gq, hd = NSA_GQA, NSA_HEAD_DIM
    per_head = lambda a: pl.BlockSpec((None, None) + a.shape[2:],
                                      lambda b, h, i: (b, h) + (0,) * (a.ndim - 2),
                                      pipeline_mode=pl.Buffered(1))
    return pl.pallas_call(
        functools.partial(_nsa_kernel, tq=tq, tk=tk, top_n=min(SEL_TOP, n_sb)),
        grid=(bsz, NSA_KV_HEADS, seq // tq),
        in_specs=[pl.BlockSpec((None, gq * hd, tq), lambda b, h, i: (b, h, i)),
                  pl.BlockSpec((None, None, 3 * gq, tq), lambda b, h, i: (b, h, 0, i)),
                  per_head(kcmp), per_head(vcmp_t), per_head(ks), per_head(vs_t),
                  per_head(kw), per_head(vw_t), _resident(ov_t.shape)],
        out_specs=pl.BlockSpec((tq, gq * hd), lambda b, h, i: (b * (seq // tq) + i, h)),
        out_shape=jax.ShapeDtypeStruct((bsz * seq, NSA_Q), F32),
        scratch_shapes=[pltpu.VMEM((n_sb, tq), F32), pltpu.VMEM((hd, gq * tq), F32),
                        pltpu.VMEM((tk, gq * tq), F32), pltpu.VMEM((tk, gq * tq), F32)],
        compiler_params=_cparams(("parallel", "parallel", "arbitrary")),
        name="nsa",
    )(q, gates, kcmp, vcmp_t, ks, vs_t, kw, vw_t, ov_t)


def _dsa_kernel(h_ref, g_ref, q_ref, qi_ref, wi_ref, kcat_ref, ckt_ref, kidx_ref, wuv_ref, wout_ref,
                o_ref, ihi, ilo, acc_ref, m_ref, l_ref, tie_cap, *, tq, top):
    tk = tq
    n_h = MLA_HEADS
    i16 = jnp.int16
    i = pl.program_id(1)
    row = lax.broadcasted_iota(I32, (tk, 1), 0)
    col = lax.broadcasted_iota(I32, (1, tq), 1)
    causal = row <= col

    def tile(ref, j):
        return ref.at[pl.ds(pl.multiple_of(j * tk, tk), tk), :]

    qi = qi_ref[...]
    wi = wi_ref[...] * (IDX_HEADS ** -0.5) * (IDX_DIM ** -0.5)

    def idx_scores(j):
        kj = kidx_ref[j]
        sc = jnp.zeros((tk, tq), F32)
        for h in range(IDX_HEADS):
            d = _dot(kj, qi[h * IDX_DIM:(h + 1) * IDX_DIM, :])
            sc = sc + wi[h:h + 1, :] * jnp.maximum(d, 0.0)
        return sc

    def store_keys(j, sc):
        b = pltpu.bitcast(sc, I32)
        key = b ^ ((b >> 31) & 0x7FFFFFFF)
        tile(ihi, j)[...] = (key >> 16).astype(i16)
        tile(ilo, j)[...] = ((key & 0xFFFF) - 32768).astype(i16)

    def idx_tile(j, _):
        store_keys(j, idx_scores(j))
        return 0

    lax.fori_loop(0, i, idx_tile, 0)
    store_keys(i, jnp.where(causal, idx_scores(i), NEG))

    @pl.when(((i + 1) * tq) % kcat_ref.shape[1] != 0)
    def _():
        store_keys(i + 1, jnp.full((tk, tq), NEG, F32))

    def count(ref, pred):
        def body(j, c):
            hit = jnp.where(pred(tile(ref, j)[...]), jnp.ones((), i16), jnp.zeros((), i16))
            parts = [hit[k * 16:(k + 1) * 16] for k in range(tk // 16)]
            while len(parts) > 1:
                parts = [a + b for a, b in zip(parts[0::2], parts[1::2])]
            return c + parts[0]
        c16 = lax.fori_loop(0, i + 1, body, jnp.zeros((16, tq), i16))
        return jnp.sum(c16.astype(I32), axis=0, keepdims=True)

    def search(ref, need, n_all):
        def bit_step(b, st):
            thr, c_ge, c_gt = st
            cand = thr + (jnp.int32(1) << (15 - b))
            cand16 = cand.astype(i16)
            cnt = count(ref, lambda t: t >= cand16)
            ok = cnt >= need
            return jnp.where(ok, cand, thr), jnp.where(ok, cnt, c_ge), jnp.where(ok, c_gt, cnt)
        init = (jnp.full((1, tq), -32768, I32), n_all, jnp.zeros((1, tq), I32))
        return lax.fori_loop(0, 16, bit_step, init)

    thr_hi, n_ge_hi, n_gt_hi = search(ihi, top, jnp.full((1, tq), 1, I32) * ((i + 1) * tk))
    thr_hi16 = thr_hi.astype(i16)
    need = top - n_gt_hi
    n_eq_hi = n_ge_hi - n_gt_hi

    def mask_lo(j, _):
        lo = tile(ilo, j)
        lo[...] = jnp.where(tile(ihi, j)[...] == thr_hi16, lo[...], jnp.full((), -32768, i16))
        return 0

    lax.fori_loop(0, i + 1, mask_lo, 0)
    thr_lo, n_ge_lo, _ = search(ilo, need, n_eq_hi)

    n_sel = n_gt_hi + n_ge_lo
    tie_cap[...] = jnp.full((1, tq), ihi.shape[0], I32)
    excess = n_sel - top
    max_excess = jnp.max(excess)
    n_bits = (ihi.shape[0] - 1).bit_length()

    thr_lo16 = thr_lo.astype(i16)

    def tie_diff(j):
        return ((tile(ihi, j)[...] ^ thr_hi16) | (tile(ilo, j)[...] ^ thr_lo16)).astype(I32)

    n_top = 3
    max_scans = 2 * n_bits // n_top

    @pl.when((max_excess > 0) & (max_excess <= n_top * max_scans))
    def _():
        def drop(_, st):
            cap, left = st

            def body(j, tops):
                idx = j * tk + row
                v = jnp.where(tie_diff(j) == 0, jnp.where(idx <= cap, idx, -1), -1)
                v = v.reshape(tk // 8, 8, tq)
                for k in range(tk // 8):
                    t = v[k]
                    new = []
                    for m in tops:
                        new.append(jnp.maximum(m, t))
                        t = jnp.minimum(m, t)
                    tops = tuple(new)
                return tops

            tops = list(lax.fori_loop(0, i + 1, body, (jnp.full((8, tq), -1, I32),) * n_top))
            take = jnp.minimum(left, n_top)
            for r in range(n_top):
                g = jnp.max(functools.reduce(jnp.maximum, tops), axis=0, keepdims=True)
                cap = jnp.where(take == r + 1, g - 1, cap)
                tops = [jnp.where(m == g, -1, m) for m in tops]
            return cap, left - take

        n_scan = (max_excess + n_top - 1) // n_top
        tie_cap[...] = lax.fori_loop(0, n_scan, drop, (tie_cap[...], excess))[0]

    @pl.when(max_excess > n_top * max_scans)
    def _():
        def count_ties(below):
            def body(j, c):
                hit = jnp.where(tie_diff(j) == 0, jnp.where(j * tk + row < below, 1, 0), 0)
                return c + jnp.sum(hit.reshape(tk // 8, 8, tq), axis=0)
            c8 = lax.fori_loop(0, i + 1, body, jnp.zeros((8, tq), I32))
            return jnp.sum(c8, axis=0, keepdims=True)

        need_eq = count_ties(ihi.shape[0]) - excess

        def bit_step(b, cap):
            cand = cap + (jnp.int32(1) << (n_bits - 1 - b))
            return jnp.where(count_ties(cand) >= need_eq, cap, cand)

        tie_cap[...] = lax.fori_loop(0, n_bits, bit_step, jnp.zeros((1, tq), I32))

    m_ref[...] = jnp.full(m_ref.shape, NEG, F32)
    l_ref[...] = jnp.zeros(l_ref.shape, F32)
    acc_ref[...] = jnp.zeros(acc_ref.shape, F32)
    cap = tie_cap[...]
    thr_key = (thr_hi << 16) | (thr_lo + 32768)

    ta = kcat_ref.shape[1]
    a_row = lax.broadcasted_iota(I32, (ta, 1), 0)
    q_pos = i * tq + col

    def att_tile(j, _):
        rows = pl.ds(pl.multiple_of(j * ta, ta), ta)
        hi = ihi[rows, :].astype(I32)
        lo = ilo[rows, :].astype(I32)
        k_pos = j * ta + a_row
        key = (hi << 16) | (lo + 32768)
        sel = jnp.where(key > thr_key, 0.0,
                        jnp.where(key == thr_key, jnp.where(k_pos <= cap, 0.0, NEG), NEG))
        bias = jnp.where(k_pos <= q_pos, sel, NEG)
        s_all = _dot(kcat_ref[j], q_ref[...])
        ps, alphas = [], []
        for h in range(n_h):
            cs = slice(h * tq, (h + 1) * tq)
            s = s_all[:, cs] + bias
            m_old = m_ref[:, cs]
            m_new = jnp.maximum(m_old, jnp.max(s, axis=0, keepdims=True))
            alpha = jnp.exp2(m_old - m_new)
            p = jnp.exp2(s - m_new)
            l_ref[:, cs] = alpha * l_ref[:, cs] + jnp.sum(p, axis=0, keepdims=True)
            m_ref[:, cs] = m_new
            ps.append(p.astype(BF16))
            alphas.append(alpha)
        p_all = jnp.concatenate(ps, axis=1)
        acc_ref[...] = acc_ref[...] * jnp.concatenate(alphas, axis=1) + _dot(ckt_ref[j], p_all)
        return 0

    lax.fori_loop(0, (i * tq) // ta + 1, att_tile, 0)

    outs = []
    for h in range(n_h):
        cs = slice(h * tq, (h + 1) * tq)
        o = acc_ref[:, cs] / l_ref[:, cs]
        outs.append(_dot(wuv_ref[h], o.astype(BF16)))
    mix = jnp.concatenate(outs, axis=0).T
    y = _dot(mix.astype(BF16), wout_ref[...])
    o_ref[...] = h_ref[...] + _rms_rows(y, g_ref[3:4, :])


def _dsa(h, g, q, qi, wi, kcat, ckt, kidx, wuv_t, w_out, seq):
    bsz = q.shape[0]
    d = h.shape[1]
    tq = min(TQ_DSA, seq)
    nq = seq // tq
    top = min(DSA_TOP, seq // 4)
    wd = MLA_HEADS * tq
    per_b = lambda a: pl.BlockSpec((None,) + a.shape[1:], lambda b, i: (b,) + (0,) * (a.ndim - 1),
                                   pipeline_mode=pl.Buffered(1))
    return pl.pallas_call(
        functools.partial(_dsa_kernel, tq=tq, top=top),
        grid=(bsz, nq),
        in_specs=[pl.BlockSpec((tq, d), lambda b, i: (b * nq + i, 0)), _resident(g.shape),
                  pl.BlockSpec((None, None, MLA_QK, wd), lambda b, i: (b, i, 0, 0)),
                  pl.BlockSpec((None, IDX_HEADS * IDX_DIM, tq), lambda b, i: (b, 0, i)),
                  pl.BlockSpec((None, IDX_HEADS, tq), lambda b, i: (b, 0, i)),
                  per_b(kcat), per_b(ckt), per_b(kidx), _resident(wuv_t.shape),
                  _resident(w_out.shape)],
        out_specs=pl.BlockSpec((tq, d), lambda b, i: (b * nq + i, 0)),
        out_shape=jax.ShapeDtypeStruct(h.shape, F32),
        scratch_shapes=[pltpu.VMEM((seq, tq), jnp.int16),
                        pltpu.VMEM((seq, tq), jnp.int16),
                        pltpu.VMEM((MLA_LATENT, wd), F32),
                        pltpu.VMEM((1, wd), F32),
                        pltpu.VMEM((1, wd), F32),
                        pltpu.VMEM((1, tq), I32)],
        compiler_params=_cparams(("parallel", "arbitrary")),
        name="dsa",
    )(h, g, q, qi, wi, kcat, ckt, kidx, wuv_t, w_out)


def _rope_tables(seq, dim):
    half = dim // 2
    inv = ROPE_THETA ** (-jnp.arange(half, dtype=F32) / half)
    ang = inv[:, None] * jnp.arange(seq, dtype=F32)[None, :]
    return jnp.cos(ang), jnp.sin(ang)


def _even_mixer(h, g, bsz, seq, w_in, w_out, s5, pe_k, pe_v, wk1, wk2, wv1, wv2, tabs):
    cos32, sin32 = tabs[64]
    hd, hk = NSA_HEAD_DIM, NSA_KV_HEADS
    o = 0
    cols = {}
    for name, size in (("u", S5_WIDTH), ("q", NSA_Q), ("kc", NSA_KV), ("vc", NSA_KV), ("ks", NSA_KV),
                       ("vs", NSA_KV), ("kw", NSA_KV), ("vw", NSA_KV), ("gt", 3 * NSA_HEADS)):
        cols[name] = w_in[:, o:o + size]
        o += size
    wu = cols["u"].astype(BF16)
    wt = jnp.concatenate([cols[n] for n in ("q", "kc", "ks", "kw", "vc", "vs", "vw", "gt")], axis=1).T
    wt = jnp.pad(wt, ((0, (-wt.shape[0]) % 16), (0, 0))).astype(BF16)
    u, q_t, k_t, v_t, gt_t = _proj_even(h, g, wu, wt, cos32, sin32, bsz, seq)

    a_re, a_im, b_re, b_im, c_re, c_im, log_dt, d_skip, w_glu = s5
    mats = _s5_matrices(a_re, a_im, b_re, b_im, c_re, c_im, log_dt, S5_CHUNK)
    ys = _s5_scan(u, mats, bsz, seq)

    k5 = k_t.reshape(bsz, 3, hk, hd, seq)
    v5 = v_t.reshape(bsz, 3, hk, hd, seq)
    nb = seq // CMP_STRIDE
    half_blk = lambda a: (a.reshape(bsz, hk, hd, nb, CMP_STRIDE).transpose(0, 1, 3, 4, 2)
                          .reshape(bsz, hk, nb, CMP_STRIDE * hd))
    pe2 = lambda pe: pe.reshape(2, CMP_STRIDE * hd)
    w1s = lambda w: w.reshape(2, CMP_STRIDE * hd, hd).astype(BF16)
    kcmp = _compress(half_blk(k5[:, 0]), pe2(pe_k), w1s(wk1), wk2.astype(BF16))
    vcmp = _compress(half_blk(v5[:, 0]), pe2(pe_v), w1s(wv1), wv2.astype(BF16))
    vcmp_t = vcmp.transpose(0, 1, 3, 2)
    tk = min(TK_NSA, seq)
    ks = k5[:, 1].reshape(bsz, hk, hd, seq // tk, tk).transpose(0, 1, 3, 4, 2)
    vs_t = v5[:, 1].reshape(bsz, hk, hd, seq // tk, tk).transpose(0, 1, 3, 2, 4)
    kw = k5[:, 2].reshape(bsz, hk, hd, seq // LANE, LANE).transpose(0, 1, 3, 4, 2)
    vw_t = v5[:, 2].reshape(bsz, hk, hd, seq // LANE, LANE).transpose(0, 1, 3, 2, 4)
    n_sb = seq // SEL_BLOCK
    c_start = jnp.arange(nb) * CMP_STRIDE
    b_start = jnp.arange(n_sb) * SEL_BLOCK
    ov_t = ((c_start[None, :] < b_start[:, None] + SEL_BLOCK)
            & (c_start[None, :] + CMP_LEN > b_start[:, None])
            & (jnp.arange(nb)[None, :] < nb - 1)).astype(BF16)
    gates = gt_t.reshape(bsz, hk, 3 * NSA_GQA, seq)
    b_out = _nsa(q_t, gates, kcmp, vcmp_t, ks, vs_t, kw, vw_t, ov_t, seq)
    return _outproj_even(h, ys, u, b_out, g, d_skip.reshape(1, -1), w_glu.astype(BF16),
                         w_out[:S5_WIDTH].astype(BF16), w_out[S5_WIDTH:].astype(BF16))


def _odd_mixer(h, g, bsz, seq, w_in, kv_norm, w_uv, w_out, tabs):
    cos32, sin32 = tabs[64]
    cos16, sin16 = tabs[32]
    d = h.shape[1]
    sizes = (MLA_HEADS * MLA_LATENT, MLA_HEADS * MLA_ROPE, MLA_LATENT, MLA_ROPE,
             IDX_HEADS * IDX_DIM, IDX_DIM, IDX_HEADS)
    parts = []
    o = 0
    for s in sizes:
        parts.append(w_in[:, o:o + s])
        o += s
    w_ql, w_qr, w_c, w_kr, w_qi, w_ki, w_wi = parts
    wq = jnp.concatenate([w_ql.reshape(d, MLA_HEADS, MLA_LATENT), w_qr.reshape(d, MLA_HEADS, MLA_ROPE)],
                         axis=2).reshape(d, MLA_HEADS * MLA_QK).T.astype(BF16)
    wkv = jnp.concatenate([w_c, w_kr, w_qi, w_ki, w_wi], axis=1).T
    wkv = jnp.pad(wkv, ((0, (-wkv.shape[0]) % 16), (0, 0))).astype(BF16)
    ckv_t, kr_t, qi_t, ki_t, wi_t = _proj_odd_kv(h, g, wkv, kv_norm.reshape(-1, 1), cos32, sin32,
                                                 cos16, sin16, bsz, seq)
    q = _proj_odd_q(h, g, wq, cos16, sin16, bsz, seq)
    tk = min(TQ_DSA, seq)
    ta = min(TA_DSA, seq)
    kcat = jnp.concatenate([ckv_t, kr_t], axis=1).transpose(0, 2, 1).reshape(bsz, seq // ta, ta, MLA_QK)
    ckt = ckv_t.reshape(bsz, MLA_LATENT, seq // ta, ta).transpose(0, 2, 1, 3)
    kidx = ki_t.transpose(0, 2, 1).reshape(bsz, seq // tk, tk, IDX_DIM)
    wuv_t = w_uv.transpose(0, 2, 1).astype(BF16)
    return _dsa(h, g, q, qi_t, wi_t, kcat, ckt, kidx, wuv_t, w_out.astype(BF16), seq)


def kernel(x, p, norm_g, ffn1_w_in, ffn1_w_out, ffn2_w_in, ffn2_w_out, ple_w_gate, ple_w_proj,
           ev_w_in, ev_w_out, s5_a_re, s5_a_im, s5_b_re, s5_b_im, s5_c_re, s5_c_im, s5_log_dt, s5_d,
           s5_w_glu, nsa_pe_k, nsa_pe_v, nsa_wk1, nsa_wk2, nsa_wv1, nsa_wv2,
           od_w_in, od_kv_norm, od_w_uv, od_w_out):
    bsz, seq, d = x.shape
    depth = norm_g.shape[0]
    tabs = {64: _rope_tables(seq, 64), 32: _rope_tables(seq, 32)}
    h = x.reshape(bsz * seq, d)
    for i in range(depth):
        g = norm_g[i]
        j = i // 2
        h = _ffn(h, g, ffn1_w_in[i].astype(BF16), ffn1_w_out[i].astype(BF16), 0)
        if i % 2 == 0:
            s5 = (s5_a_re[j], s5_a_im[j], s5_b_re[j], s5_b_im[j], s5_c_re[j], s5_c_im[j],
                  s5_log_dt[j], s5_d[j], s5_w_glu[j])
            h = _even_mixer(h, g, bsz, seq, ev_w_in[j], ev_w_out[j], s5, nsa_pe_k[j], nsa_pe_v[j],
                            nsa_wk1[j], nsa_wk2[j], nsa_wv1[j], nsa_wv2[j], tabs)
        else:
            h = _odd_mixer(h, g, bsz, seq, od_w_in[j], od_kv_norm[j], od_w_uv[j], od_w_out[j], tabs)
        h = _ffn_ple(h, p[i].reshape(bsz * seq, -1), g, ffn2_w_in[i].astype(BF16),
                     ffn2_w_out[i].astype(BF16), ple_w_gate[i].astype(BF16),
                     ple_w_proj[i].astype(BF16))
    return h.reshape(bsz, seq, d)
```
